```python
import jax, jax.numpy as jnp
from jax import lax
import numpy as np

D_MODEL = 2048
BATCH = 8
SEQ = 8192
DEPTH = 4

HEAD_DIM = 128
GRID_W = 64
BLOCK = 128
EPS = 1e-6
NEG_INF = -1e30
NA_HEADS = 8
NA_WIN_R = 8
NA_WIN_C = 16
GQ_HEADS = 8
GKV_HEADS = 2
ROPE_BASE = 10000.0
DIL_GROUPS = ((128, 1), (512, 4), (2048, 16))
DIL_HEADS_PER_GROUP = 4
DIL_HEADS = 12
ALIBI_MAX_EXP = 8.0
WA = NA_HEADS * HEAD_DIM
WB_Q = GQ_HEADS * HEAD_DIM
WB_KV = GKV_HEADS * HEAD_DIM
WC = DIL_HEADS * HEAD_DIM
WC_OUT = DIL_HEADS_PER_GROUP * HEAD_DIM
SPLIT_SIZES = (WA, WA, WA, WB_Q, WB_KV, WB_KV, WC, WC, WC, WA, WB_Q, WC_OUT, 3 * D_MODEL)
N_IN = WA * 4 + WB_Q * 2 + WB_KV * 2 + WC * 3 + WC_OUT + 3 * D_MODEL

kernel_name = "hybrid_gated_parallel_encoder"


def rmsnorm(x, g):
    x32 = x.astype(jnp.float32)
    y = x32 * lax.rsqrt(jnp.mean(x32 * x32, axis=-1, keepdims=True) + EPS)
    return (y * g.astype(jnp.float32)).astype(x.dtype)


def rope_1d(x, pos):
    half = x.shape[-1] // 2
    freqs = ROPE_BASE ** (-jnp.arange(half, dtype=jnp.float32) / half)
    ang = pos.astype(jnp.float32)[:, None] * freqs[None, :]
    cos = jnp.cos(ang)[None, :, None, :].astype(x.dtype)
    sin = jnp.sin(ang)[None, :, None, :].astype(x.dtype)
    x1, x2 = x[..., :half], x[..., half:]
    return jnp.concatenate([x1 * cos - x2 * sin, x1 * sin + x2 * cos], axis=-1)


def axial_rope(x):
    t = jnp.arange(x.shape[1])
    half = x.shape[-1] // 2
    return jnp.concatenate([rope_1d(x[..., :half], t // GRID_W),
                            rope_1d(x[..., half:], t % GRID_W)], axis=-1)


def neighbourhood_attention(q, k, v, rpb):
    bsz, seq, heads, e = q.shape
    rows = seq // GRID_W
    kr = min(NA_WIN_R, rows)
    qg = q.reshape(bsz, rows, GRID_W, heads, e)
    kg = k.reshape(bsz, rows, GRID_W, heads, e)
    vg = v.reshape(bsz, rows, GRID_W, heads, e)
    j = jnp.arange(GRID_W)
    c = jnp.arange(GRID_W)
    cs = jnp.clip(j - NA_WIN_C // 2, 0, GRID_W - NA_WIN_C)
    col_ok = (c[None, :] >= cs[:, None]) & (c[None, :] < cs[:, None] + NA_WIN_C)
    dc_idx = jnp.clip(c[None, :] - j[:, None] + NA_WIN_C - 1, 0, 2 * NA_WIN_C - 2)
    scale = e ** -0.5

    def row_block(r):
        start = jnp.clip(r - kr // 2, 0, rows - kr)
        q_r = lax.dynamic_index_in_dim(qg, r, axis=1, keepdims=False)
        k_r = lax.dynamic_slice_in_dim(kg, start, kr, axis=1)
        v_r = lax.dynamic_slice_in_dim(vg, start, kr, axis=1)
        dr_idx = start + jnp.arange(kr) - r + NA_WIN_R - 1
        bias = rpb[:, dr_idx[None, :, None], dc_idx[:, None, :]]
        s = jnp.einsum('bjhe,bkche->bhjkc', q_r, k_r,
                       preferred_element_type=jnp.float32) * scale + bias.astype(jnp.float32)[None]
        s = jnp.where(col_ok[:, None, :], s, NEG_INF)
        p = jax.nn.softmax(s.reshape(bsz, heads, GRID_W, kr * GRID_W), axis=-1).reshape(s.shape)
        return jnp.einsum('bhjkc,bkche->bjhe', p.astype(v.dtype), v_r)

    o = lax.map(row_block, jnp.arange(rows))
    return o.transpose(1, 0, 2, 3, 4).reshape(bsz, seq, heads * e)


def gqa_attention(q, k, v):
    bsz, seq, hq, e = q.shape
    hkv = k.shape[2]
    grp = hq // hkv
    nb = seq // BLOCK
    scale = e ** -0.5
    qb = q.reshape(bsz, nb, BLOCK, hkv, grp, e).transpose(1, 0, 2, 3, 4, 5)

    def block(qi):
        s = jnp.einsum('bqkge,bske->bkgqs', qi, k, preferred_element_type=jnp.float32) * scale
        p = jax.nn.softmax(s, axis=-1)
        return jnp.einsum('bkgqs,bske->bqkge', p.astype(v.dtype), v)

    o = lax.map(block, qb)
    return o.transpose(1, 0, 2, 3, 4, 5).reshape(bsz, seq, hq * e)


def dilated_group(q, k, v, dil, reach, slopes):
    bsz, seq, hg, e = q.shape
    length = seq // dil
    nb = -(-length // BLOCK)
    lp = nb * BLOCK
    kb_len = BLOCK + 2 * reach
    scale = e ** -0.5

    def to_sub(t):
        return t.reshape(bsz, length, dil, hg, e).transpose(0, 2, 3, 1, 4)

    qs = jnp.pad(to_sub(q), ((0, 0), (0, 0), (0, 0), (0, lp - length), (0, 0)))
    qs = qs.reshape(bsz, dil, hg, nb, BLOCK, e)
    pad_kv = ((0, 0), (0, 0), (0, 0), (reach, lp - length + reach), (0, 0))
    ks = jnp.pad(to_sub(k), pad_kv)
    vs = jnp.pad(to_sub(v), pad_kv)
    idx = jnp.arange(nb)[:, None] * BLOCK + jnp.arange(kb_len)[None, :]
    kbk = ks[:, :, :, idx]
    kbv = vs[:, :, :, idx]
    lq = jnp.arange(nb)[:, None] * BLOCK + jnp.arange(BLOCK)[None, :]
    lk = idx - reach
    dist = jnp.abs(lq[:, :, None] - lk[:, None, :])
    valid = (dist <= reach) & ((lk >= 0) & (lk < length))[:, None, :]
    s = jnp.einsum('bdhnqe,bdhnke->bdhnqk', qs, kbk, preferred_element_type=jnp.float32) * scale
    s = s - slopes[:, None, None, None] * (dil * dist).astype(jnp.float32)
    s = jnp.where(valid, s, NEG_INF)
    lse = jax.nn.logsumexp(s, axis=-1)
    p = jnp.exp(s - lse[..., None])
    o = jnp.einsum('bdhnqk,bdhnke->bdhnqe', p.astype(v.dtype), kbv)
    o = o.reshape(bsz, dil, hg, lp, e)[:, :, :, :length]
    o = o.transpose(0, 3, 1, 2, 4).reshape(bsz, seq, hg, e)
    lse = lse.reshape(bsz, dil, hg, lp)[..., :length].transpose(0, 3, 1, 2).reshape(bsz, seq, hg)
    return o, lse


def dilated_mixture(q, k, v):
    bsz, seq = q.shape[0], q.shape[1]
    slopes = 2.0 ** (-ALIBI_MAX_EXP * jnp.arange(1, DIL_HEADS + 1, dtype=jnp.float32) / DIL_HEADS)
    outs, lses = [], []
    for g, (win, dil) in enumerate(DIL_GROUPS):
        sl = slice(g * DIL_HEADS_PER_GROUP, (g + 1) * DIL_HEADS_PER_GROUP)
        o, l = dilated_group(q[:, :, sl], k[:, :, sl], v[:, :, sl], dil, (win // 2) // dil, slopes[sl])
        outs.append(o)
        lses.append(l)
    wts = jax.nn.softmax(jnp.stack(lses, axis=-1), axis=-1)
    o = jnp.sum(jnp.stack(outs, axis=3) * wts[..., None].astype(q.dtype), axis=3)
    return o.reshape(bsz, seq, WC_OUT)


def _fwd_setup_inputs(seed: int = 0) -> dict:
    key = jax.random.key(seed)
    ks = jax.random.split(key, 12)
    f32 = jnp.float32
    return {
        "x": jax.random.normal(ks[0], (BATCH, SEQ, D_MODEL), f32),
        "pre_norm_g": 1.0 + 0.02 * jax.random.normal(ks[1], (DEPTH, D_MODEL), f32),
        "w_in": jax.random.normal(ks[2], (DEPTH, D_MODEL, N_IN), f32) * D_MODEL ** -0.5,
        "b_gate": 0.1 * jax.random.normal(ks[3], (DEPTH, 3 * D_MODEL), f32),
        "q_norm_g": 1.0 + 0.02 * jax.random.normal(ks[4], (DEPTH, HEAD_DIM), f32),
        "k_norm_g": 1.0 + 0.02 * jax.random.normal(ks[5], (DEPTH, HEAD_DIM), f32),
        "rpb": 0.1 * jax.random.normal(ks[6], (DEPTH, NA_HEADS, 2 * NA_WIN_R - 1, 2 * NA_WIN_C - 1), f32),
        "w_branch_a": jax.random.normal(ks[7], (DEPTH, WA, D_MODEL), f32) * WA ** -0.5,
        "w_branch_b": jax.random.normal(ks[8], (DEPTH, WB_Q, D_MODEL), f32) * WB_Q ** -0.5,
        "w_branch_c": jax.random.normal(ks[9], (DEPTH, WC_OUT, D_MODEL), f32) * WC_OUT ** -0.5,
        "w_out": jax.random.normal(ks[10], (DEPTH, D_MODEL, D_MODEL), f32) * D_MODEL ** -0.5,
        "post_norm_g": 1.0 + 0.02 * jax.random.normal(ks[11], (DEPTH, D_MODEL), f32),
    }


def _fwd_reference(x, pre_norm_g, w_in, b_gate, q_norm_g, k_norm_g, rpb, w_branch_a, w_branch_b,
              w_branch_c, w_out, post_norm_g):
    bsz, seq = x.shape[0], x.shape[1]
    split_points = [int(p) for p in np.cumsum(SPLIT_SIZES)[:-1]]

    def heads(t, n):
        return t.reshape(bsz, seq, n, HEAD_DIM)

    for l in range(DEPTH):
        h = rmsnorm(x, pre_norm_g[l])
        proj = jnp.einsum('bsd,dn->bsn', h, w_in[l])
        (qa, ka, va, qb, kb, vb, qc, kc, vc, za, zb, zc, gates) = jnp.split(proj, split_points, axis=-1)
        ya = neighbourhood_attention(heads(qa, NA_HEADS), heads(ka, NA_HEADS), heads(va, NA_HEADS), rpb[l])
        qb_h = axial_rope(rmsnorm(heads(qb, GQ_HEADS), q_norm_g[l]))
        kb_h = axial_rope(rmsnorm(heads(kb, GKV_HEADS), k_norm_g[l]))
        yb = gqa_attention(qb_h, kb_h, heads(vb, GKV_HEADS))
        yc = dilated_mixture(heads(qc, DIL_HEADS), heads(kc, DIL_HEADS), heads(vc, DIL_HEADS))
        g = jax.nn.sigmoid((gates + b_gate[l]).astype(jnp.float32)).astype(x.dtype)
        ga, gb, gc = jnp.split(g, 3, axis=-1)
        merged = (ga * jnp.einsum('bsw,wd->bsd', ya * jax.nn.silu(za), w_branch_a[l])
                  + gb * jnp.einsum('bsw,wd->bsd', yb * jax.nn.silu(zb), w_branch_b[l])
                  + gc * jnp.einsum('bsw,wd->bsd', yc * jax.nn.silu(zc), w_branch_c[l]))
        out = jnp.einsum('bsd,de->bse', merged, w_out[l])
        x = x + rmsnorm(out, post_norm_g[l])
    return x


import jax as _jax
import jax.numpy as _jnp

TWIN_FORMAT = 'train_step'
FWD_PARAMS = ['x', 'pre_norm_g', 'w_in', 'b_gate', 'q_norm_g', 'k_norm_g', 'rpb', 'w_branch_a', 'w_branch_b', 'w_branch_c', 'w_out', 'post_norm_g']
TWIN_WEIGHTS = ['pre_norm_g', 'w_in', 'b_gate', 'q_norm_g', 'k_norm_g', 'rpb', 'w_branch_a', 'w_branch_b', 'w_branch_c', 'w_out', 'post_norm_g']
TWIN_DIFF_INPUT = 'x'
TWIN_INPUTS = ['x', 'pre_norm_g', 'w_in', 'b_gate', 'q_norm_g', 'k_norm_g', 'rpb', 'w_branch_a', 'w_branch_b', 'w_branch_c', 'w_out', 'post_norm_g', 'loss_target', 'm_pre_norm_g', 'm_w_in', 'm_b_gate', 'm_q_norm_g', 'm_k_norm_g', 'm_rpb', 'm_w_branch_a', 'm_w_branch_b', 'm_w_branch_c', 'm_w_out', 'm_post_norm_g', 'v_pre_norm_g', 'v_w_in', 'v_b_gate', 'v_q_norm_g', 'v_k_norm_g', 'v_rpb', 'v_w_branch_a', 'v_w_branch_b', 'v_w_branch_c', 'v_w_out', 'v_post_norm_g']
TWIN_OUTPUTS = ['loss', 'grad_x', 'grad_pre_norm_g', 'grad_w_in', 'grad_b_gate', 'grad_q_norm_g', 'grad_k_norm_g', 'grad_rpb', 'grad_w_branch_a', 'grad_w_branch_b', 'grad_w_branch_c', 'grad_w_out', 'grad_post_norm_g', 'delta_pre_norm_g', 'delta_w_in', 'delta_b_gate', 'delta_q_norm_g', 'delta_k_norm_g', 'delta_rpb', 'delta_w_branch_a', 'delta_w_branch_b', 'delta_w_branch_c', 'delta_w_out', 'delta_post_norm_g', 'new_m_pre_norm_g', 'new_m_w_in', 'new_m_b_gate', 'new_m_q_norm_g', 'new_m_k_norm_g', 'new_m_rpb', 'new_m_w_branch_a', 'new_m_w_branch_b', 'new_m_w_branch_c', 'new_m_w_out', 'new_m_post_norm_g', 'new_v_pre_norm_g', 'new_v_w_in', 'new_v_b_gate', 'new_v_q_norm_g', 'new_v_k_norm_g', 'new_v_rpb', 'new_v_w_branch_a', 'new_v_w_branch_b', 'new_v_w_branch_c', 'new_v_w_out', 'new_v_post_norm_g']
TWIN_LEAF_KINDS = {'loss': 'loss', 'grad_x': 'grad_x', 'grad_pre_norm_g': 'grad_w', 'grad_w_in': 'grad_w', 'grad_b_gate': 'grad_w', 'grad_q_norm_g': 'grad_w', 'grad_k_norm_g': 'grad_w', 'grad_rpb': 'grad_w', 'grad_w_branch_a': 'grad_w', 'grad_w_branch_b': 'grad_w', 'grad_w_branch_c': 'grad_w', 'grad_w_out': 'grad_w', 'grad_post_norm_g': 'grad_w', 'delta_pre_norm_g': 'delta_w', 'delta_w_in': 'delta_w', 'delta_b_gate': 'delta_w', 'delta_q_norm_g': 'delta_w', 'delta_k_norm_g': 'delta_w', 'delta_rpb': 'delta_w', 'delta_w_branch_a': 'delta_w', 'delta_w_branch_b': 'delta_w', 'delta_w_branch_c': 'delta_w', 'delta_w_out': 'delta_w', 'delta_post_norm_g': 'delta_w', 'new_m_pre_norm_g': 'new_m', 'new_m_w_in': 'new_m', 'new_m_b_gate': 'new_m', 'new_m_q_norm_g': 'new_m', 'new_m_k_norm_g': 'new_m', 'new_m_rpb': 'new_m', 'new_m_w_branch_a': 'new_m', 'new_m_w_branch_b': 'new_m', 'new_m_w_branch_c': 'new_m', 'new_m_w_out': 'new_m', 'new_m_post_norm_g': 'new_m', 'new_v_pre_norm_g': 'new_v', 'new_v_w_in': 'new_v', 'new_v_b_gate': 'new_v', 'new_v_q_norm_g': 'new_v', 'new_v_k_norm_g': 'new_v', 'new_v_rpb': 'new_v', 'new_v_w_branch_a': 'new_v', 'new_v_w_branch_b': 'new_v', 'new_v_w_branch_c': 'new_v', 'new_v_w_out': 'new_v', 'new_v_post_norm_g': 'new_v'}


def _forward(args):
    return _fwd_reference(*[args[k] for k in FWD_PARAMS])


def _output_shape():
    def fwd():
        inp = _fwd_setup_inputs(0)
        return _fwd_reference(*[inp[k] for k in FWD_PARAMS])
    out = _jax.eval_shape(fwd)
    return out.shape, out.dtype

N_MICROBATCH = 1
ADAM_LR = 0.001
ADAM_B1 = 0.9
ADAM_B2 = 0.999
ADAM_EPS = 1e-08
ADAM_WD = 0.01
ADAM_STEP = 10
PER_EXAMPLE_BATCH_AXIS = {'x': 0, 'loss_target': 0}
SHARED_INPUTS = []
_WEIGHT_DTYPES = {'pre_norm_g': _jnp.float32, 'w_in': _jnp.float32, 'b_gate': _jnp.float32, 'q_norm_g': _jnp.float32, 'k_norm_g': _jnp.float32, 'rpb': _jnp.float32, 'w_branch_a': _jnp.float32, 'w_branch_b': _jnp.float32, 'w_branch_c': _jnp.float32, 'w_out': _jnp.float32, 'post_norm_g': _jnp.float32}
MOMENT_SCALE = {'pre_norm_g': 1.153458e+00, 'w_in': 4.034522e-01, 'b_gate': 1.515614e-01, 'q_norm_g': 6.876242e-01, 'k_norm_g': 6.770633e-01, 'rpb': 2.400432e-01, 'w_branch_a': 3.878331e-01, 'w_branch_b': 2.302810e-01, 'w_branch_c': 5.596729e-01, 'w_out': 6.645336e-01, 'post_norm_g': 3.190309e+01}


def _to_microbatches(a, axis):
    t = _jnp.moveaxis(a, axis, 0)
    t = t.reshape((N_MICROBATCH, t.shape[0] // N_MICROBATCH) + t.shape[1:])
    return _jnp.moveaxis(t, 1, axis + 1)


def setup_inputs(seed: int = 0) -> dict:
    inp = _fwd_setup_inputs(seed)
    key = _jax.random.fold_in(_jax.random.key(seed), 7919)
    shape, _ = _output_shape()
    out = dict(inp)
    out["loss_target"] = _jax.random.normal(_jax.random.fold_in(key, 0), shape, _jnp.float32)
    for i, name in enumerate(TWIN_WEIGHTS):
        w = inp[name].astype(_jnp.float32)
        if MOMENT_SCALE is None:
            s = _jnp.sqrt(_jnp.mean(_jnp.square(w)) + 1e-30)
        else:
            s = MOMENT_SCALE[name]
        km, kv = _jax.random.split(_jax.random.fold_in(key, i + 1))
        out[name] = w
        out["m_" + name] = s * _jax.random.normal(km, w.shape, _jnp.float32)
        out["v_" + name] = (s * s) * _jax.random.uniform(kv, w.shape, _jnp.float32, 0.5, 1.5)
    if N_MICROBATCH > 1:
        for name, axis in PER_EXAMPLE_BATCH_AXIS.items():
            out[name] = _to_microbatches(out[name], axis)
    return {'x': out['x'], 'pre_norm_g': out['pre_norm_g'], 'w_in': out['w_in'], 'b_gate': out['b_gate'], 'q_norm_g': out['q_norm_g'], 'k_norm_g': out['k_norm_g'], 'rpb': out['rpb'], 'w_branch_a': out['w_branch_a'], 'w_branch_b': out['w_branch_b'], 'w_branch_c': out['w_branch_c'], 'w_out': out['w_out'], 'post_norm_g': out['post_norm_g'], 'loss_target': out['loss_target'], 'm_pre_norm_g': out['m_pre_norm_g'], 'm_w_in': out['m_w_in'], 'm_b_gate': out['m_b_gate'], 'm_q_norm_g': out['m_q_norm_g'], 'm_k_norm_g': out['m_k_norm_g'], 'm_rpb': out['m_rpb'], 'm_w_branch_a': out['m_w_branch_a'], 'm_w_branch_b': out['m_w_branch_b'], 'm_w_branch_c': out['m_w_branch_c'], 'm_w_out': out['m_w_out'], 'm_post_norm_g': out['m_post_norm_g'], 'v_pre_norm_g': out['v_pre_norm_g'], 'v_w_in': out['v_w_in'], 'v_b_gate': out['v_b_gate'], 'v_q_norm_g': out['v_q_norm_g'], 'v_k_norm_g': out['v_k_norm_g'], 'v_rpb': out['v_rpb'], 'v_w_branch_a': out['v_w_branch_a'], 'v_w_branch_b': out['v_w_branch_b'], 'v_w_branch_c': out['v_w_branch_c'], 'v_w_out': out['v_w_out'], 'v_post_norm_g': out['v_post_norm_g']}


def _loss(weights, diff, rest, loss_target):
    with _jax.named_scope("forward"):
        args = {**rest, TWIN_DIFF_INPUT: diff, **{k: w.astype(_WEIGHT_DTYPES[k]) for k, w in weights.items()}}
        y = _forward(args)
    with _jax.named_scope("loss_head"):
        err = _jnp.square(y.astype(_jnp.float32) - loss_target)
        return 0.5 * _jnp.sum(_jnp.mean(err, axis=-1)) if err.ndim else 0.5 * err


def _adamw(w, g, m, v):
    m = ADAM_B1 * m + (1.0 - ADAM_B1) * g
    v = ADAM_B2 * v + (1.0 - ADAM_B2) * _jnp.square(g)
    m_hat = m / (1.0 - ADAM_B1 ** ADAM_STEP)
    v_hat = v / (1.0 - ADAM_B2 ** ADAM_STEP)
    delta = -ADAM_LR * (m_hat / (_jnp.sqrt(v_hat) + ADAM_EPS) + ADAM_WD * w)
    return delta, m, v


def reference(x, pre_norm_g, w_in, b_gate, q_norm_g, k_norm_g, rpb, w_branch_a, w_branch_b, w_branch_c, w_out, post_norm_g, loss_target, m_pre_norm_g, m_w_in, m_b_gate, m_q_norm_g, m_k_norm_g, m_rpb, m_w_branch_a, m_w_branch_b, m_w_branch_c, m_w_out, m_post_norm_g, v_pre_norm_g, v_w_in, v_b_gate, v_q_norm_g, v_k_norm_g, v_rpb, v_w_branch_a, v_w_branch_b, v_w_branch_c, v_w_out, v_post_norm_g):
    given = dict(x=x, pre_norm_g=pre_norm_g, w_in=w_in, b_gate=b_gate, q_norm_g=q_norm_g, k_norm_g=k_norm_g, rpb=rpb, w_branch_a=w_branch_a, w_branch_b=w_branch_b, w_branch_c=w_branch_c, w_out=w_out, post_norm_g=post_norm_g, loss_target=loss_target, m_pre_norm_g=m_pre_norm_g, m_w_in=m_w_in, m_b_gate=m_b_gate, m_q_norm_g=m_q_norm_g, m_k_norm_g=m_k_norm_g, m_rpb=m_rpb, m_w_branch_a=m_w_branch_a, m_w_branch_b=m_w_branch_b, m_w_branch_c=m_w_branch_c, m_w_out=m_w_out, m_post_norm_g=m_post_norm_g, v_pre_norm_g=v_pre_norm_g, v_w_in=v_w_in, v_b_gate=v_b_gate, v_q_norm_g=v_q_norm_g, v_k_norm_g=v_k_norm_g, v_rpb=v_rpb, v_w_branch_a=v_w_branch_a, v_w_branch_b=v_w_branch_b, v_w_branch_c=v_w_branch_c, v_w_out=v_w_out, v_post_norm_g=v_post_norm_g)
    weights = {n: given[n] for n in TWIN_WEIGHTS}
    shared = {n: given[n] for n in SHARED_INPUTS}
    per_example = {n: given[n] for n in ['x']}
    grad_fn = _jax.value_and_grad(_loss, argnums=(0, 1))

    def one_microbatch(ex, loss_target):
        ex = dict(ex)
        diff = ex.pop(TWIN_DIFF_INPUT)
        return grad_fn(weights, diff, {**shared, **ex}, loss_target)

    if N_MICROBATCH == 1:
        loss, (grad_w, grad_x) = one_microbatch(per_example, given["loss_target"])
    else:
        def body(carry, xs):
            loss_sum, grad_sum = carry
            l_k, (gw_k, gx_k) = one_microbatch(xs[0], xs[1])
            with _jax.named_scope("update"):
                return (loss_sum + l_k, _jax.tree.map(_jnp.add, grad_sum, gw_k)), gx_k

        init = (_jnp.zeros((), _jnp.float32), _jax.tree.map(_jnp.zeros_like, weights))
        (loss, grad_w), grad_x = _jax.lax.scan(body, init, (per_example, given["loss_target"]))
    with _jax.named_scope("update"):
        delta_w, new_m, new_v = {}, {}, {}
        for n in TWIN_WEIGHTS:
            delta_w[n], new_m[n], new_v[n] = _adamw(weights[n], grad_w[n], given["m_" + n], given["v_" + n])
    return (loss, grad_x, *[grad_w[n] for n in TWIN_WEIGHTS], *[delta_w[n] for n in TWIN_WEIGHTS],
            *[new_m[n] for n in TWIN_WEIGHTS], *[new_v[n] for n in TWIN_WEIGHTS])
```

```python
import functools

import numpy as np
import jax
import jax.numpy as jnp
from jax import lax
from jax.experimental import pallas as pl
from jax.experimental.pallas import tpu as pltpu

F32 = jnp.float32
BF16 = jnp.bfloat16
MESH = pl.DeviceIdType.MESH

HEAD = 128
GRID_W = 64
EPS = 1e-6
NEG = -1e30
NA_HEADS = 8
NA_WIN_R = 8
NA_WIN_C = 16
GQ_HEADS = 8
GKV_HEADS = 2
ROPE_BASE = 10000.0
DIL_GROUPS = ((128, 1), (512, 4), (2048, 16))
DIL_REACH = 64
DIL_HPG = 4
DIL_HEADS = 12
ALIBI_MAX_EXP = 8.0
ADAM_LR = 0.001
ADAM_B1 = 0.9
ADAM_B2 = 0.999
ADAM_EPS = 1e-08
ADAM_WD = 0.01
ADAM_STEP = 10

C_QA, C_KA, C_VA = 0, 8, 16
C_QB, C_KB, C_VB = 24, 32, 34
C_QC, C_KC, C_VC = 36, 48, 60
C_ZA, C_ZB, C_ZC = 72, 80, 88
C_GATE = 92

NA_QR = 4
NA_KR = NA_QR + NA_WIN_R
NA_VARIANTS = 3

VMEM_LIMIT_BYTES = 48 * 1024 * 1024
LANE = 128
N_CHIPS = 4
N_DEV = 8


def _params(sem=None):
    kw = dict(vmem_limit_bytes=VMEM_LIMIT_BYTES)
    if sem is not None:
        kw["dimension_semantics"] = sem
    return pltpu.CompilerParams(**kw)


def _tile(n, cap):
    units = n // LANE
    assert units * LANE == n, n
    best = 1
    for d in range(1, units + 1):
        if units % d == 0 and d * LANE <= cap:
            best = d
    return best * LANE


def _ew(name, fn, ins, outs, accs=(), *, rows, width, ncols, ts):
    ts = min(ts, rows)
    assert rows % ts == 0
    nr = rows // ts
    ni, no = len(ins), len(outs)

    def imap(mode, off):
        if mode == "t":
            return lambda n, i: (i, off + n)
        if mode == "tf":
            return lambda n, i: (i, off)
        if mode == "v":
            return lambda n, i: (0, off + n)
        return lambda n, i: (0, off)

    in_specs = [pl.BlockSpec((ts if m in ("t", "tf") else 1, width), imap(m, o)) for (_, m, o) in ins]
    out_specs = [pl.BlockSpec((ts, width), lambda n, i: (i, n)) for _ in outs]
    out_shape = [jax.ShapeDtypeStruct((rows, c), d) for (c, d) in outs]
    for a in accs:
        if a == "n":
            out_specs.append(pl.BlockSpec((8, width), lambda n, i: (0, n)))
            out_shape.append(jax.ShapeDtypeStruct((8, width * ncols), F32))
        else:
            out_specs.append(pl.BlockSpec((8, width), lambda n, i: (0, 0)))
            out_shape.append(jax.ShapeDtypeStruct((8, width), F32))

    def body(*refs):
        res = fn(*[r[...] for r in refs[:ni]])
        if not isinstance(res, tuple):
            res = (res,)
        for r, val in zip(refs[ni:ni + no], res[:no]):
            r[...] = val.astype(r.dtype)
        n = pl.program_id(0)
        i = pl.program_id(1)
        for kind, r, val in zip(accs, refs[ni + no:], res[no:]):
            first = (i == 0) if kind == "n" else jnp.logical_and(i == 0, n == 0)

            @pl.when(first)
            def _(r=r):
                r[...] = jnp.zeros(r.shape, r.dtype)

            r[...] += jnp.broadcast_to(jnp.sum(val.astype(F32), axis=0, keepdims=True), r.shape)

    res = pl.pallas_call(
        body, name=name, grid=(ncols, nr), in_specs=in_specs, out_specs=out_specs, out_shape=out_shape,
        compiler_params=_params(("arbitrary", "arbitrary")),
    )(*[a for (a, _, _) in ins])
    return res


def _mm_nn(name, a, w, *, out_dtype, tm=512, tn_cap=1024):
    M, K = a.shape
    blocked = w.ndim == 3
    NB = w.shape[-1]
    N = NB * (w.shape[0] if blocked else 1)
    tm = min(tm, M)
    tn = _tile(NB, tn_cap)
    per = NB // tn
    if blocked:
        w_spec = pl.BlockSpec((None, K, tn), lambda i, n: (n // per, 0, n % per))
    else:
        w_spec = pl.BlockSpec((K, tn), lambda i, n: (0, n))

    def body(a_ref, w_ref, o_ref):
        o_ref[...] = jnp.dot(a_ref[...], w_ref[...], preferred_element_type=F32).astype(o_ref.dtype)

    return pl.pallas_call(
        body, name=name, grid=(M // tm, N // tn),
        in_specs=[pl.BlockSpec((tm, K), lambda i, n: (i, 0)), w_spec],
        out_specs=pl.BlockSpec((tm, tn), lambda i, n: (i, n)),
        out_shape=jax.ShapeDtypeStruct((M, N), out_dtype),
        compiler_params=_params(("arbitrary", "arbitrary")),
    )(a, w)


def _mm_nt(name, a, w, *, out_dtype, tm=512, tk_cap=1024):
    M, N = a.shape
    blocked = w.ndim == 3
    NB = w.shape[-1]
    K = w.shape[-2]
    assert N == NB * (w.shape[0] if blocked else 1)
    tm = min(tm, M)
    tk = _tile(NB, tk_cap)
    per = NB // tk
    nk = N // tk
    if blocked:
        w_spec = pl.BlockSpec((None, K, tk), lambda i, k: (k // per, 0, k % per))
    else:
        w_spec = pl.BlockSpec((K, tk), lambda i, k: (0, k))

    def body(a_ref, w_ref, o_ref, acc_ref):
        k = pl.program_id(1)

        @pl.when(k == 0)
        def _():
            acc_ref[...] = jnp.zeros(acc_ref.shape, F32)

        acc_ref[...] += lax.dot_general(a_ref[...], w_ref[...], (((1,), (1,)), ((), ())),
                                        preferred_element_type=F32)

        @pl.when(k == nk - 1)
        def _():
            o_ref[...] = acc_ref[...].astype(o_ref.dtype)

    return pl.pallas_call(
        body, name=name, grid=(M // tm, nk),
        in_specs=[pl.BlockSpec((tm, tk), lambda i, k: (i, k)), w_spec],
        out_specs=pl.BlockSpec((tm, K), lambda i, k: (i, 0)),
        out_shape=jax.ShapeDtypeStruct((M, K), out_dtype),
        scratch_shapes=[pltpu.VMEM((tm, K), F32)],
        compiler_params=_params(("arbitrary", "arbitrary")),
    )(a, w)


def _mm_tn(name, a, b, *, blocks, ts=512, tn_cap=1024):
    S, K = a.shape
    N = b.shape[1]
    NB = N // blocks
    ts = min(ts, S)
    tn = _tile(NB, tn_cap)
    per = NB // tn
    ns = S // ts

    def body(a_ref, b_ref, o_ref):
        s = pl.program_id(1)

        @pl.when(s == 0)
        def _():
            o_ref[...] = jnp.zeros(o_ref.shape, F32)

        o_ref[...] += lax.dot_general(a_ref[...], b_ref[...], (((0,), (0,)), ((), ())),
                                      preferred_element_type=F32)

    return pl.pallas_call(
        body, name=name, grid=(N // tn, ns),
        in_specs=[pl.BlockSpec((ts, K), lambda n, s: (s, 0)), pl.BlockSpec((ts, tn), lambda n, s: (s, n))],
        out_specs=pl.BlockSpec((None, K, tn), lambda n, s: (n // per, 0, n % per)),
        out_shape=jax.ShapeDtypeStruct((blocks, K, NB), F32),
        compiler_params=_params(("arbitrary", "arbitrary")),
    )(a, b)


def _aligned(off):
    return off if isinstance(off, int) else pl.multiple_of(off, LANE)


def _attn_bias(s, mode, extra_ref, i, tq, kc, dil, L):
    if mode == "table":
        return s + extra_ref[...]
    if mode == "dil":
        r = lax.broadcasted_iota(jnp.int32, (tq, kc), 0)
        tk = lax.broadcasted_iota(jnp.int32, (tq, kc), 1) - DIL_REACH
        dist = jnp.abs(r - tk)
        kin = (i * tq) % L + tk
        valid = (dist <= DIL_REACH) & (kin >= 0) & (kin < L)
        s = s - extra_ref[0:1, :] * (dil * dist).astype(F32)
        return jnp.where(valid, s, NEG)
    return s


def _attn_fwd(name, q, qo, k, ko, v, vo, *, H, G, S, tq, kw, kc, kstart, mode=None, extra=None,
              var=None, dil=1, L=1, out_dtype=BF16):
    nq = S // tq
    nch = kw // kc
    Sk = k.shape[0]
    scale = HEAD ** -0.5
    has_extra = mode is not None

    def body(*refs):
        if has_extra:
            q_ref, k_ref, v_ref, e_ref, o_ref, l_ref = refs
        else:
            q_ref, k_ref, v_ref, o_ref, l_ref = refs
            e_ref = None
        i = pl.program_id(1)
        qb = q_ref[...]
        start = kstart(i)

        def chunk(j, carry):
            m, l, acc = carry
            off = _aligned(start + j * kc)
            kb = k_ref[pl.ds(off, kc), :]
            vb = v_ref[pl.ds(off, kc), :]
            s = lax.dot_general(qb, kb, (((1,), (1,)), ((), ())), preferred_element_type=F32) * scale
            s = _attn_bias(s, mode, e_ref, i, tq, kc, dil, L)
            mn = jnp.maximum(m, jnp.max(s, axis=1, keepdims=True))
            p = jnp.exp(s - mn)
            a = jnp.exp(m - mn)
            l = a * l + jnp.sum(p, axis=1, keepdims=True)
            acc = a * acc + jnp.dot(p.astype(BF16), vb, preferred_element_type=F32)
            return mn, l, acc

        init = (jnp.full((tq, 1), -3.0e38, F32), jnp.zeros((tq, 1), F32), jnp.zeros((tq, HEAD), F32))
        if nch == 1:
            m, l, acc = chunk(0, init)
        else:
            m, l, acc = lax.fori_loop(0, nch, chunk, init)
        o_ref[...] = (acc / l).astype(o_ref.dtype)
        l_ref[...] = jnp.broadcast_to(m + jnp.log(l), (tq, HEAD))

    in_specs = [
        pl.BlockSpec((tq, HEAD), lambda h, i: (i, qo + h)),
        pl.BlockSpec((Sk, HEAD), lambda h, i: (0, ko + h // G)),
        pl.BlockSpec((Sk, HEAD), lambda h, i: (0, vo + h // G)),
    ]
    args = [q, k, v]
    if mode == "table":
        in_specs.append(pl.BlockSpec((None, None, tq, kw), lambda h, i: (var(i), h, 0, 0)))
        args.append(extra)
    elif mode == "dil":
        in_specs.append(pl.BlockSpec((None, 8, kw), lambda h, i: (h, 0, 0)))
        args.append(extra)
    return pl.pallas_call(
        body, name=name, grid=(H, nq), in_specs=in_specs,
        out_specs=[pl.BlockSpec((tq, HEAD), lambda h, i: (i, h)), pl.BlockSpec((tq, HEAD), lambda h, i: (i, h))],
        out_shape=[jax.ShapeDtypeStruct((S, H * HEAD), out_dtype), jax.ShapeDtypeStruct((S, H * HEAD), F32)],
        compiler_params=_params(("arbitrary", "arbitrary")),
    )(*args)


def _attn_bwd(name, q, qo, k, ko, v, vo, o, do, lse, *, H, G, S, tq, kw, kc, kstart, mode=None, extra=None,
              var=None, dil=1, L=1):
    nq = S // tq
    nch = kw // kc
    Sk = k.shape[0]
    HK = H // G
    scale = HEAD ** -0.5
    has_extra = mode is not None
    table = mode == "table"
    if table:
        assert nq >= NA_VARIANTS and nch == 1

    def body(*refs):
        refs = list(refs)
        q_ref, k_ref, v_ref, o_ref, do_ref, l_ref = refs[:6]
        e_ref = refs[6] if has_extra else None
        outs = refs[7:] if has_extra else refs[6:]
        dq_ref, dk_ref, dv_ref = outs[:3]
        h = pl.program_id(0)
        i = pl.program_id(1)

        @pl.when(jnp.logical_and(h % G == 0, i == 0))
        def _():
            dk_ref[...] = jnp.zeros(dk_ref.shape, F32)
            dv_ref[...] = jnp.zeros(dv_ref.shape, F32)

        if table:
            db_ref = outs[3]

            @pl.when((i == 0) | (i == 1) | (i == nq - 1))
            def _():
                db_ref[...] = jnp.zeros(db_ref.shape, F32)

        qb = q_ref[...]
        dob = do_ref[...]
        delta = jnp.sum(o_ref[...].astype(F32) * dob.astype(F32), axis=1, keepdims=True)
        lse_b = l_ref[:, 0:1]
        start = kstart(i)

        def chunk(j, dq):
            off = _aligned(start + j * kc)
            kb = k_ref[pl.ds(off, kc), :]
            vb = v_ref[pl.ds(off, kc), :]
            s = lax.dot_general(qb, kb, (((1,), (1,)), ((), ())), preferred_element_type=F32) * scale
            s = _attn_bias(s, mode, e_ref, i, tq, kc, dil, L)
            p = jnp.exp(s - lse_b)
            dp = lax.dot_general(dob, vb, (((1,), (1,)), ((), ())), preferred_element_type=F32)
            ds = p * (dp - delta)
            dsb = ds.astype(BF16)
            dk_ref[pl.ds(off, kc), :] += scale * lax.dot_general(
                dsb, qb, (((0,), (0,)), ((), ())), preferred_element_type=F32)
            dv_ref[pl.ds(off, kc), :] += lax.dot_general(
                p.astype(BF16), dob, (((0,), (0,)), ((), ())), preferred_element_type=F32)
            if table:
                db_ref[...] += ds
            return dq + jnp.dot(dsb, kb, preferred_element_type=F32)

        dq0 = jnp.zeros((tq, HEAD), F32)
        dq = chunk(0, dq0) if nch == 1 else lax.fori_loop(0, nch, chunk, dq0)
        dq_ref[...] = (dq * scale).astype(dq_ref.dtype)

    in_specs = [
        pl.BlockSpec((tq, HEAD), lambda h, i: (i, qo + h)),
        pl.BlockSpec((Sk, HEAD), lambda h, i: (0, ko + h // G)),
        pl.BlockSpec((Sk, HEAD), lambda h, i: (0, vo + h // G)),
        pl.BlockSpec((tq, HEAD), lambda h, i: (i, h)),
        pl.BlockSpec((tq, HEAD), lambda h, i: (i, h)),
        pl.BlockSpec((tq, HEAD), lambda h, i: (i, h)),
    ]
    args = [q, k, v, o, do, lse]
    out_specs = [
        pl.BlockSpec((tq, HEAD), lambda h, i: (i, h)),
        pl.BlockSpec((Sk, HEAD), lambda h, i: (0, h // G)),
        pl.BlockSpec((Sk, HEAD), lambda h, i: (0, h // G)),
    ]
    out_shape = [
        jax.ShapeDtypeStruct((S, H * HEAD), BF16),
        jax.ShapeDtypeStruct((Sk, HK * HEAD), F32),
        jax.ShapeDtypeStruct((Sk, HK * HEAD), F32),
    ]
    if table:
        in_specs.append(pl.BlockSpec((None, None, tq, kw), lambda h, i: (var(i), h, 0, 0)))
        args.append(extra)
        out_specs.append(pl.BlockSpec((None, None, tq, kw), lambda h, i: (var(i), h, 0, 0)))
        out_shape.append(jax.ShapeDtypeStruct(extra.shape, F32))
    elif mode == "dil":
        in_specs.append(pl.BlockSpec((None, 8, kw), lambda h, i: (h, 0, 0)))
        args.append(extra)
    return pl.pallas_call(
        body, name=name, grid=(H, nq), in_specs=in_specs, out_specs=out_specs, out_shape=out_shape,
        compiler_params=_params(("arbitrary", "arbitrary")),
    )(*args)


def _na_constants():
    pairs = NA_VARIANTS * NA_QR * NA_KR
    sel = np.zeros((pairs, 16), np.float32)
    row_ok = np.zeros((pairs, 1), np.float32)
    for var in range(NA_VARIANTS):
        for qr in range(NA_QR):
            for kr in range(NA_KR):
                p = (var * NA_QR + qr) * NA_KR + kr
                first_key = (0, qr, NA_QR)[var]
                dr = kr - qr + (NA_WIN_R - 1) - (0, NA_QR, 2 * NA_QR)[var]
                if first_key <= kr < first_key + NA_WIN_R:
                    assert 0 <= dr < 2 * NA_WIN_R - 1
                    sel[p, dr] = 1.0
                    row_ok[p, 0] = 1.0
    j = np.arange(GRID_W)[:, None]
    c = np.arange(GRID_W)[None, :]
    cs = np.clip(j - NA_WIN_C // 2, 0, GRID_W - NA_WIN_C)
    col_ok = ((c >= cs) & (c < cs + NA_WIN_C)).astype(np.float32).reshape(1, GRID_W * GRID_W)
    dc = np.clip(c - j + NA_WIN_C - 1, 0, 2 * NA_WIN_C - 2).reshape(-1)
    toe = np.zeros((32, GRID_W * GRID_W), np.float32)
    toe[dc, np.arange(GRID_W * GRID_W)] = 1.0
    return sel, row_ok, col_ok, toe


def _split3(x):
    hi = x.astype(BF16)
    r1 = x - hi.astype(F32)
    mid = r1.astype(BF16)
    lo = (r1 - mid.astype(F32)).astype(BF16)
    return hi, mid, lo


def _dot_sel_left(onehot_bf16, x):
    return sum(jnp.dot(onehot_bf16, t, preferred_element_type=F32) for t in _split3(x))


def _dot_sel_right(x, onehot_bf16):
    return sum(jnp.dot(t, onehot_bf16, preferred_element_type=F32) for t in _split3(x))


def _na_bias_build(rpb_l):
    sel, row_ok, col_ok, toe = _na_constants()
    pairs = sel.shape[0]
    ww = GRID_W * GRID_W
    t = jnp.pad(rpb_l, ((0, 0), (0, 1), (0, 1)))

    def body(t_ref, sel_ref, toe_ref, rok_ref, cok_ref, o_ref):
        rows = _dot_sel_right(t_ref[...], toe_ref[...])
        val = _dot_sel_left(sel_ref[...], rows)
        ok = rok_ref[...] * cok_ref[...]
        o_ref[...] = jnp.where(ok > 0.5, val, NEG)

    full = lambda shape: pl.BlockSpec(shape, lambda h: (0,) * len(shape))
    out = pl.pallas_call(
        body, name="na_bias_build", grid=(NA_HEADS,),
        in_specs=[pl.BlockSpec((None, 16, 32), lambda h: (h, 0, 0)), full((pairs, 16)), full((32, ww)),
                  full((pairs, 1)), full((1, ww))],
        out_specs=pl.BlockSpec((None, pairs, ww), lambda h: (h, 0, 0)),
        out_shape=jax.ShapeDtypeStruct((NA_HEADS, pairs, ww), F32),
        compiler_params=_params(("arbitrary",)),
    )(t, jnp.asarray(sel, BF16), jnp.asarray(toe, BF16), jnp.asarray(row_ok), jnp.asarray(col_ok))
    out = out.reshape(NA_HEADS, NA_VARIANTS, NA_QR, NA_KR, GRID_W, GRID_W)
    out = out.transpose(1, 0, 2, 4, 3, 5)
    return out.reshape(NA_VARIANTS, NA_HEADS, NA_QR * GRID_W, NA_KR * GRID_W)


def _na_bias_grad(dbias):
    sel, _, _, toe = _na_constants()
    pairs = sel.shape[0]
    ww = GRID_W * GRID_W
    d = dbias.reshape(NA_VARIANTS, NA_HEADS, NA_QR, GRID_W, NA_KR, GRID_W).transpose(1, 0, 2, 4, 3, 5)
    d = d.reshape(NA_HEADS, pairs, ww)

    def body(d_ref, selt_ref, toet_ref, o_ref):
        rows = _dot_sel_left(selt_ref[...], d_ref[...])
        o_ref[...] = _dot_sel_right(rows, toet_ref[...])

    full = lambda shape: pl.BlockSpec(shape, lambda h: (0,) * len(shape))
    out = pl.pallas_call(
        body, name="na_bias_grad", grid=(NA_HEADS,),
        in_specs=[pl.BlockSpec((None, pairs, ww), lambda h: (h, 0, 0)), full((16, pairs)), full((ww, 32))],
        out_specs=pl.BlockSpec((None, 16, 32), lambda h: (h, 0, 0)),
        out_shape=jax.ShapeDtypeStruct((NA_HEADS, 16, 32), F32),
        compiler_params=_params(("arbitrary",)),
    )(d, jnp.asarray(sel.T, BF16), jnp.asarray(toe.T, BF16))
    return out[:, :2 * NA_WIN_R - 1, :2 * NA_WIN_C - 1]


def _sigmoid(x):
    return 1.0 / (1.0 + jnp.exp(-x))


def _swap32(y):
    lane = lax.broadcasted_iota(jnp.int32, y.shape, 1)
    up = pltpu.roll(y, HEAD - 32, 1)
    down = pltpu.roll(y, 32, 1)
    return jnp.where((lane // 32) % 2 == 0, up, down)


def _rope_tables(S):
    t = jnp.arange(S, dtype=jnp.int32)
    freqs = ROPE_BASE ** (-jnp.arange(32, dtype=F32) / 32)
    ar = (t // GRID_W).astype(F32)[:, None] * freqs[None, :]
    ac = (t % GRID_W).astype(F32)[:, None] * freqs[None, :]
    cos = jnp.concatenate([jnp.cos(ar), jnp.cos(ar), jnp.cos(ac), jnp.cos(ac)], axis=1)
    sin = jnp.concatenate([-jnp.sin(ar), jnp.sin(ar), -jnp.sin(ac), jnp.sin(ac)], axis=1)
    return cos, sin


def _rms_stats(x):
    x = x.astype(F32)
    r = lax.rsqrt(jnp.mean(x * x, axis=-1, keepdims=True) + EPS)
    return x * r, r


def _rms_bwd(xhat, r, dy, g):
    dxh = dy * g
    return r * (dxh - xhat * jnp.mean(dxh * xhat, axis=-1, keepdims=True))


def _qk_fwd(x, g, cos, sin):
    xhat, _ = _rms_stats(x)
    y = xhat * g
    return y * cos + _swap32(y) * sin


def _qk_bwd(dyr, x, g, cos, sin):
    dyr = dyr.astype(F32)
    dy = dyr * cos + _swap32(dyr * sin)
    xhat, r = _rms_stats(x)
    return _rms_bwd(xhat, r, dy, g), dy * xhat


def _perm(a, dil):
    if dil == 1:
        return a
    S = a.shape[0]
    return a.reshape(S // dil, dil, *a.shape[1:]).swapaxes(0, 1).reshape(a.shape)


def _unperm(a, dil):
    if dil == 1:
        return a
    S = a.shape[0]
    return a.reshape(dil, S // dil, *a.shape[1:]).swapaxes(0, 1).reshape(a.shape)


def _place():
    x, y, c = lax.axis_index("x"), lax.axis_index("y"), lax.axis_index("c")
    chips = [(1 - x, y), (x, 1 - y), (1 - x, 1 - y)]
    return x, y, c, chips


def _remote(src, dst, send, recv, dev):
    return pltpu.make_async_remote_copy(src_ref=src, dst_ref=dst, send_sem=send, recv_sem=recv,
                                        device_id=dev, device_id_type=MESH)


ANY = pl.BlockSpec(memory_space=pl.ANY)


def _allgather_weights(ws):
    n = len(ws)

    def body(*refs):
        ins, outs = refs[:n], refs[n:2 * n]
        send, recv, lsem = refs[2 * n:]
        x, y, c, chips = _place()
        me = 2 * x + y
        sib = (x, y, 1 - c)
        sends = []
        local = []
        for w in range(n):
            hr = ins[w].shape[1] // 2
            mine = pl.ds(pl.multiple_of(c * hr, 16), hr)
            cp = pltpu.make_async_copy(ins[w], outs[w].at[:, me], lsem.at[w])
            cp.start()
            local.append(cp)
            for r, (px, py) in enumerate(chips):
                cp = _remote(ins[w].at[:, mine, :], outs[w].at[:, me, mine, :], send.at[w, r], recv.at[w, r],
                             (px, py, c))
                cp.start()
                sends.append(cp)
        for w in range(n):
            hr = ins[w].shape[1] // 2
            mine = pl.ds(pl.multiple_of(c * hr, 16), hr)
            for r, (px, py) in enumerate(chips):
                blk = outs[w].at[:, 2 * px + py, mine, :]
                _remote(blk, blk, send.at[w, r], recv.at[w, r], (px, py, c)).wait_recv()
                cp = _remote(blk, blk, send.at[w, 3 + r], recv.at[w, 3 + r], sib)
                cp.start()
                sends.append(cp)
        for w in range(n):
            hr = ins[w].shape[1] // 2
            other = pl.ds(pl.multiple_of((1 - c) * hr, 16), hr)
            for r, (px, py) in enumerate(chips):
                blk = outs[w].at[:, 2 * px + py, other, :]
                _remote(blk, blk, send.at[w, 3 + r], recv.at[w, 3 + r], sib).wait_recv()
        for cp in sends:
            cp.wait_send()
        for cp in local:
            cp.wait()

    out_shape = [jax.ShapeDtypeStruct((a.shape[0], N_CHIPS) + a.shape[1:], a.dtype) for a in ws]
    return pl.pallas_call(
        body, name="allgather_weights", in_specs=[ANY] * n, out_specs=[ANY] * n, out_shape=out_shape,
        scratch_shapes=[pltpu.SemaphoreType.DMA((n, 6)), pltpu.SemaphoreType.DMA((n, 6)),
                        pltpu.SemaphoreType.DMA((n,))],
    )(*ws)


def _pair_exchange(gs):
    n = len(gs)

    def body(*refs):
        ins, outs = refs[:n], refs[n:2 * n]
        send, recv = refs[2 * n:]
        x, y, c, _ = _place()
        sib = (x, y, 1 - c)
        cps = []
        for w in range(n):
            hr = ins[w].shape[1] // 2
            theirs = pl.ds(pl.multiple_of((1 - c) * hr, 8), hr)
            cp = _remote(ins[w].at[:, theirs, :], outs[w], send.at[w], recv.at[w], sib)
            cp.start()
            cps.append(cp)
        for cp in cps:
            cp.wait_recv()
        for cp in cps:
            cp.wait_send()

    out_shape = [jax.ShapeDtypeStruct((a.shape[0], a.shape[1] // 2, a.shape[2]), a.dtype) for a in gs]
    return pl.pallas_call(
        body, name="grad_pair_exchange", in_specs=[ANY] * n, out_specs=[ANY] * n, out_shape=out_shape,
        scratch_shapes=[pltpu.SemaphoreType.DMA((n,)), pltpu.SemaphoreType.DMA((n,))],
    )(*gs)


def _chip_exchange(qs):
    n = len(qs)

    def body(*refs):
        ins, outs = refs[:n], refs[n:2 * n]
        send, recv = refs[2 * n:]
        x, y, c, chips = _place()
        cps = []
        for w in range(n):
            for r, (px, py) in enumerate(chips):
                cp = _remote(ins[w].at[2 * px + py], outs[w].at[r], send.at[w, r], recv.at[w, r], (px, py, c))
                cp.start()
                cps.append(cp)
        for cp in cps:
            cp.wait_recv()
        for cp in cps:
            cp.wait_send()

    out_shape = [jax.ShapeDtypeStruct((3,) + a.shape[1:], a.dtype) for a in qs]
    return pl.pallas_call(
        body, name="grad_chip_exchange", in_specs=[ANY] * n, out_specs=[ANY] * n, out_shape=out_shape,
        scratch_shapes=[pltpu.SemaphoreType.DMA((n, 3)), pltpu.SemaphoreType.DMA((n, 3))],
    )(*qs)


def _pair_complete(rs):
    n = len(rs)

    def body(*refs):
        ins, outs = refs[:n], refs[n:2 * n]
        send, recv, lsem = refs[2 * n:]
        x, y, c, _ = _place()
        sib = (x, y, 1 - c)
        cps, local = [], []
        for w in range(n):
            hr = ins[w].shape[0]
            mine = pl.ds(pl.multiple_of(c * hr, 8), hr)
            other = pl.ds(pl.multiple_of((1 - c) * hr, 8), hr)
            lc = pltpu.make_async_copy(ins[w], outs[w].at[mine, :], lsem.at[w])
            lc.start()
            local.append(lc)
            cp = _remote(ins[w], outs[w].at[mine, :], send.at[w], recv.at[w], sib)
            cp.start()
            cps.append((cp, _remote(ins[w], outs[w].at[other, :], send.at[w], recv.at[w], sib)))
        for _, rc in cps:
            rc.wait_recv()
        for cp, _ in cps:
            cp.wait_send()
        for lc in local:
            lc.wait()

    out_shape = [jax.ShapeDtypeStruct((2 * a.shape[0], a.shape[1]), a.dtype) for a in rs]
    return pl.pallas_call(
        body, name="grad_pair_complete", in_specs=[ANY] * n, out_specs=[ANY] * n, out_shape=out_shape,
        scratch_shapes=[pltpu.SemaphoreType.DMA((n,)), pltpu.SemaphoreType.DMA((n,)),
                        pltpu.SemaphoreType.DMA((n,))],
    )(*rs)


def _allreduce_small(vec, loss_rows, loss_scale):
    rows = vec.shape[0]

    def body(v_ref, tot_ref, loss_ref, gat_ref, send, recv):
        x, y, c, _ = _place()
        me = 4 * x + 2 * y + c
        gat_ref[me] = v_ref[...]
        peers = []
        for r in range(1, N_DEV):
            px = 1 - x if r & 4 else x
            py = 1 - y if r & 2 else y
            pc = 1 - c if r & 1 else c
            peers.append((px, py, pc))
            _remote(v_ref, gat_ref.at[me], send.at[r - 1], recv.at[r - 1], (px, py, pc)).start()
        for r, (px, py, pc) in enumerate(peers):
            _remote(v_ref, gat_ref.at[4 * px + 2 * py + pc], send.at[r], recv.at[r], (px, py, pc)).wait_recv()
        for r, (px, py, pc) in enumerate(peers):
            _remote(v_ref, gat_ref.at[me], send.at[r], recv.at[r], (px, py, pc)).wait_send()
        tot = gat_ref[0]
        for d in range(1, N_DEV):
            tot = tot + gat_ref[d]
        tot_ref[...] = tot
        loss_ref[...] = jnp.full(loss_ref.shape, loss_scale, F32) * jnp.sum(tot[:loss_rows])

    vm = pl.BlockSpec(memory_space=pltpu.VMEM)
    tot, loss = pl.pallas_call(
        body, name="allreduce_small", in_specs=[vm], out_specs=[vm, vm],
        out_shape=[jax.ShapeDtypeStruct((rows, LANE), F32), jax.ShapeDtypeStruct((8, LANE), F32)],
        scratch_shapes=[pltpu.VMEM((N_DEV, rows, LANE), F32), pltpu.SemaphoreType.DMA((N_DEV - 1,)),
                        pltpu.SemaphoreType.DMA((N_DEV - 1,))],
        compiler_params=_params(),
    )(vec)
    return tot, loss


def _add_rows(name, terms):
    shape = terms[0].shape
    C = shape[-1]
    flat = [t.reshape(-1, C) for t in terms]
    rows = flat[0].shape[0]
    width = _tile(C, 1024)

    def fn(*vals):
        tot = vals[0]
        for val in vals[1:]:
            tot = tot + val
        return tot

    (out,) = _ew(name, fn, [(t, "t", 0) for t in flat], [(C, F32)], rows=rows, width=width, ncols=C // width,
                 ts=512)
    return out.reshape(shape)


def _adamw(name, w, g, m, v):
    shape = w.shape
    C = shape[-1]
    flat = [t.reshape(-1, C) for t in (w, g, m, v)]
    rows = flat[0].shape[0]
    width = _tile(C, 1024)

    def fn(w, g, m, v):
        m2 = ADAM_B1 * m + (1.0 - ADAM_B1) * g
        v2 = ADAM_B2 * v + (1.0 - ADAM_B2) * (g * g)
        m_hat = m2 / (1.0 - ADAM_B1 ** ADAM_STEP)
        v_hat = v2 / (1.0 - ADAM_B2 ** ADAM_STEP)
        delta = -ADAM_LR * (m_hat / (jnp.sqrt(v_hat) + ADAM_EPS) + ADAM_WD * w)
        return delta, m2, v2

    outs = _ew(name, fn, [(t, "t", 0) for t in flat], [(C, F32)] * 3, rows=rows, width=width,
               ncols=C // width, ts=min(512, rows))
    return [o.reshape(shape) for o in outs]


def _na_geometry(S):
    rows = S // GRID_W
    nq = rows // NA_QR
    tq = NA_QR * GRID_W
    kw = NA_KR * GRID_W

    def kstart(i):
        return jnp.clip(NA_QR * i - NA_WIN_R // 2, 0, rows - NA_KR) * GRID_W

    def var(i):
        return jnp.where(i == 0, 0, jnp.where(i == nq - 1, 2, 1))

    return dict(S=S, tq=tq, kw=kw, kc=kw, kstart=kstart, var=var, mode="table")


def _dil_geometry(S, dil):
    L = S // dil
    tq = min(512, L)
    kw = tq + 2 * DIL_REACH
    return dict(S=S, tq=tq, kw=kw, kc=kw, kstart=lambda i: i * tq, mode="dil", dil=dil, L=L)


def _dil_slopes(g, kw):
    idx = jnp.arange(1, DIL_HEADS + 1, dtype=F32)[g * DIL_HPG:(g + 1) * DIL_HPG]
    slopes = 2.0 ** (-ALIBI_MAX_EXP * idx / DIL_HEADS)
    return jnp.broadcast_to(slopes[:, None, None], (DIL_HPG, 8, kw))


def _layer_fwd(x, p, cos, sin):
    S, D = x.shape
    ts = 256
    gp = p["pre_g"][None, :]
    (h,) = _ew("pre_norm", lambda x, g: _rms_stats(x)[0] * g, [(x, "t", 0), (gp, "v", 0)], [(D, BF16)],
               rows=S, width=D, ncols=1, ts=ts)
    proj = _mm_nn("proj_in", h, p["w_in"], out_dtype=BF16)

    bias = _na_bias_build(p["rpb"])
    ya, lse_a = _attn_fwd("na_fwd", proj, C_QA, proj, C_KA, proj, C_VA, H=NA_HEADS, G=1, extra=bias,
                          **_na_geometry(S))

    def qk(name, off, nh, g):
        (out,) = _ew(name, _qk_fwd, [(proj, "t", off), (g[None, :], "vf", 0), (cos, "tf", 0), (sin, "tf", 0)],
                     [(nh * HEAD, BF16)], rows=S, width=HEAD, ncols=nh, ts=512)
        return out

    qb = qk("q_norm_rope", C_QB, GQ_HEADS, p["q_g"])
    kb = qk("k_norm_rope", C_KB, GKV_HEADS, p["k_g"])
    tq_b = min(512, S)
    kc_b = min(1024, S)
    geo_b = dict(S=S, tq=tq_b, kw=S, kc=kc_b, kstart=lambda i: 0)
    yb, lse_b = _attn_fwd("gqa_fwd", qb, 0, kb, 0, proj, C_VB, H=GQ_HEADS, G=GQ_HEADS // GKV_HEADS, **geo_b)

    W = DIL_HPG * HEAD
    og, lg, saved_c = [], [], []
    for g, (win, dil) in enumerate(DIL_GROUPS):
        assert (win // 2) // dil == DIL_REACH
        sl = lambda off: proj[:, (off + g * DIL_HPG) * HEAD:(off + (g + 1) * DIL_HPG) * HEAD]
        qg = _perm(sl(C_QC), dil)
        kg = jnp.pad(_perm(sl(C_KC), dil), ((DIL_REACH, DIL_REACH), (0, 0)))
        vg = jnp.pad(_perm(sl(C_VC), dil), ((DIL_REACH, DIL_REACH), (0, 0)))
        geo = _dil_geometry(S, dil)
        o, l = _attn_fwd(f"dil{g}_fwd", qg, 0, kg, 0, vg, 0, H=DIL_HPG, G=1, extra=_dil_slopes(g, geo["kw"]),
                         out_dtype=F32, **geo)
        og.append(_unperm(o, dil))
        lg.append(_unperm(l, dil))
        saved_c.append((qg, kg, vg))

    def combine(o1, o2, o3, l1, l2, l3):
        m = jnp.maximum(jnp.maximum(l1, l2), l3)
        e1, e2, e3 = jnp.exp(l1 - m), jnp.exp(l2 - m), jnp.exp(l3 - m)
        tot = e1 + e2 + e3
        return (e1 * o1 + e2 * o2 + e3 * o3) / tot, m + jnp.log(tot)

    yc, lse_c = _ew("dil_combine", combine, [(a, "t", 0) for a in og + lg], [(W, BF16), (W, F32)],
                    rows=S, width=W, ncols=1, ts=512)

    def gate(y, z):
        z = z.astype(F32)
        return y.astype(F32) * (z * _sigmoid(z))

    def branch(name, y, zoff, width, w):
        unit = width // HEAD
        (u,) = _ew(name + "_silu", gate, [(y, "t", 0), (proj, "tf", zoff // unit)], [(width, BF16)],
                   rows=S, width=width, ncols=1, ts=512)
        return u, _mm_nn(name + "_proj", u, w, out_dtype=BF16)

    ua, ta = branch("branch_a", ya, C_ZA, NA_HEADS * HEAD, p["w_a"])
    ub, tb = branch("branch_b", yb, C_ZB, GQ_HEADS * HEAD, p["w_b"])
    uc, tc = branch("branch_c", yc, C_ZC, W, p["w_c"])

    mw = 512
    nm = D // mw
    goff = C_GATE * HEAD // mw
    bg = p["b_gate"][None, :]

    def merge(ta, tb, tc, ga, gb, gc, ba, bb, bc):
        sa = _sigmoid(ga.astype(F32) + ba)
        sb = _sigmoid(gb.astype(F32) + bb)
        sc = _sigmoid(gc.astype(F32) + bc)
        return sa * ta.astype(F32) + sb * tb.astype(F32) + sc * tc.astype(F32)

    (merged,) = _ew("merge", merge,
                    [(ta, "t", 0), (tb, "t", 0), (tc, "t", 0),
                     (proj, "t", goff), (proj, "t", goff + nm), (proj, "t", goff + 2 * nm),
                     (bg, "v", 0), (bg, "v", nm), (bg, "v", 2 * nm)],
                    [(D, BF16)], rows=S, width=mw, ncols=nm, ts=512)
    out = _mm_nn("proj_out", merged, p["w_out"], out_dtype=F32)
    gq = p["post_g"][None, :]
    (x_next,) = _ew("post_norm_residual", lambda x, o, g: x + _rms_stats(o)[0] * g,
                    [(x, "t", 0), (out, "t", 0), (gq, "v", 0)], [(D, F32)], rows=S, width=D, ncols=1, ts=ts)
    saved = dict(x=x, h=h, proj=proj, bias=bias, ya=ya, lse_a=lse_a, qb=qb, kb=kb, yb=yb, lse_b=lse_b,
                 geo_b=geo_b, saved_c=saved_c, yc=yc, lse_c=lse_c, ua=ua, ub=ub, uc=uc, ta=ta, tb=tb, tc=tc,
                 merged=merged, out=out)
    return x_next, saved


def _layer_bwd(gy, p, s, cos, sin):
    S, D = gy.shape
    ts = 256
    proj = s["proj"]
    grads = {}

    def post_bwd(o, dy, g):
        xhat, r = _rms_stats(o)
        return _rms_bwd(xhat, r, dy, g), dy * xhat

    dout, dg_post = _ew("post_norm_bwd", post_bwd, [(s["out"], "t", 0), (gy, "t", 0), (p["post_g"][None, :], "v", 0)],
                        [(D, BF16)], accs=("n",), rows=S, width=D, ncols=1, ts=ts)
    grads["post_g"] = dg_post[0]
    dmerged = _mm_nt("proj_out_bwd", dout, p["w_out"], out_dtype=BF16)
    grads["w_out"] = _mm_tn("proj_out_wgrad", s["merged"], dout, blocks=1).reshape(N_CHIPS, D // N_CHIPS, D)

    mw = 512
    nm = D // mw
    goff = C_GATE * HEAD // mw
    bg = p["b_gate"][None, :]

    def merge_bwd(dm, ta, tb, tc, ga, gb, gc, ba, bb, bc):
        dm = dm.astype(F32)
        res_t, res_g = [], []
        for t, g, b in ((ta, ga, ba), (tb, gb, bb), (tc, gc, bc)):
            sg = _sigmoid(g.astype(F32) + b)
            res_t.append(dm * sg)
            res_g.append(dm * t.astype(F32) * sg * (1.0 - sg))
        return (*res_t, *res_g, *res_g)

    mres = _ew("merge_bwd", merge_bwd,
               [(dmerged, "t", 0), (s["ta"], "t", 0), (s["tb"], "t", 0), (s["tc"], "t", 0),
                (proj, "t", goff), (proj, "t", goff + nm), (proj, "t", goff + 2 * nm),
                (bg, "v", 0), (bg, "v", nm), (bg, "v", 2 * nm)],
               [(D, BF16)] * 6, accs=("n", "n", "n"), rows=S, width=mw, ncols=nm, ts=512)
    dta, dtb, dtc, dga, dgb, dgc = mres[:6]
    grads["b_gate"] = jnp.concatenate([a[0] for a in mres[6:]])

    def silu_bwd(du, y, z):
        du, y, z = du.astype(F32), y.astype(F32), z.astype(F32)
        sg = _sigmoid(z)
        return du * (z * sg), du * y * (sg * (1.0 + z * (1.0 - sg)))

    def branch_bwd(name, dt, u, y, zoff, width, w):
        du = _mm_nt(name + "_proj_bwd", dt, w, out_dtype=BF16)
        gw = _mm_tn(name + "_wgrad", u, dt, blocks=N_CHIPS)
        unit = width // HEAD
        dy, dz = _ew(name + "_silu_bwd", silu_bwd, [(du, "t", 0), (y, "t", 0), (proj, "tf", zoff // unit)],
                     [(width, BF16), (width, BF16)], rows=S, width=width, ncols=1, ts=512)
        return dy, dz, gw

    W = DIL_HPG * HEAD
    dya, dza, grads["w_a"] = branch_bwd("branch_a", dta, s["ua"], s["ya"], C_ZA, NA_HEADS * HEAD, p["w_a"])
    dyb, dzb, grads["w_b"] = branch_bwd("branch_b", dtb, s["ub"], s["yb"], C_ZB, GQ_HEADS * HEAD, p["w_b"])
    dyc, dzc, grads["w_c"] = branch_bwd("branch_c", dtc, s["uc"], s["yc"], C_ZC, W, p["w_c"])

    dqa, dka, dva, dbias = _attn_bwd("na_bwd", proj, C_QA, proj, C_KA, proj, C_VA, s["ya"], dya, s["lse_a"],
                                     H=NA_HEADS, G=1, extra=s["bias"], **_na_geometry(S))
    grads["rpb"] = _na_bias_grad(dbias)

    geo_b = dict(s["geo_b"])
    geo_b["kc"] = min(512, S)
    dqr, dkr, dvb = _attn_bwd("gqa_bwd", s["qb"], 0, s["kb"], 0, proj, C_VB, s["yb"], dyb, s["lse_b"],
                              H=GQ_HEADS, G=GQ_HEADS // GKV_HEADS, **geo_b)

    def qk_bwd(name, dyr, off, nh, g):
        return _ew(name, _qk_bwd, [(dyr, "t", 0), (proj, "t", off), (g[None, :], "vf", 0), (cos, "tf", 0),
                                   (sin, "tf", 0)],
                   [(nh * HEAD, BF16)], accs=("f",), rows=S, width=HEAD, ncols=nh, ts=512)

    dqb, dgq = qk_bwd("q_norm_rope_bwd", dqr, C_QB, GQ_HEADS, p["q_g"])
    dkb, dgk = qk_bwd("k_norm_rope_bwd", dkr, C_KB, GKV_HEADS, p["k_g"])
    grads["q_g"] = dgq[0]
    grads["k_g"] = dgk[0]

    dqc, dkc, dvc = [], [], []
    for g, (win, dil) in enumerate(DIL_GROUPS):
        qg, kg, vg = s["saved_c"][g]
        geo = _dil_geometry(S, dil)
        dq, dk, dv = _attn_bwd(f"dil{g}_bwd", qg, 0, kg, 0, vg, 0, _perm(s["yc"], dil), _perm(dyc, dil),
                               _perm(s["lse_c"], dil), H=DIL_HPG, G=1, extra=_dil_slopes(g, geo["kw"]), **geo)
        dqc.append(_unperm(dq, dil))
        dkc.append(_unperm(dk[DIL_REACH:-DIL_REACH], dil).astype(BF16))
        dvc.append(_unperm(dv[DIL_REACH:-DIL_REACH], dil).astype(BF16))

    dproj = jnp.concatenate(
        [dqa, dka.astype(BF16), dva.astype(BF16), dqb, dkb, dvb.astype(BF16), *dqc, *dkc, *dvc,
         dza, dzb, dzc, dga, dgb, dgc], axis=1)
    dh = _mm_nt("proj_in_bwd", dproj, p["w_in"], out_dtype=F32)
    grads["w_in"] = _mm_tn("proj_in_wgrad", s["h"], dproj, blocks=N_CHIPS)

    def pre_bwd(x, dh, g, gy):
        xhat, r = _rms_stats(x)
        return gy + _rms_bwd(xhat, r, dh, g), dh * xhat

    dx, dg_pre = _ew("pre_norm_bwd", pre_bwd, [(s["x"], "t", 0), (dh, "t", 0), (p["pre_g"][None, :], "v", 0),
                                                 (gy, "t", 0)],
                     [(D, F32)], accs=("n",), rows=S, width=D, ncols=1, ts=ts)
    grads["pre_g"] = dg_pre[0]
    return dx, grads


BIG = ("w_in", "w_a", "w_b", "w_c", "w_out")
SMALL = ("pre_g", "b_gate", "q_g", "k_g", "rpb", "post_g")


def _reduce_big(gs):
    c = lax.axis_index("c")
    me = 2 * lax.axis_index("x") + lax.axis_index("y")
    got = _pair_exchange(gs)
    pair = []
    for g, t in zip(gs, got):
        hr = g.shape[1] // 2
        pair.append(_add_rows("grad_pair_sum", [lax.dynamic_slice_in_dim(g, c * hr, hr, axis=1), t]))
    recv = _chip_exchange(pair)
    done = []
    for q, t in zip(pair, recv):
        own = lax.dynamic_index_in_dim(q, me, axis=0, keepdims=False)
        done.append(_add_rows("grad_chip_sum", [own, t[0], t[1], t[2]]))
    return _pair_complete(done)


def _pack(parts, rows_mult=8):
    flat = jnp.concatenate([a.reshape(-1) for a in parts])
    n = flat.shape[0]
    unit = rows_mult * LANE
    padded = -(-n // unit) * unit
    return jnp.pad(flat, (0, padded - n)).reshape(-1, LANE)


def kernel(x, pre_norm_g, w_in, b_gate, q_norm_g, k_norm_g, rpb, w_branch_a, w_branch_b, w_branch_c, w_out, post_norm_g, loss_target, m_pre_norm_g, m_w_in, m_b_gate, m_q_norm_g, m_k_norm_g, m_rpb, m_w_branch_a, m_w_branch_b, m_w_branch_c, m_w_out, m_post_norm_g, v_pre_norm_g, v_w_in, v_b_gate, v_q_norm_g, v_k_norm_g, v_rpb, v_w_branch_a, v_w_branch_b, v_w_branch_c, v_w_out, v_post_norm_g):
    n_layers = w_in.shape[0]
    S, D = x.shape[1], x.shape[2]
    big_w = dict(w_in=w_in, w_a=w_branch_a, w_b=w_branch_b, w_c=w_branch_c, w_out=w_out)
    big_m = dict(w_in=m_w_in, w_a=m_w_branch_a, w_b=m_w_branch_b, w_c=m_w_branch_c, w_out=m_w_out)
    big_v = dict(w_in=v_w_in, w_a=v_w_branch_a, w_b=v_w_branch_b, w_c=v_w_branch_c, w_out=v_w_out)
    small_w = dict(pre_g=pre_norm_g, b_gate=b_gate, q_g=q_norm_g, k_g=k_norm_g, rpb=rpb, post_g=post_norm_g)
    small_m = dict(pre_g=m_pre_norm_g, b_gate=m_b_gate, q_g=m_q_norm_g, k_g=m_k_norm_g, rpb=m_rpb,
                   post_g=m_post_norm_g)
    small_v = dict(pre_g=v_pre_norm_g, b_gate=v_b_gate, q_g=v_q_norm_g, k_g=v_k_norm_g, rpb=v_rpb,
                   post_g=v_post_norm_g)

    full = dict(zip(BIG, _allgather_weights([big_w[n].astype(BF16) for n in BIG])))
    cos, sin = _rope_tables(S)

    def layer_params(l):
        p = {n: full[n][l] for n in ("w_in", "w_a", "w_b", "w_c")}
        p["w_out"] = full["w_out"][l].reshape(D, D)
        p.update(pre_g=pre_norm_g[l], b_gate=b_gate[l], q_g=q_norm_g[l], k_g=k_norm_g[l], rpb=rpb[l],
                 post_g=post_norm_g[l])
        return p

    act = x[0]
    saved = []
    for l in range(n_layers):
        act, s = _layer_fwd(act, layer_params(l), cos, sin)
        saved.append(s)

    dy, loss_cols = _ew("loss", lambda y, t: ((y - t) * (1.0 / D), (y - t) * (y - t)),
                        [(act, "t", 0), (loss_target[0], "t", 0)], [(D, F32)], accs=("n",),
                        rows=S, width=D, ncols=1, ts=256)

    big_g = {n: [None] * n_layers for n in BIG}
    small_g = {n: [None] * n_layers for n in SMALL}
    for l in reversed(range(n_layers)):
        dy, grads = _layer_bwd(dy, layer_params(l), saved[l], cos, sin)
        for n, r in zip(BIG, _reduce_big([grads[n] for n in BIG])):
            big_g[n][l] = r
        for n in SMALL:
            small_g[n][l] = grads[n]

    loss_rows = -(-D // (8 * LANE)) * 8
    parts = [_pack([loss_cols[0]])] + [_pack([jnp.stack(small_g[n]) for n in SMALL])]
    tot, loss = _allreduce_small(jnp.concatenate(parts), loss_rows, 0.5 / D)
    wvec = jnp.concatenate([jnp.zeros((loss_rows, LANE), F32), _pack([small_w[n] for n in SMALL])])
    mvec = jnp.concatenate([jnp.zeros((loss_rows, LANE), F32), _pack([small_m[n] for n in SMALL])])
    vvec = jnp.concatenate([jnp.ones((loss_rows, LANE), F32), _pack([small_v[n] for n in SMALL])])
    small_out = [tot] + _adamw("adamw_small", wvec, tot, mvec, vvec)

    def unpack(vec):
        flat = vec[loss_rows:].reshape(-1)
        res, pos = {}, 0
        for n in SMALL:
            size = int(np.prod(small_w[n].shape))
            res[n] = flat[pos:pos + size].reshape(small_w[n].shape)
            pos += size
        return res

    small_res = [unpack(v) for v in small_out]

    big_res = [{}, {}, {}, {}]
    for n in BIG:
        g = jnp.stack(big_g[n])
        big_res[0][n] = g
        for k, val in enumerate(_adamw("adamw_" + n, big_w[n], g, big_m[n], big_v[n])):
            big_res[k + 1][n] = val

    order = (("pre_g", small_res), ("w_in", big_res), ("b_gate", small_res), ("q_g", small_res),
             ("k_g", small_res), ("rpb", small_res), ("w_a", big_res), ("w_b", big_res), ("w_c", big_res),
             ("w_out", big_res), ("post_g", small_res))
    outs = [loss[0, 0], dy[None]]
    for k in range(4):
        outs.extend(src[k][n] for n, src in order)
    return tuple(outs)
```

```python
import functools

import numpy as np
import jax
import jax.numpy as jnp
from jax import lax
from jax.experimental import pallas as pl
from jax.experimental.pallas import tpu as pltpu

F32 = jnp.float32
BF16 = jnp.bfloat16
MESH = pl.DeviceIdType.MESH

HEAD = 128
GRID_W = 64
EPS = 1e-6
NEG = -1e30
NA_HEADS = 8
NA_WIN_R = 8
NA_WIN_C = 16
GQ_HEADS = 8
GKV_HEADS = 2
ROPE_BASE = 10000.0
DIL_GROUPS = ((128, 1), (512, 4), (2048, 16))
DIL_REACH = 64
DIL_HPG = 4
DIL_HEADS = 12
ALIBI_MAX_EXP = 8.0
ADAM_LR = 0.001
ADAM_B1 = 0.9
ADAM_B2 = 0.999
ADAM_EPS = 1e-08
ADAM_WD = 0.01
ADAM_STEP = 10

C_QA, C_KA, C_VA = 0, 8, 16
C_QB, C_KB, C_VB = 24, 32, 34
C_QC, C_KC, C_VC = 36, 48, 60
C_ZA, C_ZB, C_ZC = 72, 80, 88
C_GATE = 92

NA_QR = 4
NA_KR = NA_QR + NA_WIN_R
NA_VARIANTS = 3

VMEM_LIMIT_BYTES = 48 * 1024 * 1024
LANE = 128
N_CHIPS = 4
N_DEV = 8


def _params(sem=None):
    kw = dict(vmem_limit_bytes=VMEM_LIMIT_BYTES)
    if sem is not None:
        kw["dimension_semantics"] = sem
    return pltpu.CompilerParams(**kw)


def _tile(n, cap):
    units = n // LANE
    assert units * LANE == n, n
    best = 1
    for d in range(1, units + 1):
        if units % d == 0 and d * LANE <= cap:
            best = d
    return best * LANE


def _ew(name, fn, ins, outs, accs=(), *, rows, width, ncols, ts, carry=None):
    ts = min(ts, rows)
    assert rows % ts == 0
    nr = rows // ts
    ni, no = len(ins), len(outs)

    def imap(mode, off):
        if mode == "t":
            return lambda n, i: (i, off + n)
        if mode == "tf":
            return lambda n, i: (i, off)
        if mode == "v":
            return lambda n, i: (0, off + n)
        return lambda n, i: (0, off)

    in_specs = [pl.BlockSpec((ts if m in ("t", "tf") else 1, width), imap(m, o)) for (_, m, o) in ins]
    out_specs = [pl.BlockSpec((ts, width), lambda n, i: (i, n)) for _ in outs]
    out_shape = [jax.ShapeDtypeStruct((rows, c), d) for (c, d) in outs]
    for a in accs:
        if a == "n":
            out_specs.append(pl.BlockSpec((8, width), lambda n, i: (0, n)))
            out_shape.append(jax.ShapeDtypeStruct((8, width * ncols), F32))
        else:
            out_specs.append(pl.BlockSpec((8, width), lambda n, i: (0, 0)))
            out_shape.append(jax.ShapeDtypeStruct((8, width), F32))

    def body(*refs):
        res = fn(*[r[...] for r in refs[:ni]])
        if not isinstance(res, tuple):
            res = (res,)
        for r, val in zip(refs[ni:ni + no], res[:no]):
            r[...] = val.astype(r.dtype)
        n = pl.program_id(0)
        i = pl.program_id(1)
        for kind, r, val in zip(accs, refs[ni + no:], res[no:]):
            first = (i == 0) if kind == "n" else jnp.logical_and(i == 0, n == 0)

            @pl.when(first)
            def _(r=r):
                r[...] = jnp.zeros(r.shape, r.dtype)

            r[...] += jnp.broadcast_to(jnp.sum(val.astype(F32), axis=0, keepdims=True), r.shape)

    grid = (ncols, nr)
    body, c_in, c_in_specs, c_out, c_out_specs, c_scratch = _carried(body, ni, no + len(accs), grid, carry)
    res = pl.pallas_call(
        body, name=name, grid=grid, in_specs=in_specs + c_in_specs, out_specs=out_specs + c_out_specs,
        out_shape=out_shape + c_out, scratch_shapes=c_scratch,
        compiler_params=_params(("arbitrary", "arbitrary")),
    )(*[a for (a, _, _) in ins], *c_in)
    return res


def _mm_nn(name, a, w, *, out_dtype, tm=512, tn_cap=1024):
    M, K = a.shape
    blocked = w.ndim == 3
    NB = w.shape[-1]
    N = NB * (w.shape[0] if blocked else 1)
    tm = min(tm, M)
    tn = _tile(NB, tn_cap)
    per = NB // tn
    if blocked:
        w_spec = pl.BlockSpec((None, K, tn), lambda i, n: (n // per, 0, n % per))
    else:
        w_spec = pl.BlockSpec((K, tn), lambda i, n: (0, n))

    def body(a_ref, w_ref, o_ref):
        o_ref[...] = jnp.dot(a_ref[...], w_ref[...], preferred_element_type=F32).astype(o_ref.dtype)

    return pl.pallas_call(
        body, name=name, grid=(M // tm, N // tn),
        in_specs=[pl.BlockSpec((tm, K), lambda i, n: (i, 0)), w_spec],
        out_specs=pl.BlockSpec((tm, tn), lambda i, n: (i, n)),
        out_shape=jax.ShapeDtypeStruct((M, N), out_dtype),
        compiler_params=_params(("arbitrary", "arbitrary")),
    )(a, w)


def _mm_nt(name, a, w, *, out_dtype, tm=512, tk_cap=1024, carry=None):
    M, N = a.shape
    blocked = w.ndim == 3
    NB = w.shape[-1]
    K = w.shape[-2]
    assert N == NB * (w.shape[0] if blocked else 1)
    tm = min(tm, M)
    tk = _tile(NB, tk_cap)
    per = NB // tk
    nk = N // tk
    if blocked:
        w_spec = pl.BlockSpec((None, K, tk), lambda i, k: (k // per, 0, k % per))
    else:
        w_spec = pl.BlockSpec((K, tk), lambda i, k: (0, k))

    def body(a_ref, w_ref, o_ref, acc_ref):
        k = pl.program_id(1)

        @pl.when(k == 0)
        def _():
            acc_ref[...] = jnp.zeros(acc_ref.shape, F32)

        acc_ref[...] += lax.dot_general(a_ref[...], w_ref[...], (((1,), (1,)), ((), ())),
                                        preferred_element_type=F32)

        @pl.when(k == nk - 1)
        def _():
            o_ref[...] = acc_ref[...].astype(o_ref.dtype)

    grid = (M // tm, nk)
    body, c_in, c_in_specs, c_out, c_out_specs, c_scratch = _carried(body, 2, 1, grid, carry)
    res = pl.pallas_call(
        body, name=name, grid=grid,
        in_specs=[pl.BlockSpec((tm, tk), lambda i, k: (i, k)), w_spec] + c_in_specs,
        out_specs=[pl.BlockSpec((tm, K), lambda i, k: (i, 0))] + c_out_specs,
        out_shape=[jax.ShapeDtypeStruct((M, K), out_dtype)] + c_out,
        scratch_shapes=[pltpu.VMEM((tm, K), F32)] + c_scratch,
        compiler_params=_params(("arbitrary", "arbitrary")),
    )(a, w, *c_in)
    return res[0] if carry is None else res


def _mm_tn(name, a, b, *, blocks, ts=512, tn_cap=1024):
    S, K = a.shape
    N = b.shape[1]
    NB = N // blocks
    ts = min(ts, S)
    tn = _tile(NB, tn_cap)
    per = NB // tn
    ns = S // ts

    def body(a_ref, b_ref, o_ref):
        s = pl.program_id(1)

        @pl.when(s == 0)
        def _():
            o_ref[...] = jnp.zeros(o_ref.shape, F32)

        o_ref[...] += lax.dot_general(a_ref[...], b_ref[...], (((0,), (0,)), ((), ())),
                                      preferred_element_type=F32)

    return pl.pallas_call(
        body, name=name, grid=(N // tn, ns),
        in_specs=[pl.BlockSpec((ts, K), lambda n, s: (s, 0)), pl.BlockSpec((ts, tn), lambda n, s: (s, n))],
        out_specs=pl.BlockSpec((None, K, tn), lambda n, s: (n // per, 0, n % per)),
        out_shape=jax.ShapeDtypeStruct((blocks, K, NB), F32),
        compiler_params=_params(("arbitrary", "arbitrary")),
    )(a, b)


def _aligned(off):
    return off if isinstance(off, int) else pl.multiple_of(off, LANE)


def _attn_bias(s, mode, extra_ref, i, tq, kc, dil, L):
    if mode == "table":
        return s + extra_ref[...]
    if mode == "dil":
        r = lax.broadcasted_iota(jnp.int32, (tq, kc), 0)
        tk = lax.broadcasted_iota(jnp.int32, (tq, kc), 1) - DIL_REACH
        dist = jnp.abs(r - tk)
        kin = (i * tq) % L + tk
        valid = (dist <= DIL_REACH) & (kin >= 0) & (kin < L)
        s = s - extra_ref[0:1, :] * (dil * dist).astype(F32)
        return jnp.where(valid, s, NEG)
    return s


def _attn_fwd(name, q, qo, k, ko, v, vo, *, H, G, S, tq, kw, kc, kstart, mode=None, extra=None,
              var=None, dil=1, L=1, out_dtype=BF16, prescaled=False, carry=None):
    nq = S // tq
    nch = kw // kc
    Sk = k.shape[0]
    scale = HEAD ** -0.5
    has_extra = mode is not None
    exp = jnp.exp2 if prescaled else jnp.exp
    log = jnp.log2 if prescaled else jnp.log

    def body(*refs):
        if has_extra:
            q_ref, k_ref, v_ref, e_ref, o_ref, l_ref = refs
        else:
            q_ref, k_ref, v_ref, o_ref, l_ref = refs
            e_ref = None
        i = pl.program_id(1)
        qb = q_ref[...]
        start = kstart(i)

        def chunk(j, carry):
            m, l, acc = carry
            off = _aligned(start + j * kc)
            kb = k_ref[pl.ds(off, kc), :]
            vb = v_ref[pl.ds(off, kc), :]
            s = lax.dot_general(qb, kb, (((1,), (1,)), ((), ())), preferred_element_type=F32)
            if not prescaled:
                s = s * scale
            s = _attn_bias(s, mode, e_ref, i, tq, kc, dil, L)
            mn = jnp.maximum(m, jnp.max(s, axis=1, keepdims=True))
            p = exp(s - mn)
            a = exp(m - mn)
            l = a * l + jnp.sum(p, axis=1, keepdims=True)
            acc = a * acc + jnp.dot(p.astype(BF16), vb, preferred_element_type=F32)
            return mn, l, acc

        init = (jnp.full((tq, 1), -3.0e38, F32), jnp.zeros((tq, 1), F32), jnp.zeros((tq, HEAD), F32))
        if nch == 1:
            m, l, acc = chunk(0, init)
        else:
            m, l, acc = lax.fori_loop(0, nch, chunk, init)
        o_ref[...] = (acc / l).astype(o_ref.dtype)
        l_ref[...] = jnp.broadcast_to(m + log(l), (tq, HEAD))

    in_specs = [
        pl.BlockSpec((tq, HEAD), lambda h, i: (i, qo + h)),
        pl.BlockSpec((Sk, HEAD), lambda h, i: (0, ko + h // G)),
        pl.BlockSpec((Sk, HEAD), lambda h, i: (0, vo + h // G)),
    ]
    args = [q, k, v]
    if mode == "table":
        in_specs.append(pl.BlockSpec((None, None, tq, kw), lambda h, i: (var(i), h, 0, 0)))
        args.append(extra)
    elif mode == "dil":
        in_specs.append(pl.BlockSpec((None, 8, kw), lambda h, i: (h, 0, 0)))
        args.append(extra)
    grid = (H, nq)
    body, c_in, c_in_specs, c_out, c_out_specs, c_scratch = _carried(body, len(args), 2, grid, carry)
    return pl.pallas_call(
        body, name=name, grid=grid, in_specs=in_specs + c_in_specs,
        out_specs=[pl.BlockSpec((tq, HEAD), lambda h, i: (i, h)),
                   pl.BlockSpec((tq, HEAD), lambda h, i: (i, h))] + c_out_specs,
        out_shape=[jax.ShapeDtypeStruct((S, H * HEAD), out_dtype),
                   jax.ShapeDtypeStruct((S, H * HEAD), F32)] + c_out,
        scratch_shapes=c_scratch,
        compiler_params=_params(("arbitrary", "arbitrary")),
    )(*args, *c_in)


def _attn_bwd(name, q, qo, k, ko, v, vo, o, do, lse, *, H, G, S, tq, kw, kc, kstart, mode=None, extra=None,
              var=None, dil=1, L=1, prescaled=False, carry=None):
    nq = S // tq
    nch = kw // kc
    Sk = k.shape[0]
    HK = H // G
    scale = HEAD ** -0.5
    dk_mult = float(np.log(2.0)) if prescaled else scale
    exp = jnp.exp2 if prescaled else jnp.exp
    has_extra = mode is not None
    table = mode == "table"
    if table:
        assert nq >= NA_VARIANTS and nch == 1

    def body(*refs):
        refs = list(refs)
        q_ref, k_ref, v_ref, o_ref, do_ref, l_ref = refs[:6]
        e_ref = refs[6] if has_extra else None
        outs = refs[7:] if has_extra else refs[6:]
        dq_ref, dk_ref, dv_ref = outs[:3]
        h = pl.program_id(0)
        i = pl.program_id(1)

        @pl.when(jnp.logical_and(h % G == 0, i == 0))
        def _():
            dk_ref[...] = jnp.zeros(dk_ref.shape, F32)
            dv_ref[...] = jnp.zeros(dv_ref.shape, F32)

        if table:
            db_ref = outs[3]

            @pl.when((i == 0) | (i == 1) | (i == nq - 1))
            def _():
                db_ref[...] = jnp.zeros(db_ref.shape, F32)

        qb = q_ref[...]
        dob = do_ref[...]
        delta = jnp.sum(o_ref[...].astype(F32) * dob.astype(F32), axis=1, keepdims=True)
        lse_b = l_ref[:, 0:1]
        start = kstart(i)

        def chunk(j, dq):
            off = _aligned(start + j * kc)
            kb = k_ref[pl.ds(off, kc), :]
            vb = v_ref[pl.ds(off, kc), :]
            s = lax.dot_general(qb, kb, (((1,), (1,)), ((), ())), preferred_element_type=F32)
            if not prescaled:
                s = s * scale
            s = _attn_bias(s, mode, e_ref, i, tq, kc, dil, L)
            p = exp(s - lse_b)
            dp = lax.dot_general(dob, vb, (((1,), (1,)), ((), ())), preferred_element_type=F32)
            ds = p * (dp - delta)
            dsb = ds.astype(BF16)
            dk_ref[pl.ds(off, kc), :] += dk_mult * lax.dot_general(
                dsb, qb, (((0,), (0,)), ((), ())), preferred_element_type=F32)
            dv_ref[pl.ds(off, kc), :] += lax.dot_general(
                p.astype(BF16), dob, (((0,), (0,)), ((), ())), preferred_element_type=F32)
            if table:
                db_ref[...] += ds
            return dq + jnp.dot(dsb, kb, preferred_element_type=F32)

        dq0 = jnp.zeros((tq, HEAD), F32)
        dq = chunk(0, dq0) if nch == 1 else lax.fori_loop(0, nch, chunk, dq0)
        dq_ref[...] = (dq * scale).astype(dq_ref.dtype)

    in_specs = [
        pl.BlockSpec((tq, HEAD), lambda h, i: (i, qo + h)),
        pl.BlockSpec((Sk, HEAD), lambda h, i: (0, ko + h // G)),
        pl.BlockSpec((Sk, HEAD), lambda h, i: (0, vo + h // G)),
        pl.BlockSpec((tq, HEAD), lambda h, i: (i, h)),
        pl.BlockSpec((tq, HEAD), lambda h, i: (i, h)),
        pl.BlockSpec((tq, HEAD), lambda h, i: (i, h)),
    ]
    args = [q, k, v, o, do, lse]
    out_specs = [
        pl.BlockSpec((tq, HEAD), lambda h, i: (i, h)),
        pl.BlockSpec((Sk, HEAD), lambda h, i: (0, h // G)),
        pl.BlockSpec((Sk, HEAD), lambda h, i: (0, h // G)),
    ]
    out_shape = [
        jax.ShapeDtypeStruct((S, H * HEAD), BF16),
        jax.ShapeDtypeStruct((Sk, HK * HEAD), F32),
        jax.ShapeDtypeStruct((Sk, HK * HEAD), F32),
    ]
    if table:
        in_specs.append(pl.BlockSpec((None, None, tq, kw), lambda h, i: (var(i), h, 0, 0)))
        args.append(extra)
        out_specs.append(pl.BlockSpec((None, None, tq, kw), lambda h, i: (var(i), h, 0, 0)))
        out_shape.append(jax.ShapeDtypeStruct(extra.shape, F32))
    elif mode == "dil":
        in_specs.append(pl.BlockSpec((None, 8, kw), lambda h, i: (h, 0, 0)))
        args.append(extra)
    grid = (H, nq)
    body, c_in, c_in_specs, c_out, c_out_specs, c_scratch = _carried(body, len(args), len(out_shape), grid, carry)
    return pl.pallas_call(
        body, name=name, grid=grid, in_specs=in_specs + c_in_specs, out_specs=out_specs + c_out_specs,
        out_shape=out_shape + c_out, scratch_shapes=c_scratch,
        compiler_params=_params(("arbitrary", "arbitrary")),
    )(*args, *c_in)


def _na_constants():
    pairs = NA_VARIANTS * NA_QR * NA_KR
    sel = np.zeros((pairs, 16), np.float32)
    row_ok = np.zeros((pairs, 1), np.float32)
    for var in range(NA_VARIANTS):
        for qr in range(NA_QR):
            for kr in range(NA_KR):
                p = (var * NA_QR + qr) * NA_KR + kr
                first_key = (0, qr, NA_QR)[var]
                dr = kr - qr + (NA_WIN_R - 1) - (0, NA_QR, 2 * NA_QR)[var]
                if first_key <= kr < first_key + NA_WIN_R:
                    assert 0 <= dr < 2 * NA_WIN_R - 1
                    sel[p, dr] = 1.0
                    row_ok[p, 0] = 1.0
    j = np.arange(GRID_W)[:, None]
    c = np.arange(GRID_W)[None, :]
    cs = np.clip(j - NA_WIN_C // 2, 0, GRID_W - NA_WIN_C)
    col_ok = ((c >= cs) & (c < cs + NA_WIN_C)).astype(np.float32).reshape(1, GRID_W * GRID_W)
    dc = np.clip(c - j + NA_WIN_C - 1, 0, 2 * NA_WIN_C - 2).reshape(-1)
    toe = np.zeros((32, GRID_W * GRID_W), np.float32)
    toe[dc, np.arange(GRID_W * GRID_W)] = 1.0
    return sel, row_ok, col_ok, toe


def _split3(x):
    hi = x.astype(BF16)
    r1 = x - hi.astype(F32)
    mid = r1.astype(BF16)
    lo = (r1 - mid.astype(F32)).astype(BF16)
    return hi, mid, lo


def _dot_sel_left(onehot_bf16, x):
    return sum(jnp.dot(onehot_bf16, t, preferred_element_type=F32) for t in _split3(x))


def _dot_sel_right(x, onehot_bf16):
    return sum(jnp.dot(t, onehot_bf16, preferred_element_type=F32) for t in _split3(x))


def _na_bias_build(rpb_l):
    sel, row_ok, col_ok, toe = _na_constants()
    pairs = sel.shape[0]
    ww = GRID_W * GRID_W
    t = jnp.pad(rpb_l, ((0, 0), (0, 1), (0, 1)))

    def body(t_ref, sel_ref, toe_ref, rok_ref, cok_ref, o_ref):
        rows = _dot_sel_right(t_ref[...], toe_ref[...])
        val = _dot_sel_left(sel_ref[...], rows)
        ok = rok_ref[...] * cok_ref[...]
        o_ref[...] = jnp.where(ok > 0.5, val, NEG)

    full = lambda shape: pl.BlockSpec(shape, lambda h: (0,) * len(shape))
    out = pl.pallas_call(
        body, name="na_bias_build", grid=(NA_HEADS,),
        in_specs=[pl.BlockSpec((None, 16, 32), lambda h: (h, 0, 0)), full((pairs, 16)), full((32, ww)),
                  full((pairs, 1)), full((1, ww))],
        out_specs=pl.BlockSpec((None, pairs, ww), lambda h: (h, 0, 0)),
        out_shape=jax.ShapeDtypeStruct((NA_HEADS, pairs, ww), F32),
        compiler_params=_params(("arbitrary",)),
    )(t, jnp.asarray(sel, BF16), jnp.asarray(toe, BF16), jnp.asarray(row_ok), jnp.asarray(col_ok))
    out = out.reshape(NA_HEADS, NA_VARIANTS, NA_QR, NA_KR, GRID_W, GRID_W)
    out = out.transpose(1, 0, 2, 4, 3, 5)
    return out.reshape(NA_VARIANTS, NA_HEADS, NA_QR * GRID_W, NA_KR * GRID_W)


def _na_bias_grad(dbias):
    sel, _, _, toe = _na_constants()
    pairs = sel.shape[0]
    ww = GRID_W * GRID_W
    d = dbias.reshape(NA_VARIANTS, NA_HEADS, NA_QR, GRID_W, NA_KR, GRID_W).transpose(1, 0, 2, 4, 3, 5)
    d = d.reshape(NA_HEADS, pairs, ww)

    def body(d_ref, selt_ref, toet_ref, o_ref):
        rows = _dot_sel_left(selt_ref[...], d_ref[...])
        o_ref[...] = _dot_sel_right(rows, toet_ref[...])

    full = lambda shape: pl.BlockSpec(shape, lambda h: (0,) * len(shape))
    out = pl.pallas_call(
        body, name="na_bias_grad", grid=(NA_HEADS,),
        in_specs=[pl.BlockSpec((None, pairs, ww), lambda h: (h, 0, 0)), full((16, pairs)), full((ww, 32))],
        out_specs=pl.BlockSpec((None, 16, 32), lambda h: (h, 0, 0)),
        out_shape=jax.ShapeDtypeStruct((NA_HEADS, 16, 32), F32),
        compiler_params=_params(("arbitrary",)),
    )(d, jnp.asarray(sel.T, BF16), jnp.asarray(toe.T, BF16))
    return out[:, :2 * NA_WIN_R - 1, :2 * NA_WIN_C - 1]


def _sigmoid(x):
    return 1.0 / (1.0 + jnp.exp(-x))


def _swap32(y):
    lane = lax.broadcasted_iota(jnp.int32, y.shape, 1)
    up = pltpu.roll(y, HEAD - 32, 1)
    down = pltpu.roll(y, 32, 1)
    return jnp.where((lane // 32) % 2 == 0, up, down)


def _rope_tables(S):
    t = jnp.arange(S, dtype=jnp.int32)
    freqs = ROPE_BASE ** (-jnp.arange(32, dtype=F32) / 32)
    ar = (t // GRID_W).astype(F32)[:, None] * freqs[None, :]
    ac = (t % GRID_W).astype(F32)[:, None] * freqs[None, :]
    cos = jnp.concatenate([jnp.cos(ar), jnp.cos(ar), jnp.cos(ac), jnp.cos(ac)], axis=1)
    sin = jnp.concatenate([-jnp.sin(ar), jnp.sin(ar), -jnp.sin(ac), jnp.sin(ac)], axis=1)
    return cos, sin


def _rms_stats(x):
    x = x.astype(F32)
    r = lax.rsqrt(jnp.mean(x * x, axis=-1, keepdims=True) + EPS)
    return x * r, r


def _rms_bwd(xhat, r, dy, g):
    dxh = dy * g
    return r * (dxh - xhat * jnp.mean(dxh * xhat, axis=-1, keepdims=True))


def _qk_fwd(x, g, cos, sin):
    xhat, _ = _rms_stats(x)
    y = xhat * g
    return y * cos + _swap32(y) * sin


def _qk_bwd(dyr, x, g, cos, sin):
    dyr = dyr.astype(F32)
    dy = dyr * cos + _swap32(dyr * sin)
    xhat, r = _rms_stats(x)
    return _rms_bwd(xhat, r, dy, g), dy * xhat


def _perm(a, dil):
    if dil == 1:
        return a
    S = a.shape[0]
    return a.reshape(S // dil, dil, *a.shape[1:]).swapaxes(0, 1).reshape(a.shape)


def _unperm(a, dil):
    if dil == 1:
        return a
    S = a.shape[0]
    return a.reshape(dil, S // dil, *a.shape[1:]).swapaxes(0, 1).reshape(a.shape)


def _place():
    x, y, c = lax.axis_index("x"), lax.axis_index("y"), lax.axis_index("c")
    chips = [(1 - x, y), (x, 1 - y), (1 - x, 1 - y)]
    return x, y, c, chips


def _remote(src, dst, send, recv, dev):
    return pltpu.make_async_remote_copy(src_ref=src, dst_ref=dst, send_sem=send, recv_sem=recv,
                                        device_id=dev, device_id_type=MESH)


ANY = pl.BlockSpec(memory_space=pl.ANY)


class _Xfer:
    def __init__(self, send=None, recv=None, local=None):
        self.send, self.recv, self.local = send, recv, local

    def start(self):
        if self.local is not None:
            self.local().start()
        if self.send is not None:
            self.send().start()

    def wait(self):
        if self.recv is not None:
            self.recv().wait_recv()
        if self.send is not None:
            self.send().wait_send()
        if self.local is not None:
            self.local().wait()


def _sem_shapes(nsem):
    return [pltpu.SemaphoreType.DMA((nsem,))] * 3


def _carried(body, n_in, n_out, grid, carry):
    if carry is None:
        return body, [], [], [], [], []
    ins, out_shapes, plan, nsem = carry
    ci, co = len(ins), len(out_shapes)

    def new_body(*refs):
        a, cin = refs[:n_in], refs[n_in:n_in + ci]
        b = refs[n_in + ci:n_in + ci + n_out]
        cout = refs[n_in + ci + n_out:n_in + ci + n_out + co]
        rest = refs[n_in + ci + n_out + co:]
        scratch, sems = rest[:-3], rest[-3:]
        ids = [pl.program_id(d) for d in range(len(grid))]
        first = functools.reduce(jnp.logical_and, [i == 0 for i in ids])
        last = functools.reduce(jnp.logical_and, [i == g - 1 for i, g in zip(ids, grid)])

        @pl.when(first)
        def _():
            for t in plan(cin, cout, *sems):
                t.start()

        body(*a, *b, *scratch)

        @pl.when(last)
        def _():
            for t in plan(cin, cout, *sems):
                t.wait()

    return new_body, list(ins), [ANY] * ci, list(out_shapes), [ANY] * co, _sem_shapes(nsem)


def _exchange(name, carry):
    ins, out_shapes, plan, nsem = carry
    n = len(ins)

    def body(*refs):
        items = plan(refs[:n], refs[n:n + len(out_shapes)], *refs[-3:])
        for t in items:
            t.start()
        for t in items:
            t.wait()

    return pl.pallas_call(body, name=name, in_specs=[ANY] * n, out_specs=[ANY] * len(out_shapes),
                          out_shape=list(out_shapes), scratch_shapes=_sem_shapes(nsem))(*ins)


def _plan_gather(ins, outs, send, recv, lsem):
    x, y, c, chips = _place()
    me = 2 * x + y
    items = []
    for w, (src, dst) in enumerate(zip(ins, outs)):
        items.append(_Xfer(local=functools.partial(pltpu.make_async_copy, src, dst.at[me], lsem.at[w])))
        for r, (px, py) in enumerate(chips):
            k = 3 * w + r
            to = functools.partial(_remote, src, dst.at[me], send.at[k], recv.at[k], (px, py, c))
            frm = functools.partial(_remote, src, dst.at[2 * px + py], send.at[k], recv.at[k], (px, py, c))
            items.append(_Xfer(send=to, recv=frm))
    return items


def _carry_gather(shards):
    shapes = [jax.ShapeDtypeStruct((N_CHIPS,) + a.shape, a.dtype) for a in shards]
    return (shards, shapes, _plan_gather, 3 * len(shards))


def _plan_pair_exchange(ins, outs, send, recv, lsem):
    x, y, c, _ = _place()
    items = []
    for w, (src, dst) in enumerate(zip(ins, outs)):
        hr = src.shape[1] // 2
        theirs = pl.ds(pl.multiple_of((1 - c) * hr, 8), hr)
        cp = functools.partial(_remote, src.at[:, theirs, :], dst, send.at[w], recv.at[w], (x, y, 1 - c))
        items.append(_Xfer(send=cp, recv=cp))
    return items


def _carry_pair_exchange(gs):
    shapes = [jax.ShapeDtypeStruct((a.shape[0], a.shape[1] // 2, a.shape[2]), a.dtype) for a in gs]
    return (gs, shapes, _plan_pair_exchange, len(gs))


def _plan_chip_exchange(ins, outs, send, recv, lsem):
    x, y, c, chips = _place()
    items = []
    for w, (src, dst) in enumerate(zip(ins, outs)):
        for r, (px, py) in enumerate(chips):
            k = 3 * w + r
            cp = functools.partial(_remote, src.at[2 * px + py], dst.at[r], send.at[k], recv.at[k], (px, py, c))
            items.append(_Xfer(send=cp, recv=cp))
    return items


def _carry_chip_exchange(qs):
    shapes = [jax.ShapeDtypeStruct((3,) + a.shape[1:], a.dtype) for a in qs]
    return (qs, shapes, _plan_chip_exchange, 3 * len(qs))


def _plan_pair_complete(ins, outs, send, recv, lsem):
    x, y, c, _ = _place()
    sib = (x, y, 1 - c)
    items = []
    for w, (src, dst) in enumerate(zip(ins, outs)):
        hr = src.shape[0]
        mine = pl.ds(pl.multiple_of(c * hr, 8), hr)
        other = pl.ds(pl.multiple_of((1 - c) * hr, 8), hr)
        items.append(_Xfer(send=functools.partial(_remote, src, dst.at[mine, :], send.at[w], recv.at[w], sib),
                           recv=functools.partial(_remote, src, dst.at[other, :], send.at[w], recv.at[w], sib),
                           local=functools.partial(pltpu.make_async_copy, src, dst.at[mine, :], lsem.at[w])))
    return items


def _carry_pair_complete(rs):
    shapes = [jax.ShapeDtypeStruct((2 * a.shape[0], a.shape[1]), a.dtype) for a in rs]
    return (rs, shapes, _plan_pair_complete, len(rs))


def _allgather_weights(ws):
    n = len(ws)

    def body(*refs):
        ins, outs = refs[:n], refs[n:2 * n]
        send, recv, lsem = refs[2 * n:]
        x, y, c, chips = _place()
        me = 2 * x + y
        sib = (x, y, 1 - c)
        sends = []
        local = []
        for w in range(n):
            hr = ins[w].shape[1] // 2
            mine = pl.ds(pl.multiple_of(c * hr, 16), hr)
            cp = pltpu.make_async_copy(ins[w], outs[w].at[:, me], lsem.at[w])
            cp.start()
            local.append(cp)
            for r, (px, py) in enumerate(chips):
                cp = _remote(ins[w].at[:, mine, :], outs[w].at[:, me, mine, :], send.at[w, r], recv.at[w, r],
                             (px, py, c))
                cp.start()
                sends.append(cp)
        for w in range(n):
            hr = ins[w].shape[1] // 2
            mine = pl.ds(pl.multiple_of(c * hr, 16), hr)
            for r, (px, py) in enumerate(chips):
                blk = outs[w].at[:, 2 * px + py, mine, :]
                _remote(blk, blk, send.at[w, r], recv.at[w, r], (px, py, c)).wait_recv()
                cp = _remote(blk, blk, send.at[w, 3 + r], recv.at[w, 3 + r], sib)
                cp.start()
                sends.append(cp)
        for w in range(n):
            hr = ins[w].shape[1] // 2
            other = pl.ds(pl.multiple_of((1 - c) * hr, 16), hr)
            for r, (px, py) in enumerate(chips):
                blk = outs[w].at[:, 2 * px + py, other, :]
                _remote(blk, blk, send.at[w, 3 + r], recv.at[w, 3 + r], sib).wait_recv()
        for cp in sends:
            cp.wait_send()
        for cp in local:
            cp.wait()

    out_shape = [jax.ShapeDtypeStruct((a.shape[0], N_CHIPS) + a.shape[1:], a.dtype) for a in ws]
    return pl.pallas_call(
        body, name="allgather_weights", in_specs=[ANY] * n, out_specs=[ANY] * n, out_shape=out_shape,
        scratch_shapes=[pltpu.SemaphoreType.DMA((n, 6)), pltpu.SemaphoreType.DMA((n, 6)),
                        pltpu.SemaphoreType.DMA((n,))],
    )(*ws)


def _allreduce_small(vec, loss_rows, loss_scale):
    rows = vec.shape[0]

    def body(v_ref, tot_ref, loss_ref, gat_ref, send, recv):
        x, y, c, _ = _place()
        me = 4 * x + 2 * y + c
        gat_ref[me] = v_ref[...]
        peers = []
        for r in range(1, N_DEV):
            px = 1 - x if r & 4 else x
            py = 1 - y if r & 2 else y
            pc = 1 - c if r & 1 else c
            peers.append((px, py, pc))
            _remote(v_ref, gat_ref.at[me], send.at[r - 1], recv.at[r - 1], (px, py, pc)).start()
        for r, (px, py, pc) in enumerate(peers):
            _remote(v_ref, gat_ref.at[4 * px + 2 * py + pc], send.at[r], recv.at[r], (px, py, pc)).wait_recv()
        for r, (px, py, pc) in enumerate(peers):
            _remote(v_ref, gat_ref.at[me], send.at[r], recv.at[r], (px, py, pc)).wait_send()
        tot = gat_ref[0]
        for d in range(1, N_DEV):
            tot = tot + gat_ref[d]
        tot_ref[...] = tot
        loss_ref[...] = jnp.full(loss_ref.shape, loss_scale, F32) * jnp.sum(tot[:loss_rows])

    vm = pl.BlockSpec(memory_space=pltpu.VMEM)
    tot, loss = pl.pallas_call(
        body, name="allreduce_small", in_specs=[vm], out_specs=[vm, vm],
        out_shape=[jax.ShapeDtypeStruct((rows, LANE), F32), jax.ShapeDtypeStruct((8, LANE), F32)],
        scratch_shapes=[pltpu.VMEM((N_DEV, rows, LANE), F32), pltpu.SemaphoreType.DMA((N_DEV - 1,)),
                        pltpu.SemaphoreType.DMA((N_DEV - 1,))],
        compiler_params=_params(),
    )(vec)
    return tot, loss


def _add_rows(name, terms):
    shape = terms[0].shape
    C = shape[-1]
    flat = [t.reshape(-1, C) for t in terms]
    rows = flat[0].shape[0]
    width = _tile(C, 1024)

    def fn(*vals):
        tot = vals[0]
        for val in vals[1:]:
            tot = tot + val
        return tot

    (out,) = _ew(name, fn, [(t, "t", 0) for t in flat], [(C, F32)], rows=rows, width=width, ncols=C // width,
                 ts=512)
    return out.reshape(shape)


def _adamw(name, w, g, m, v):
    shape = w.shape
    C = shape[-1]
    flat = [t.reshape(-1, C) for t in (w, g, m, v)]
    rows = flat[0].shape[0]
    width = _tile(C, 1024)

    def fn(w, g, m, v):
        m2 = ADAM_B1 * m + (1.0 - ADAM_B1) * g
        v2 = ADAM_B2 * v + (1.0 - ADAM_B2) * (g * g)
        m_hat = m2 / (1.0 - ADAM_B1 ** ADAM_STEP)
        v_hat = v2 / (1.0 - ADAM_B2 ** ADAM_STEP)
        delta = -ADAM_LR * (m_hat / (jnp.sqrt(v_hat) + ADAM_EPS) + ADAM_WD * w)
        return delta, m2, v2

    outs = _ew(name, fn, [(t, "t", 0) for t in flat], [(C, F32)] * 3, rows=rows, width=width,
               ncols=C // width, ts=min(512, rows))
    return [o.reshape(shape) for o in outs]


def _na_geometry(S):
    rows = S // GRID_W
    nq = rows // NA_QR
    tq = NA_QR * GRID_W
    kw = NA_KR * GRID_W

    def kstart(i):
        return jnp.clip(NA_QR * i - NA_WIN_R // 2, 0, rows - NA_KR) * GRID_W

    def var(i):
        return jnp.where(i == 0, 0, jnp.where(i == nq - 1, 2, 1))

    return dict(S=S, tq=tq, kw=kw, kc=kw, kstart=kstart, var=var, mode="table")


def _dil_geometry(S, dil):
    L = S // dil
    tq = min(512, L)
    kw = tq + 2 * DIL_REACH
    return dict(S=S, tq=tq, kw=kw, kc=kw, kstart=lambda i: i * tq, mode="dil", dil=dil, L=L)


def _dil_slopes(g, kw):
    idx = jnp.arange(1, DIL_HEADS + 1, dtype=F32)[g * DIL_HPG:(g + 1) * DIL_HPG]
    slopes = 2.0 ** (-ALIBI_MAX_EXP * idx / DIL_HEADS)
    return jnp.broadcast_to(slopes[:, None, None], (DIL_HPG, 8, kw))


def _layer_fwd(x, p, cos, sin, next_shards=None):
    S, D = x.shape
    ts = 256
    gp = p["pre_g"][None, :]
    (h,) = _ew("pre_norm", lambda x, g: _rms_stats(x)[0] * g, [(x, "t", 0), (gp, "v", 0)], [(D, BF16)],
               rows=S, width=D, ncols=1, ts=ts)
    proj = _mm_nn("proj_in", h, p["w_in"], out_dtype=BF16)

    bias = _na_bias_build(p["rpb"])
    ya, lse_a = _attn_fwd("na_fwd", proj, C_QA, proj, C_KA, proj, C_VA, H=NA_HEADS, G=1, extra=bias,
                          **_na_geometry(S))

    def qk(name, off, nh, g, mult):
        fn = lambda x, g, cos, sin: _qk_fwd(x, g, cos, sin) * mult
        (out,) = _ew(name, fn, [(proj, "t", off), (g[None, :], "vf", 0), (cos, "tf", 0), (sin, "tf", 0)],
                     [(nh * HEAD, BF16)], rows=S, width=HEAD, ncols=nh, ts=512)
        return out

    qb = qk("q_norm_rope", C_QB, GQ_HEADS, p["q_g"], HEAD ** -0.5 * float(np.log2(np.e)))
    kb = qk("k_norm_rope", C_KB, GKV_HEADS, p["k_g"], 1.0)
    tq_b = min(512, S)
    kc_b = min(1024, S)
    geo_b = dict(S=S, tq=tq_b, kw=S, kc=kc_b, kstart=lambda i: 0, prescaled=True)
    carry = None if next_shards is None else _carry_gather(next_shards)
    yb, lse_b, *next_full = _attn_fwd("gqa_fwd", qb, 0, kb, 0, proj, C_VB, H=GQ_HEADS, G=GQ_HEADS // GKV_HEADS,
                                      carry=carry, **geo_b)

    W = DIL_HPG * HEAD
    og, lg, saved_c = [], [], []
    for g, (win, dil) in enumerate(DIL_GROUPS):
        assert (win // 2) // dil == DIL_REACH
        sl = lambda off: proj[:, (off + g * DIL_HPG) * HEAD:(off + (g + 1) * DIL_HPG) * HEAD]
        qg = _perm(sl(C_QC), dil)
        kg = jnp.pad(_perm(sl(C_KC), dil), ((DIL_REACH, DIL_REACH), (0, 0)))
        vg = jnp.pad(_perm(sl(C_VC), dil), ((DIL_REACH, DIL_REACH), (0, 0)))
        geo = _dil_geometry(S, dil)
        o, l = _attn_fwd(f"dil{g}_fwd", qg, 0, kg, 0, vg, 0, H=DIL_HPG, G=1, extra=_dil_slopes(g, geo["kw"]),
                         out_dtype=F32, **geo)
        og.append(_unperm(o, dil))
        lg.append(_unperm(l, dil))
        saved_c.append((qg, kg, vg))

    def combine(o1, o2, o3, l1, l2, l3):
        m = jnp.maximum(jnp.maximum(l1, l2), l3)
        e1, e2, e3 = jnp.exp(l1 - m), jnp.exp(l2 - m), jnp.exp(l3 - m)
        tot = e1 + e2 + e3
        return (e1 * o1 + e2 * o2 + e3 * o3) / tot, m + jnp.log(tot)

    yc, lse_c = _ew("dil_combine", combine, [(a, "t", 0) for a in og + lg], [(W, BF16), (W, F32)],
                    rows=S, width=W, ncols=1, ts=512)

    def gate(y, z):
        z = z.astype(F32)
        return y.astype(F32) * (z * _sigmoid(z))

    def branch(name, y, zoff, width, w):
        unit = width // HEAD
        (u,) = _ew(name + "_silu", gate, [(y, "t", 0), (proj, "tf", zoff // unit)], [(width, BF16)],
                   rows=S, width=width, ncols=1, ts=512)
        return u, _mm_nn(name + "_proj", u, w, out_dtype=BF16)

    ua, ta = branch("branch_a", ya, C_ZA, NA_HEADS * HEAD, p["w_a"])
    ub, tb = branch("branch_b", yb, C_ZB, GQ_HEADS * HEAD, p["w_b"])
    uc, tc = branch("branch_c", yc, C_ZC, W, p["w_c"])

    mw = 512
    nm = D // mw
    goff = C_GATE * HEAD // mw
    bg = p["b_gate"][None, :]

    def merge(ta, tb, tc, ga, gb, gc, ba, bb, bc):
        sa = _sigmoid(ga.astype(F32) + ba)
        sb = _sigmoid(gb.astype(F32) + bb)
        sc = _sigmoid(gc.astype(F32) + bc)
        return sa * ta.astype(F32) + sb * tb.astype(F32) + sc * tc.astype(F32)

    (merged,) = _ew("merge", merge,
                    [(ta, "t", 0), (tb, "t", 0), (tc, "t", 0),
                     (proj, "t", goff), (proj, "t", goff + nm), (proj, "t", goff + 2 * nm),
                     (bg, "v", 0), (bg, "v", nm), (bg, "v", 2 * nm)],
                    [(D, BF16)], rows=S, width=mw, ncols=nm, ts=512)
    out = _mm_nn("proj_out", merged, p["w_out"], out_dtype=F32)
    gq = p["post_g"][None, :]
    (x_next,) = _ew("post_norm_residual", lambda x, o, g: x + _rms_stats(o)[0] * g,
                    [(x, "t", 0), (out, "t", 0), (gq, "v", 0)], [(D, F32)], rows=S, width=D, ncols=1, ts=ts)
    saved = dict(x=x, h=h, proj=proj, bias=bias, ya=ya, lse_a=lse_a, qb=qb, kb=kb, yb=yb, lse_b=lse_b,
                 geo_b=geo_b, saved_c=saved_c, yc=yc, lse_c=lse_c, ua=ua, ub=ub, uc=uc, ta=ta, tb=tb, tc=tc,
                 merged=merged, out=out)
    return x_next, saved, next_full


def _pair_sums(gs, got):
    c = lax.axis_index("c")
    out = []
    for g, t in zip(gs, got):
        hr = g.shape[1] // 2
        out.append(_add_rows("grad_pair_sum", [lax.dynamic_slice_in_dim(g, c * hr, hr, axis=1), t]))
    return out


def _chip_sums(pair, recv):
    me = 2 * lax.axis_index("x") + lax.axis_index("y")
    out = []
    for q, t in zip(pair, recv):
        own = lax.dynamic_index_in_dim(q, me, axis=0, keepdims=False)
        out.append(_add_rows("grad_chip_sum", [own, t[0], t[1], t[2]]))
    return out


def _layer_bwd(gy, p, s, cos, sin, pending=None):
    S, D = gy.shape
    ts = 256
    proj = s["proj"]
    grads = {}

    def post_bwd(o, dy, g):
        xhat, r = _rms_stats(o)
        return _rms_bwd(xhat, r, dy, g), dy * xhat

    dout, dg_post = _ew("post_norm_bwd", post_bwd, [(s["out"], "t", 0), (gy, "t", 0), (p["post_g"][None, :], "v", 0)],
                        [(D, BF16)], accs=("n",), rows=S, width=D, ncols=1, ts=ts)
    grads["post_g"] = dg_post[0]
    dmerged = _mm_nt("proj_out_bwd", dout, p["w_out"], out_dtype=BF16)
    grads["w_out"] = _mm_tn("proj_out_wgrad", s["merged"], dout, blocks=1).reshape(N_CHIPS, D // N_CHIPS, D)

    mw = 512
    nm = D // mw
    goff = C_GATE * HEAD // mw
    bg = p["b_gate"][None, :]

    def merge_bwd(dm, ta, tb, tc, ga, gb, gc, ba, bb, bc):
        dm = dm.astype(F32)
        res_t, res_g = [], []
        for t, g, b in ((ta, ga, ba), (tb, gb, bb), (tc, gc, bc)):
            sg = _sigmoid(g.astype(F32) + b)
            res_t.append(dm * sg)
            res_g.append(dm * t.astype(F32) * sg * (1.0 - sg))
        return (*res_t, *res_g, *res_g)

    mres = _ew("merge_bwd", merge_bwd,
               [(dmerged, "t", 0), (s["ta"], "t", 0), (s["tb"], "t", 0), (s["tc"], "t", 0),
                (proj, "t", goff), (proj, "t", goff + nm), (proj, "t", goff + 2 * nm),
                (bg, "v", 0), (bg, "v", nm), (bg, "v", 2 * nm)],
               [(D, BF16)] * 6, accs=("n", "n", "n"), rows=S, width=mw, ncols=nm, ts=512,
               carry=None if pending is None else _carry_pair_exchange(pending))
    dta, dtb, dtc, dga, dgb, dgc = mres[:6]
    grads["b_gate"] = jnp.concatenate([a[0] for a in mres[6:9]])
    pair = None if pending is None else _pair_sums(pending, mres[9:])

    def silu_bwd(du, y, z):
        du, y, z = du.astype(F32), y.astype(F32), z.astype(F32)
        sg = _sigmoid(z)
        return du * (z * sg), du * y * (sg * (1.0 + z * (1.0 - sg)))

    def branch_bwd(name, dt, u, y, zoff, width, w):
        du = _mm_nt(name + "_proj_bwd", dt, w, out_dtype=BF16)
        gw = _mm_tn(name + "_wgrad", u, dt, blocks=N_CHIPS)
        unit = width // HEAD
        dy, dz = _ew(name + "_silu_bwd", silu_bwd, [(du, "t", 0), (y, "t", 0), (proj, "tf", zoff // unit)],
                     [(width, BF16), (width, BF16)], rows=S, width=width, ncols=1, ts=512)
        return dy, dz, gw

    W = DIL_HPG * HEAD
    dya, dza, grads["w_a"] = branch_bwd("branch_a", dta, s["ua"], s["ya"], C_ZA, NA_HEADS * HEAD, p["w_a"])
    dyb, dzb, grads["w_b"] = branch_bwd("branch_b", dtb, s["ub"], s["yb"], C_ZB, GQ_HEADS * HEAD, p["w_b"])
    dyc, dzc, grads["w_c"] = branch_bwd("branch_c", dtc, s["uc"], s["yc"], C_ZC, W, p["w_c"])

    dqa, dka, dva, dbias = _attn_bwd("na_bwd", proj, C_QA, proj, C_KA, proj, C_VA, s["ya"], dya, s["lse_a"],
                                     H=NA_HEADS, G=1, extra=s["bias"], **_na_geometry(S))
    grads["rpb"] = _na_bias_grad(dbias)

    geo_b = dict(s["geo_b"])
    geo_b["kc"] = min(512, S)
    dqr, dkr, dvb, *recv = _attn_bwd("gqa_bwd", s["qb"], 0, s["kb"], 0, proj, C_VB, s["yb"], dyb, s["lse_b"],
                                     H=GQ_HEADS, G=GQ_HEADS // GKV_HEADS,
                                     carry=None if pending is None else _carry_chip_exchange(pair), **geo_b)
    half = None if pending is None else _chip_sums(pair, recv)

    def qk_bwd(name, dyr, off, nh, g):
        return _ew(name, _qk_bwd, [(dyr, "t", 0), (proj, "t", off), (g[None, :], "vf", 0), (cos, "tf", 0),
                                   (sin, "tf", 0)],
                   [(nh * HEAD, BF16)], accs=("f",), rows=S, width=HEAD, ncols=nh, ts=512)

    dqb, dgq = qk_bwd("q_norm_rope_bwd", dqr, C_QB, GQ_HEADS, p["q_g"])
    dkb, dgk = qk_bwd("k_norm_rope_bwd", dkr, C_KB, GKV_HEADS, p["k_g"])
    grads["q_g"] = dgq[0]
    grads["k_g"] = dgk[0]

    dqc, dkc, dvc = [], [], []
    for g, (win, dil) in enumerate(DIL_GROUPS):
        qg, kg, vg = s["saved_c"][g]
        geo = _dil_geometry(S, dil)
        dq, dk, dv = _attn_bwd(f"dil{g}_bwd", qg, 0, kg, 0, vg, 0, _perm(s["yc"], dil), _perm(dyc, dil),
                               _perm(s["lse_c"], dil), H=DIL_HPG, G=1, extra=_dil_slopes(g, geo["kw"]), **geo)
        dqc.append(_unperm(dq, dil))
        dkc.append(_unperm(dk[DIL_REACH:-DIL_REACH], dil).astype(BF16))
        dvc.append(_unperm(dv[DIL_REACH:-DIL_REACH], dil).astype(BF16))

    dproj = jnp.concatenate(
        [dqa, dka.astype(BF16), dva.astype(BF16), dqb, dkb, dvb.astype(BF16), *dqc, *dkc, *dvc,
         dza, dzb, dzc, dga, dgb, dgc], axis=1)
    if pending is None:
        dh, done = _mm_nt("proj_in_bwd", dproj, p["w_in"], out_dtype=F32), None
    else:
        dh, *done = _mm_nt("proj_in_bwd", dproj, p["w_in"], out_dtype=F32, carry=_carry_pair_complete(half))
    grads["w_in"] = _mm_tn("proj_in_wgrad", s["h"], dproj, blocks=N_CHIPS)

    def pre_bwd(x, dh, g, gy):
        xhat, r = _rms_stats(x)
        return gy + _rms_bwd(xhat, r, dh, g), dh * xhat

    dx, dg_pre = _ew("pre_norm_bwd", pre_bwd, [(s["x"], "t", 0), (dh, "t", 0), (p["pre_g"][None, :], "v", 0),
                                                 (gy, "t", 0)],
                     [(D, F32)], accs=("n",), rows=S, width=D, ncols=1, ts=ts)
    grads["pre_g"] = dg_pre[0]
    return dx, grads, done


BIG = ("w_in", "w_a", "w_b", "w_c", "w_out")
SMALL = ("pre_g", "b_gate", "q_g", "k_g", "rpb", "post_g")


def _reduce_big(gs):
    pair = _pair_sums(gs, _exchange("grad_pair_exchange", _carry_pair_exchange(gs)))
    half = _chip_sums(pair, _exchange("grad_chip_exchange", _carry_chip_exchange(pair)))
    return _exchange("grad_pair_complete", _carry_pair_complete(half))


def _pack(parts, rows_mult=8):
    flat = jnp.concatenate([a.reshape(-1) for a in parts])
    n = flat.shape[0]
    unit = rows_mult * LANE
    padded = -(-n // unit) * unit
    return jnp.pad(flat, (0, padded - n)).reshape(-1, LANE)


def kernel(x, pre_norm_g, w_in, b_gate, q_norm_g, k_norm_g, rpb, w_branch_a, w_branch_b, w_branch_c, w_out, post_norm_g, loss_target, m_pre_norm_g, m_w_in, m_b_gate, m_q_norm_g, m_k_norm_g, m_rpb, m_w_branch_a, m_w_branch_b, m_w_branch_c, m_w_out, m_post_norm_g, v_pre_norm_g, v_w_in, v_b_gate, v_q_norm_g, v_k_norm_g, v_rpb, v_w_branch_a, v_w_branch_b, v_w_branch_c, v_w_out, v_post_norm_g):
    n_layers = w_in.shape[0]
    S, D = x.shape[1], x.shape[2]
    big_w = dict(w_in=w_in, w_a=w_branch_a, w_b=w_branch_b, w_c=w_branch_c, w_out=w_out)
    big_m = dict(w_in=m_w_in, w_a=m_w_branch_a, w_b=m_w_branch_b, w_c=m_w_branch_c, w_out=m_w_out)
    big_v = dict(w_in=v_w_in, w_a=v_w_branch_a, w_b=v_w_branch_b, w_c=v_w_branch_c, w_out=v_w_out)
    small_w = dict(pre_g=pre_norm_g, b_gate=b_gate, q_g=q_norm_g, k_g=k_norm_g, rpb=rpb, post_g=post_norm_g)
    small_m = dict(pre_g=m_pre_norm_g, b_gate=m_b_gate, q_g=m_q_norm_g, k_g=m_k_norm_g, rpb=m_rpb,
                   post_g=m_post_norm_g)
    small_v = dict(pre_g=v_pre_norm_g, b_gate=v_b_gate, q_g=v_q_norm_g, k_g=v_k_norm_g, rpb=v_rpb,
                   post_g=v_post_norm_g)

    shards = [[big_w[n][l].astype(BF16) for n in BIG] for l in range(n_layers)]
    cos, sin = _rope_tables(S)

    def layer_params(l, full):
        p = dict(zip(BIG, full))
        p["w_out"] = p["w_out"].reshape(D, D)
        p.update(pre_g=pre_norm_g[l], b_gate=b_gate[l], q_g=q_norm_g[l], k_g=k_norm_g[l], rpb=rpb[l],
                 post_g=post_norm_g[l])
        return p

    full = [a[0] for a in _allgather_weights([a[None] for a in shards[0]])]
    act = x[0]
    saved, params = [], []
    for l in range(n_layers):
        params.append(layer_params(l, full))
        act, s, full = _layer_fwd(act, params[l], cos, sin, shards[l + 1] if l + 1 < n_layers else None)
        saved.append(s)

    dy, loss_cols = _ew("loss", lambda y, t: ((y - t) * (1.0 / D), (y - t) * (y - t)),
                        [(act, "t", 0), (loss_target[0], "t", 0)], [(D, F32)], accs=("n",),
                        rows=S, width=D, ncols=1, ts=256)

    big_g = {n: [None] * n_layers for n in BIG}
    small_g = {n: [None] * n_layers for n in SMALL}
    pending = None
    for l in reversed(range(n_layers)):
        dy, grads, done = _layer_bwd(dy, params[l], saved[l], cos, sin, pending)
        if done is not None:
            for n, r in zip(BIG, done):
                big_g[n][l + 1] = r
        pending = [grads[n] for n in BIG]
        for n in SMALL:
            small_g[n][l] = grads[n]
    for n, r in zip(BIG, _reduce_big(pending)):
        big_g[n][0] = r

    loss_rows = -(-D // (8 * LANE)) * 8
    parts = [_pack([loss_cols[0]])] + [_pack([jnp.stack(small_g[n]) for n in SMALL])]
    tot, loss = _allreduce_small(jnp.concatenate(parts), loss_rows, 0.5 / D)
    wvec = jnp.concatenate([jnp.zeros((loss_rows, LANE), F32), _pack([small_w[n] for n in SMALL])])
    mvec = jnp.concatenate([jnp.zeros((loss_rows, LANE), F32), _pack([small_m[n] for n in SMALL])])
    vvec = jnp.concatenate([jnp.ones((loss_rows, LANE), F32), _pack([small_v[n] for n in SMALL])])
    small_out = [tot] + _adamw("adamw_small", wvec, tot, mvec, vvec)

    def unpack(vec):
        flat = vec[loss_rows:].reshape(-1)
        res, pos = {}, 0
        for n in SMALL:
            size = int(np.prod(small_w[n].shape))
            res[n] = flat[pos:pos + size].reshape(small_w[n].shape)
            pos += size
        return res

    small_res = [unpack(v) for v in small_out]

    big_res = [{}, {}, {}, {}]
    for n in BIG:
        g = jnp.stack(big_g[n])
        big_res[0][n] = g
        for k, val in enumerate(_adamw("adamw_" + n, big_w[n], g, big_m[n], big_v[n])):
            big_res[k + 1][n] = val

    order = (("pre_g", small_res), ("w_in", big_res), ("b_gate", small_res), ("q_g", small_res),
             ("k_g", small_res), ("rpb", small_res), ("w_a", big_res), ("w_b", big_res), ("w_c", big_res),
             ("w_out", big_res), ("post_g", small_res))
    outs = [loss[0, 0], dy[None]]
    for k in range(4):
        outs.extend(src[k][n] for n, src in order)
    return tuple(outs)
```

```python
import functools

import numpy as np
import jax
import jax.numpy as jnp
from jax import lax
from jax.experimental import pallas as pl
from jax.experimental.pallas import tpu as pltpu

F32 = jnp.float32
BF16 = jnp.bfloat16
MESH = pl.DeviceIdType.MESH

HEAD = 128
GRID_W = 64
EPS = 1e-6
NEG = -1e30
NA_HEADS = 8
NA_WIN_R = 8
NA_WIN_C = 16
GQ_HEADS = 8
GKV_HEADS = 2
ROPE_BASE = 10000.0
DIL_GROUPS = ((128, 1), (512, 4), (2048, 16))
DIL_REACH = 64
DIL_HPG = 4
DIL_HEADS = 12
ALIBI_MAX_EXP = 8.0
ADAM_LR = 0.001
ADAM_B1 = 0.9
ADAM_B2 = 0.999
ADAM_EPS = 1e-08
ADAM_WD = 0.01
ADAM_STEP = 10

C_QA, C_KA, C_VA = 0, 8, 16
C_QB, C_KB, C_VB = 24, 32, 34
C_QC, C_KC, C_VC = 36, 48, 60
C_ZA, C_ZB, C_ZC = 72, 80, 88
C_GATE = 92

NA_QR = 4
NA_KR = NA_QR + NA_WIN_R
NA_VARIANTS = 3

GQA_FWD_TILE = (1024, 1024)
GQA_BWD_TILE = (1024, 1024)

VMEM_LIMIT_BYTES = 48 * 1024 * 1024
LANE = 128
N_CHIPS = 4
N_DEV = 8


def _params(sem=None):
    kw = dict(vmem_limit_bytes=VMEM_LIMIT_BYTES)
    if sem is not None:
        kw["dimension_semantics"] = sem
    return pltpu.CompilerParams(**kw)


def _tile(n, cap):
    units = n // LANE
    assert units * LANE == n, n
    best = 1
    for d in range(1, units + 1):
        if units % d == 0 and d * LANE <= cap:
            best = d
    return best * LANE


def _ew(name, fn, ins, outs, accs=(), *, rows, width, ncols, ts, carry=None):
    ts = min(ts, rows)
    assert rows % ts == 0
    nr = rows // ts
    ni, no = len(ins), len(outs)

    def imap(mode, off):
        if mode == "t":
            return lambda n, i: (i, off + n)
        if mode == "tf":
            return lambda n, i: (i, off)
        if mode == "v":
            return lambda n, i: (0, off + n)
        return lambda n, i: (0, off)

    in_specs = [pl.BlockSpec((ts if m in ("t", "tf") else 1, width), imap(m, o)) for (_, m, o) in ins]
    out_specs = [pl.BlockSpec((ts, width), lambda n, i: (i, n)) for _ in outs]
    out_shape = [jax.ShapeDtypeStruct((rows, c), d) for (c, d) in outs]
    for a in accs:
        if a == "n":
            out_specs.append(pl.BlockSpec((8, width), lambda n, i: (0, n)))
            out_shape.append(jax.ShapeDtypeStruct((8, width * ncols), F32))
        else:
            out_specs.append(pl.BlockSpec((8, width), lambda n, i: (0, 0)))
            out_shape.append(jax.ShapeDtypeStruct((8, width), F32))

    def body(*refs):
        res = fn(*[r[...] for r in refs[:ni]])
        if not isinstance(res, tuple):
            res = (res,)
        for r, val in zip(refs[ni:ni + no], res[:no]):
            r[...] = val.astype(r.dtype)
        n = pl.program_id(0)
        i = pl.program_id(1)
        for kind, r, val in zip(accs, refs[ni + no:], res[no:]):
            first = (i == 0) if kind == "n" else jnp.logical_and(i == 0, n == 0)

            @pl.when(first)
            def _(r=r):
                r[...] = jnp.zeros(r.shape, r.dtype)

            r[...] += jnp.broadcast_to(jnp.sum(val.astype(F32), axis=0, keepdims=True), r.shape)

    grid = (ncols, nr)
    body, c_in, c_in_specs, c_out, c_out_specs, c_scratch = _carried(body, ni, no + len(accs), grid, carry)
    res = pl.pallas_call(
        body, name=name, grid=grid, in_specs=in_specs + c_in_specs, out_specs=out_specs + c_out_specs,
        out_shape=out_shape + c_out, scratch_shapes=c_scratch,
        compiler_params=_params(("arbitrary", "arbitrary")),
    )(*[a for (a, _, _) in ins], *c_in)
    return res


def _mm_nn(name, a, w, *, out_dtype, tm=512, tn_cap=1024):
    M, K = a.shape
    blocked = w.ndim == 3
    NB = w.shape[-1]
    N = NB * (w.shape[0] if blocked else 1)
    tm = min(tm, M)
    tn = _tile(NB, tn_cap)
    per = NB // tn
    if blocked:
        w_spec = pl.BlockSpec((None, K, tn), lambda i, n: (n // per, 0, n % per))
    else:
        w_spec = pl.BlockSpec((K, tn), lambda i, n: (0, n))

    def body(a_ref, w_ref, o_ref):
        o_ref[...] = jnp.dot(a_ref[...], w_ref[...], preferred_element_type=F32).astype(o_ref.dtype)

    return pl.pallas_call(
        body, name=name, grid=(M // tm, N // tn),
        in_specs=[pl.BlockSpec((tm, K), lambda i, n: (i, 0)), w_spec],
        out_specs=pl.BlockSpec((tm, tn), lambda i, n: (i, n)),
        out_shape=jax.ShapeDtypeStruct((M, N), out_dtype),
        compiler_params=_params(("arbitrary", "arbitrary")),
    )(a, w)


def _mm_nt(name, a, w, *, out_dtype, tm=512, tk_cap=1024, carry=None, w_transposed=False):
    M, N = a.shape
    blocked = w.ndim == 3
    NB = w.shape[0] if w_transposed else w.shape[-1]
    K = w.shape[1] if w_transposed else w.shape[-2]
    assert N == NB * (w.shape[0] if blocked else 1)
    tm = min(tm, M)
    tk = _tile(NB, tk_cap)
    per = NB // tk
    nk = N // tk
    if w_transposed:
        w_spec = pl.BlockSpec((tk, K), lambda i, k: (k, 0))
        contract = (((1,), (0,)), ((), ()))
    elif blocked:
        w_spec = pl.BlockSpec((None, K, tk), lambda i, k: (k // per, 0, k % per))
        contract = (((1,), (1,)), ((), ()))
    else:
        w_spec = pl.BlockSpec((K, tk), lambda i, k: (0, k))
        contract = (((1,), (1,)), ((), ()))

    def body(a_ref, w_ref, o_ref, acc_ref):
        k = pl.program_id(1)

        @pl.when(k == 0)
        def _():
            acc_ref[...] = jnp.zeros(acc_ref.shape, F32)

        acc_ref[...] += lax.dot_general(a_ref[...], w_ref[...], contract, preferred_element_type=F32)

        @pl.when(k == nk - 1)
        def _():
            o_ref[...] = acc_ref[...].astype(o_ref.dtype)

    grid = (M // tm, nk)
    body, c_in, c_in_specs, c_out, c_out_specs, c_scratch = _carried(body, 2, 1, grid, carry)
    res = pl.pallas_call(
        body, name=name, grid=grid,
        in_specs=[pl.BlockSpec((tm, tk), lambda i, k: (i, k)), w_spec] + c_in_specs,
        out_specs=[pl.BlockSpec((tm, K), lambda i, k: (i, 0))] + c_out_specs,
        out_shape=[jax.ShapeDtypeStruct((M, K), out_dtype)] + c_out,
        scratch_shapes=[pltpu.VMEM((tm, K), F32)] + c_scratch,
        compiler_params=_params(("arbitrary", "arbitrary")),
    )(a, w, *c_in)
    return res[0] if carry is None else res


def _mm_tn(name, a, b, *, blocks, ts=512, tn_cap=1024):
    S, K = a.shape
    N = b.shape[1]
    NB = N // blocks
    ts = min(ts, S)
    tn = _tile(NB, tn_cap)
    per = NB // tn
    ns = S // ts

    def body(a_ref, b_ref, o_ref):
        s = pl.program_id(1)

        @pl.when(s == 0)
        def _():
            o_ref[...] = jnp.zeros(o_ref.shape, F32)

        o_ref[...] += lax.dot_general(a_ref[...], b_ref[...], (((0,), (0,)), ((), ())),
                                      preferred_element_type=F32)

    return pl.pallas_call(
        body, name=name, grid=(N // tn, ns),
        in_specs=[pl.BlockSpec((ts, K), lambda n, s: (s, 0)), pl.BlockSpec((ts, tn), lambda n, s: (s, n))],
        out_specs=pl.BlockSpec((None, K, tn), lambda n, s: (n // per, 0, n % per)),
        out_shape=jax.ShapeDtypeStruct((blocks, K, NB), F32),
        compiler_params=_params(("arbitrary", "arbitrary")),
    )(a, b)


def _aligned(off):
    return off if isinstance(off, int) else pl.multiple_of(off, LANE)


def _attn_bias(s, mode, extra_ref, i, tq, kc, dil, L):
    if mode == "table":
        return s + extra_ref[...]
    if mode == "dil":
        r = lax.broadcasted_iota(jnp.int32, (tq, kc), 0)
        tk = lax.broadcasted_iota(jnp.int32, (tq, kc), 1) - DIL_REACH
        dist = jnp.abs(r - tk)
        kin = (i * tq) % L + tk
        valid = (dist <= DIL_REACH) & (kin >= 0) & (kin < L)
        s = s - extra_ref[0:1, :] * (dil * dist).astype(F32)
        return jnp.where(valid, s, NEG)
    return s


def _attn_fwd(name, q, qo, k, ko, v, vo, *, H, G, S, tq, kw, kc, kstart, mode=None, extra=None,
              var=None, dil=1, L=1, out_dtype=BF16, prescaled=False, carry=None):
    nq = S // tq
    nch = kw // kc
    Sk = k.shape[0]
    scale = HEAD ** -0.5
    has_extra = mode is not None
    exp = jnp.exp2 if prescaled else jnp.exp
    log = jnp.log2 if prescaled else jnp.log

    def body(*refs):
        if has_extra:
            q_ref, k_ref, v_ref, e_ref, o_ref, l_ref = refs
        else:
            q_ref, k_ref, v_ref, o_ref, l_ref = refs
            e_ref = None
        i = pl.program_id(1)
        qb = q_ref[...]
        start = kstart(i)

        def chunk(j, carry):
            m, l, acc = carry
            off = _aligned(start + j * kc)
            kb = k_ref[pl.ds(off, kc), :]
            vb = v_ref[pl.ds(off, kc), :]
            s = lax.dot_general(qb, kb, (((1,), (1,)), ((), ())), preferred_element_type=F32)
            if not prescaled:
                s = s * scale
            s = _attn_bias(s, mode, e_ref, i, tq, kc, dil, L)
            mn = jnp.maximum(m, jnp.max(s, axis=1, keepdims=True))
            p = exp(s - mn)
            a = exp(m - mn)
            l = a * l + jnp.sum(p, axis=1, keepdims=True)
            acc = a * acc + jnp.dot(p.astype(BF16), vb, preferred_element_type=F32)
            return mn, l, acc

        init = (jnp.full((tq, 1), -3.0e38, F32), jnp.zeros((tq, 1), F32), jnp.zeros((tq, HEAD), F32))
        if nch == 1:
            m, l, acc = chunk(0, init)
        else:
            m, l, acc = lax.fori_loop(0, nch, chunk, init)
        o_ref[...] = (acc / l).astype(o_ref.dtype)
        l_ref[...] = jnp.broadcast_to(m + log(l), (tq, HEAD))

    in_specs = [
        pl.BlockSpec((tq, HEAD), lambda h, i: (i, qo + h)),
        pl.BlockSpec((Sk, HEAD), lambda h, i: (0, ko + h // G)),
        pl.BlockSpec((Sk, HEAD), lambda h, i: (0, vo + h // G)),
    ]
    args = [q, k, v]
    if mode == "table":
        in_specs.append(pl.BlockSpec((None, None, tq, kw), lambda h, i: (var(i), h, 0, 0)))
        args.append(extra)
    elif mode == "dil":
        in_specs.append(pl.BlockSpec((None, 8, kw), lambda h, i: (h, 0, 0)))
        args.append(extra)
    grid = (H, nq)
    body, c_in, c_in_specs, c_out, c_out_specs, c_scratch = _carried(body, len(args), 2, grid, carry)
    return pl.pallas_call(
        body, name=name, grid=grid, in_specs=in_specs + c_in_specs,
        out_specs=[pl.BlockSpec((tq, HEAD), lambda h, i: (i, h)),
                   pl.BlockSpec((tq, HEAD), lambda h, i: (i, h))] + c_out_specs,
        out_shape=[jax.ShapeDtypeStruct((S, H * HEAD), out_dtype),
                   jax.ShapeDtypeStruct((S, H * HEAD), F32)] + c_out,
        scratch_shapes=c_scratch,
        compiler_params=_params(("arbitrary", "arbitrary")),
    )(*args, *c_in)


def _attn_bwd(name, q, qo, k, ko, v, vo, o, do, lse, *, H, G, S, tq, kw, kc, kstart, mode=None, extra=None,
              var=None, dil=1, L=1, prescaled=False, carry=None):
    nq = S // tq
    nch = kw // kc
    Sk = k.shape[0]
    HK = H // G
    scale = HEAD ** -0.5
    dk_mult = float(np.log(2.0)) if prescaled else scale
    exp = jnp.exp2 if prescaled else jnp.exp
    has_extra = mode is not None
    table = mode == "table"
    if table:
        assert nq >= NA_VARIANTS and nch == 1

    def body(*refs):
        refs = list(refs)
        q_ref, k_ref, v_ref, o_ref, do_ref, l_ref = refs[:6]
        e_ref = refs[6] if has_extra else None
        outs = refs[7:] if has_extra else refs[6:]
        dq_ref, dk_ref, dv_ref = outs[:3]
        h = pl.program_id(0)
        i = pl.program_id(1)

        @pl.when(jnp.logical_and(h % G == 0, i == 0))
        def _():
            dk_ref[...] = jnp.zeros(dk_ref.shape, F32)
            dv_ref[...] = jnp.zeros(dv_ref.shape, F32)

        if table:
            db_ref = outs[3]

            @pl.when((i == 0) | (i == 1) | (i == nq - 1))
            def _():
                db_ref[...] = jnp.zeros(db_ref.shape, F32)

        qb = q_ref[...]
        dob = do_ref[...]
        delta = jnp.sum(o_ref[...].astype(F32) * dob.astype(F32), axis=1, keepdims=True)
        lse_b = l_ref[:, 0:1]
        start = kstart(i)

        def chunk(j, dq):
            off = _aligned(start + j * kc)
            kb = k_ref[pl.ds(off, kc), :]
            vb = v_ref[pl.ds(off, kc), :]
            s = lax.dot_general(qb, kb, (((1,), (1,)), ((), ())), preferred_element_type=F32)
            if not prescaled:
                s = s * scale
            s = _attn_bias(s, mode, e_ref, i, tq, kc, dil, L)
            p = exp(s - lse_b)
            dp = lax.dot_general(dob, vb, (((1,), (1,)), ((), ())), preferred_element_type=F32)
            ds = p * (dp - delta)
            dsb = ds.astype(BF16)
            dk_ref[pl.ds(off, kc), :] += dk_mult * lax.dot_general(
                dsb, qb, (((0,), (0,)), ((), ())), preferred_element_type=F32)
            dv_ref[pl.ds(off, kc), :] += lax.dot_general(
                p.astype(BF16), dob, (((0,), (0,)), ((), ())), preferred_element_type=F32)
            if table:
                db_ref[...] += ds
            return dq + jnp.dot(dsb, kb, preferred_element_type=F32)

        dq0 = jnp.zeros((tq, HEAD), F32)
        dq = chunk(0, dq0) if nch == 1 else lax.fori_loop(0, nch, chunk, dq0)
        dq_ref[...] = (dq * scale).astype(dq_ref.dtype)

    in_specs = [
        pl.BlockSpec((tq, HEAD), lambda h, i: (i, qo + h)),
        pl.BlockSpec((Sk, HEAD), lambda h, i: (0, ko + h // G)),
        pl.BlockSpec((Sk, HEAD), lambda h, i: (0, vo + h // G)),
        pl.BlockSpec((tq, HEAD), lambda h, i: (i, h)),
        pl.BlockSpec((tq, HEAD), lambda h, i: (i, h)),
        pl.BlockSpec((tq, HEAD), lambda h, i: (i, h)),
    ]
    args = [q, k, v, o, do, lse]
    out_specs = [
        pl.BlockSpec((tq, HEAD), lambda h, i: (i, h)),
        pl.BlockSpec((Sk, HEAD), lambda h, i: (0, h // G)),
        pl.BlockSpec((Sk, HEAD), lambda h, i: (0, h // G)),
    ]
    out_shape = [
        jax.ShapeDtypeStruct((S, H * HEAD), BF16),
        jax.ShapeDtypeStruct((Sk, HK * HEAD), F32),
        jax.ShapeDtypeStruct((Sk, HK * HEAD), F32),
    ]
    if table:
        in_specs.append(pl.BlockSpec((None, None, tq, kw), lambda h, i: (var(i), h, 0, 0)))
        args.append(extra)
        out_specs.append(pl.BlockSpec((None, None, tq, kw), lambda h, i: (var(i), h, 0, 0)))
        out_shape.append(jax.ShapeDtypeStruct(extra.shape, F32))
    elif mode == "dil":
        in_specs.append(pl.BlockSpec((None, 8, kw), lambda h, i: (h, 0, 0)))
        args.append(extra)
    grid = (H, nq)
    body, c_in, c_in_specs, c_out, c_out_specs, c_scratch = _carried(body, len(args), len(out_shape), grid, carry)
    return pl.pallas_call(
        body, name=name, grid=grid, in_specs=in_specs + c_in_specs, out_specs=out_specs + c_out_specs,
        out_shape=out_shape + c_out, scratch_shapes=c_scratch,
        compiler_params=_params(("arbitrary", "arbitrary")),
    )(*args, *c_in)


def _na_constants():
    pairs = NA_VARIANTS * NA_QR * NA_KR
    sel = np.zeros((pairs, 16), np.float32)
    row_ok = np.zeros((pairs, 1), np.float32)
    for var in range(NA_VARIANTS):
        for qr in range(NA_QR):
            for kr in range(NA_KR):
                p = (var * NA_QR + qr) * NA_KR + kr
                first_key = (0, qr, NA_QR)[var]
                dr = kr - qr + (NA_WIN_R - 1) - (0, NA_QR, 2 * NA_QR)[var]
                if first_key <= kr < first_key + NA_WIN_R:
                    assert 0 <= dr < 2 * NA_WIN_R - 1
                    sel[p, dr] = 1.0
                    row_ok[p, 0] = 1.0
    j = np.arange(GRID_W)[:, None]
    c = np.arange(GRID_W)[None, :]
    cs = np.clip(j - NA_WIN_C // 2, 0, GRID_W - NA_WIN_C)
    col_ok = ((c >= cs) & (c < cs + NA_WIN_C)).astype(np.float32).reshape(1, GRID_W * GRID_W)
    dc = np.clip(c - j + NA_WIN_C - 1, 0, 2 * NA_WIN_C - 2).reshape(-1)
    toe = np.zeros((32, GRID_W * GRID_W), np.float32)
    toe[dc, np.arange(GRID_W * GRID_W)] = 1.0
    return sel, row_ok, col_ok, toe


def _split3(x):
    hi = x.astype(BF16)
    r1 = x - hi.astype(F32)
    mid = r1.astype(BF16)
    lo = (r1 - mid.astype(F32)).astype(BF16)
    return hi, mid, lo


def _dot_sel_left(onehot_bf16, x):
    return sum(jnp.dot(onehot_bf16, t, preferred_element_type=F32) for t in _split3(x))


def _dot_sel_right(x, onehot_bf16):
    return sum(jnp.dot(t, onehot_bf16, preferred_element_type=F32) for t in _split3(x))


def _na_bias_build(rpb_l):
    sel, row_ok, col_ok, toe = _na_constants()
    pairs = sel.shape[0]
    ww = GRID_W * GRID_W
    t = jnp.pad(rpb_l, ((0, 0), (0, 1), (0, 1)))

    def body(t_ref, sel_ref, toe_ref, rok_ref, cok_ref, o_ref):
        rows = _dot_sel_right(t_ref[...], toe_ref[...])
        val = _dot_sel_left(sel_ref[...], rows)
        ok = rok_ref[...] * cok_ref[...]
        o_ref[...] = jnp.where(ok > 0.5, val, NEG)

    full = lambda shape: pl.BlockSpec(shape, lambda h: (0,) * len(shape))
    out = pl.pallas_call(
        body, name="na_bias_build", grid=(NA_HEADS,),
        in_specs=[pl.BlockSpec((None, 16, 32), lambda h: (h, 0, 0)), full((pairs, 16)), full((32, ww)),
                  full((pairs, 1)), full((1, ww))],
        out_specs=pl.BlockSpec((None, pairs, ww), lambda h: (h, 0, 0)),
        out_shape=jax.ShapeDtypeStruct((NA_HEADS, pairs, ww), F32),
        compiler_params=_params(("arbitrary",)),
    )(t, jnp.asarray(sel, BF16), jnp.asarray(toe, BF16), jnp.asarray(row_ok), jnp.asarray(col_ok))
    out = out.reshape(NA_HEADS, NA_VARIANTS, NA_QR, NA_KR, GRID_W, GRID_W)
    out = out.transpose(1, 0, 2, 4, 3, 5)
    return out.reshape(NA_VARIANTS, NA_HEADS, NA_QR * GRID_W, NA_KR * GRID_W)


def _na_bias_grad(dbias):
    sel, _, _, toe = _na_constants()
    pairs = sel.shape[0]
    ww = GRID_W * GRID_W
    d = dbias.reshape(NA_VARIANTS, NA_HEADS, NA_QR, GRID_W, NA_KR, GRID_W).transpose(1, 0, 2, 4, 3, 5)
    d = d.reshape(NA_HEADS, pairs, ww)

    def body(d_ref, selt_ref, toet_ref, o_ref):
        rows = _dot_sel_left(selt_ref[...], d_ref[...])
        o_ref[...] = _dot_sel_right(rows, toet_ref[...])

    full = lambda shape: pl.BlockSpec(shape, lambda h: (0,) * len(shape))
    out = pl.pallas_call(
        body, name="na_bias_grad", grid=(NA_HEADS,),
        in_specs=[pl.BlockSpec((None, pairs, ww), lambda h: (h, 0, 0)), full((16, pairs)), full((ww, 32))],
        out_specs=pl.BlockSpec((None, 16, 32), lambda h: (h, 0, 0)),
        out_shape=jax.ShapeDtypeStruct((NA_HEADS, 16, 32), F32),
        compiler_params=_params(("arbitrary",)),
    )(d, jnp.asarray(sel.T, BF16), jnp.asarray(toe.T, BF16))
    return out[:, :2 * NA_WIN_R - 1, :2 * NA_WIN_C - 1]


def _sigmoid(x):
    return 1.0 / (1.0 + jnp.exp(-x))


def _swap32(y):
    lane = lax.broadcasted_iota(jnp.int32, y.shape, 1)
    up = pltpu.roll(y, HEAD - 32, 1)
    down = pltpu.roll(y, 32, 1)
    return jnp.where((lane // 32) % 2 == 0, up, down)


def _rope_tables(S):
    t = jnp.arange(S, dtype=jnp.int32)
    freqs = ROPE_BASE ** (-jnp.arange(32, dtype=F32) / 32)
    ar = (t // GRID_W).astype(F32)[:, None] * freqs[None, :]
    ac = (t % GRID_W).astype(F32)[:, None] * freqs[None, :]
    cos = jnp.concatenate([jnp.cos(ar), jnp.cos(ar), jnp.cos(ac), jnp.cos(ac)], axis=1)
    sin = jnp.concatenate([-jnp.sin(ar), jnp.sin(ar), -jnp.sin(ac), jnp.sin(ac)], axis=1)
    return cos, sin


def _rms_stats(x):
    x = x.astype(F32)
    r = lax.rsqrt(jnp.mean(x * x, axis=-1, keepdims=True) + EPS)
    return x * r, r


def _rms_bwd(xhat, r, dy, g):
    dxh = dy * g
    return r * (dxh - xhat * jnp.mean(dxh * xhat, axis=-1, keepdims=True))


def _qk_fwd(x, g, cos, sin):
    xhat, _ = _rms_stats(x)
    y = xhat * g
    return y * cos + _swap32(y) * sin


def _qk_bwd(dyr, x, g, cos, sin):
    dyr = dyr.astype(F32)
    dy = dyr * cos + _swap32(dyr * sin)
    xhat, r = _rms_stats(x)
    return _rms_bwd(xhat, r, dy, g), dy * xhat


def _perm(a, dil):
    if dil == 1:
        return a
    S = a.shape[0]
    return a.reshape(S // dil, dil, *a.shape[1:]).swapaxes(0, 1).reshape(a.shape)


def _unperm(a, dil):
    if dil == 1:
        return a
    S = a.shape[0]
    return a.reshape(dil, S // dil, *a.shape[1:]).swapaxes(0, 1).reshape(a.shape)


def _place():
    x, y, c = lax.axis_index("x"), lax.axis_index("y"), lax.axis_index("c")
    chips = [(1 - x, y), (x, 1 - y), (1 - x, 1 - y)]
    return x, y, c, chips


def _remote(src, dst, send, recv, dev):
    return pltpu.make_async_remote_copy(src_ref=src, dst_ref=dst, send_sem=send, recv_sem=recv,
                                        device_id=dev, device_id_type=MESH)


ANY = pl.BlockSpec(memory_space=pl.ANY)


class _Xfer:
    def __init__(self, send=None, recv=None, local=None):
        self.send, self.recv, self.local = send, recv, local

    def start(self):
        if self.local is not None:
            self.local().start()
        if self.send is not None:
            self.send().start()

    def wait(self):
        if self.recv is not None:
            self.recv().wait_recv()
        if self.send is not None:
            self.send().wait_send()
        if self.local is not None:
            self.local().wait()


def _sem_shapes(nsem):
    return [pltpu.SemaphoreType.DMA((nsem,))] * 3


def _carried(body, n_in, n_out, grid, carry):
    if carry is None:
        return body, [], [], [], [], []
    ins, out_shapes, plan, nsem = carry
    ci, co = len(ins), len(out_shapes)

    def new_body(*refs):
        a, cin = refs[:n_in], refs[n_in:n_in + ci]
        b = refs[n_in + ci:n_in + ci + n_out]
        cout = refs[n_in + ci + n_out:n_in + ci + n_out + co]
        rest = refs[n_in + ci + n_out + co:]
        scratch, sems = rest[:-3], rest[-3:]
        ids = [pl.program_id(d) for d in range(len(grid))]
        first = functools.reduce(jnp.logical_and, [i == 0 for i in ids])
        last = functools.reduce(jnp.logical_and, [i == g - 1 for i, g in zip(ids, grid)])

        @pl.when(first)
        def _():
            for t in plan(cin, cout, *sems):
                t.start()

        body(*a, *b, *scratch)

        @pl.when(last)
        def _():
            for t in plan(cin, cout, *sems):
                t.wait()

    return new_body, list(ins), [ANY] * ci, list(out_shapes), [ANY] * co, _sem_shapes(nsem)


def _exchange(name, carry):
    ins, out_shapes, plan, nsem = carry
    n = len(ins)

    def body(*refs):
        items = plan(refs[:n], refs[n:n + len(out_shapes)], *refs[-3:])
        for t in items:
            t.start()
        for t in items:
            t.wait()

    return pl.pallas_call(body, name=name, in_specs=[ANY] * n, out_specs=[ANY] * len(out_shapes),
                          out_shape=list(out_shapes), scratch_shapes=_sem_shapes(nsem))(*ins)


def _plan_gather(ins, outs, send, recv, lsem):
    x, y, c, chips = _place()
    me = 2 * x + y
    items = []
    for w, (src, dst) in enumerate(zip(ins, outs)):
        items.append(_Xfer(local=functools.partial(pltpu.make_async_copy, src, dst.at[me], lsem.at[w])))
        for r, (px, py) in enumerate(chips):
            k = 3 * w + r
            to = functools.partial(_remote, src, dst.at[me], send.at[k], recv.at[k], (px, py, c))
            frm = functools.partial(_remote, src, dst.at[2 * px + py], send.at[k], recv.at[k], (px, py, c))
            items.append(_Xfer(send=to, recv=frm))
    return items


def _carry_gather(shards):
    shapes = [jax.ShapeDtypeStruct((N_CHIPS,) + a.shape, a.dtype) for a in shards]
    return (shards, shapes, _plan_gather, 3 * len(shards))


def _plan_pair_exchange(ins, outs, send, recv, lsem):
    x, y, c, _ = _place()
    n = len(ins)
    items = []
    for w, src in enumerate(ins):
        hr = src.shape[1] // 2
        mine = pl.ds(pl.multiple_of(c * hr, 8), hr)
        theirs = pl.ds(pl.multiple_of((1 - c) * hr, 8), hr)
        cp = functools.partial(_remote, src.at[:, theirs, :], outs[w], send.at[w], recv.at[w], (x, y, 1 - c))
        keep = functools.partial(pltpu.make_async_copy, src.at[:, mine, :], outs[n + w], lsem.at[w])
        items.append(_Xfer(send=cp, recv=cp, local=keep))
    return items


def _carry_pair_exchange(gs):
    shapes = [jax.ShapeDtypeStruct((a.shape[0], a.shape[1] // 2, a.shape[2]), a.dtype) for a in gs]
    return (gs, shapes + shapes, _plan_pair_exchange, len(gs))


def _plan_chip_exchange(ins, outs, send, recv, lsem):
    x, y, c, chips = _place()
    n = len(ins)
    items = []
    for w, src in enumerate(ins):
        keep = functools.partial(pltpu.make_async_copy, src.at[2 * x + y], outs[n + w], lsem.at[3 * w])
        items.append(_Xfer(local=keep))
        for r, (px, py) in enumerate(chips):
            k = 3 * w + r
            cp = functools.partial(_remote, src.at[2 * px + py], outs[w].at[r], send.at[k], recv.at[k], (px, py, c))
            items.append(_Xfer(send=cp, recv=cp))
    return items


def _carry_chip_exchange(qs):
    shapes = [jax.ShapeDtypeStruct((3,) + a.shape[1:], a.dtype) for a in qs]
    own = [jax.ShapeDtypeStruct(a.shape[1:], a.dtype) for a in qs]
    return (qs, shapes + own, _plan_chip_exchange, 3 * len(qs))


def _plan_pair_complete(ins, outs, send, recv, lsem):
    x, y, c, _ = _place()
    sib = (x, y, 1 - c)
    items = []
    for w, (src, dst) in enumerate(zip(ins, outs)):
        hr = src.shape[0]
        mine = pl.ds(pl.multiple_of(c * hr, 8), hr)
        other = pl.ds(pl.multiple_of((1 - c) * hr, 8), hr)
        items.append(_Xfer(send=functools.partial(_remote, src, dst.at[mine, :], send.at[w], recv.at[w], sib),
                           recv=functools.partial(_remote, src, dst.at[other, :], send.at[w], recv.at[w], sib),
                           local=functools.partial(pltpu.make_async_copy, src, dst.at[mine, :], lsem.at[w])))
    return items


def _carry_pair_complete(rs):
    shapes = [jax.ShapeDtypeStruct((2 * a.shape[0], a.shape[1]), a.dtype) for a in rs]
    return (rs, shapes, _plan_pair_complete, len(rs))


def _allgather_weights(ws):
    n = len(ws)

    def body(*refs):
        ins, outs = refs[:n], refs[n:2 * n]
        send, recv, lsem = refs[2 * n:]
        x, y, c, chips = _place()
        me = 2 * x + y
        sib = (x, y, 1 - c)
        sends = []
        local = []
        for w in range(n):
            hr = ins[w].shape[1] // 2
            mine = pl.ds(pl.multiple_of(c * hr, 16), hr)
            cp = pltpu.make_async_copy(ins[w], outs[w].at[:, me], lsem.at[w])
            cp.start()
            local.append(cp)
            for r, (px, py) in enumerate(chips):
                cp = _remote(ins[w].at[:, mine, :], outs[w].at[:, me, mine, :], send.at[w, r], recv.at[w, r],
                             (px, py, c))
                cp.start()
                sends.append(cp)
        for w in range(n):
            hr = ins[w].shape[1] // 2
            mine = pl.ds(pl.multiple_of(c * hr, 16), hr)
            for r, (px, py) in enumerate(chips):
                blk = outs[w].at[:, 2 * px + py, mine, :]
                _remote(blk, blk, send.at[w, r], recv.at[w, r], (px, py, c)).wait_recv()
                cp = _remote(blk, blk, send.at[w, 3 + r], recv.at[w, 3 + r], sib)
                cp.start()
                sends.append(cp)
        for w in range(n):
            hr = ins[w].shape[1] // 2
            other = pl.ds(pl.multiple_of((1 - c) * hr, 16), hr)
            for r, (px, py) in enumerate(chips):
                blk = outs[w].at[:, 2 * px + py, other, :]
                _remote(blk, blk, send.at[w, 3 + r], recv.at[w, 3 + r], sib).wait_recv()
        for cp in sends:
            cp.wait_send()
        for cp in local:
            cp.wait()

    out_shape = [jax.ShapeDtypeStruct((a.shape[0], N_CHIPS) + a.shape[1:], a.dtype) for a in ws]
    return pl.pallas_call(
        body, name="allgather_weights", in_specs=[ANY] * n, out_specs=[ANY] * n, out_shape=out_shape,
        scratch_shapes=[pltpu.SemaphoreType.DMA((n, 6)), pltpu.SemaphoreType.DMA((n, 6)),
                        pltpu.SemaphoreType.DMA((n,))],
    )(*ws)


def _allreduce_small(vec, loss_rows, loss_scale):
    rows = vec.shape[0]

    def body(v_ref, tot_ref, loss_ref, gat_ref, send, recv):
        x, y, c, _ = _place()
        me = 4 * x + 2 * y + c
        gat_ref[me] = v_ref[...]
        peers = []
        for r in range(1, N_DEV):
            px = 1 - x if r & 4 else x
            py = 1 - y if r & 2 else y
            pc = 1 - c if r & 1 else c
            peers.append((px, py, pc))
            _remote(v_ref, gat_ref.at[me], send.at[r - 1], recv.at[r - 1], (px, py, pc)).start()
        for r, (px, py, pc) in enumerate(peers):
            _remote(v_ref, gat_ref.at[4 * px + 2 * py + pc], send.at[r], recv.at[r], (px, py, pc)).wait_recv()
        for r, (px, py, pc) in enumerate(peers):
            _remote(v_ref, gat_ref.at[me], send.at[r], recv.at[r], (px, py, pc)).wait_send()
        tot = gat_ref[0]
        for d in range(1, N_DEV):
            tot = tot + gat_ref[d]
        tot_ref[...] = tot
        loss_ref[...] = jnp.full(loss_ref.shape, loss_scale, F32) * jnp.sum(tot[:loss_rows])

    vm = pl.BlockSpec(memory_space=pltpu.VMEM)
    tot, loss = pl.pallas_call(
        body, name="allreduce_small", in_specs=[vm], out_specs=[vm, vm],
        out_shape=[jax.ShapeDtypeStruct((rows, LANE), F32), jax.ShapeDtypeStruct((8, LANE), F32)],
        scratch_shapes=[pltpu.VMEM((N_DEV, rows, LANE), F32), pltpu.SemaphoreType.DMA((N_DEV - 1,)),
                        pltpu.SemaphoreType.DMA((N_DEV - 1,))],
        compiler_params=_params(),
    )(vec)
    return tot, loss


def _add_rows(name, terms):
    shape = terms[0].shape
    C = shape[-1]
    flat = [t.reshape(-1, C) for t in terms]
    rows = flat[0].shape[0]
    width = _tile(C, 1024)

    def fn(*vals):
        tot = vals[0]
        for val in vals[1:]:
            tot = tot + val
        return tot

    (out,) = _ew(name, fn, [(t, "t", 0) for t in flat], [(C, F32)], rows=rows, width=width, ncols=C // width,
                 ts=512)
    return out.reshape(shape)


def _adamw(name, w, g, m, v):
    shape = w.shape
    C = shape[-1]
    flat = [t.reshape(-1, C) for t in (w, g, m, v)]
    rows = flat[0].shape[0]
    width = _tile(C, 1024)

    def fn(w, g, m, v):
        m2 = ADAM_B1 * m + (1.0 - ADAM_B1) * g
        v2 = ADAM_B2 * v + (1.0 - ADAM_B2) * (g * g)
        m_hat = m2 / (1.0 - ADAM_B1 ** ADAM_STEP)
        v_hat = v2 / (1.0 - ADAM_B2 ** ADAM_STEP)
        delta = -ADAM_LR * (m_hat / (jnp.sqrt(v_hat) + ADAM_EPS) + ADAM_WD * w)
        return delta, m2, v2

    outs = _ew(name, fn, [(t, "t", 0) for t in flat], [(C, F32)] * 3, rows=rows, width=width,
               ncols=C // width, ts=min(512, rows))
    return [o.reshape(shape) for o in outs]


def _na_geometry(S):
    rows = S // GRID_W
    nq = rows // NA_QR
    tq = NA_QR * GRID_W
    kw = NA_KR * GRID_W

    def kstart(i):
        return jnp.clip(NA_QR * i - NA_WIN_R // 2, 0, rows - NA_KR) * GRID_W

    def var(i):
        return jnp.where(i == 0, 0, jnp.where(i == nq - 1, 2, 1))

    return dict(S=S, tq=tq, kw=kw, kc=kw, kstart=kstart, var=var, mode="table")


def _dil_geometry(S, dil):
    L = S // dil
    tq = min(512, L)
    kw = tq + 2 * DIL_REACH
    return dict(S=S, tq=tq, kw=kw, kc=kw, kstart=lambda i: i * tq, mode="dil", dil=dil, L=L)


def _dil_slopes(g, kw):
    idx = jnp.arange(1, DIL_HEADS + 1, dtype=F32)[g * DIL_HPG:(g + 1) * DIL_HPG]
    slopes = 2.0 ** (-ALIBI_MAX_EXP * idx / DIL_HEADS)
    return jnp.broadcast_to(slopes[:, None, None], (DIL_HPG, 8, kw))


def _layer_fwd(x, p, cos, sin, next_shards=None):
    S, D = x.shape
    ts = 256
    gp = p["pre_g"][None, :]
    (h,) = _ew("pre_norm", lambda x, g: _rms_stats(x)[0] * g, [(x, "t", 0), (gp, "v", 0)], [(D, BF16)],
               rows=S, width=D, ncols=1, ts=ts)
    proj = _mm_nn("proj_in", h, p["w_in"], out_dtype=BF16)

    bias = _na_bias_build(p["rpb"])
    ya, lse_a = _attn_fwd("na_fwd", proj, C_QA, proj, C_KA, proj, C_VA, H=NA_HEADS, G=1, extra=bias,
                          **_na_geometry(S))

    def qk(name, off, nh, g, mult):
        fn = lambda x, g, cos, sin: _qk_fwd(x, g, cos, sin) * mult
        (out,) = _ew(name, fn, [(proj, "t", off), (g[None, :], "vf", 0), (cos, "tf", 0), (sin, "tf", 0)],
                     [(nh * HEAD, BF16)], rows=S, width=HEAD, ncols=nh, ts=512)
        return out

    qb = qk("q_norm_rope", C_QB, GQ_HEADS, p["q_g"], HEAD ** -0.5 * float(np.log2(np.e)))
    kb = qk("k_norm_rope", C_KB, GKV_HEADS, p["k_g"], 1.0)
    geo_b = dict(S=S, tq=min(GQA_FWD_TILE[0], S), kw=S, kc=min(GQA_FWD_TILE[1], S), kstart=lambda i: 0,
                 prescaled=True)
    carry = None if next_shards is None else _carry_gather(next_shards)
    yb, lse_b, *next_full = _attn_fwd("gqa_fwd", qb, 0, kb, 0, proj, C_VB, H=GQ_HEADS, G=GQ_HEADS // GKV_HEADS,
                                      carry=carry, **geo_b)

    W = DIL_HPG * HEAD
    og, lg, saved_c = [], [], []
    for g, (win, dil) in enumerate(DIL_GROUPS):
        assert (win // 2) // dil == DIL_REACH
        sl = lambda off: proj[:, (off + g * DIL_HPG) * HEAD:(off + (g + 1) * DIL_HPG) * HEAD]
        qg = _perm(sl(C_QC), dil)
        kg = jnp.pad(_perm(sl(C_KC), dil), ((DIL_REACH, DIL_REACH), (0, 0)))
        vg = jnp.pad(_perm(sl(C_VC), dil), ((DIL_REACH, DIL_REACH), (0, 0)))
        geo = _dil_geometry(S, dil)
        o, l = _attn_fwd(f"dil{g}_fwd", qg, 0, kg, 0, vg, 0, H=DIL_HPG, G=1, extra=_dil_slopes(g, geo["kw"]),
                         out_dtype=F32, **geo)
        og.append(_unperm(o, dil))
        lg.append(_unperm(l, dil))
        saved_c.append((qg, kg, vg))

    def combine(o1, o2, o3, l1, l2, l3):
        m = jnp.maximum(jnp.maximum(l1, l2), l3)
        e1, e2, e3 = jnp.exp(l1 - m), jnp.exp(l2 - m), jnp.exp(l3 - m)
        tot = e1 + e2 + e3
        return (e1 * o1 + e2 * o2 + e3 * o3) / tot, m + jnp.log(tot)

    yc, lse_c = _ew("dil_combine", combine, [(a, "t", 0) for a in og + lg], [(W, BF16), (W, F32)],
                    rows=S, width=W, ncols=1, ts=512)

    def gate(y, z):
        z = z.astype(F32)
        return y.astype(F32) * (z * _sigmoid(z))

    def branch(name, y, zoff, width, w):
        unit = width // HEAD
        (u,) = _ew(name + "_silu", gate, [(y, "t", 0), (proj, "tf", zoff // unit)], [(width, BF16)],
                   rows=S, width=width, ncols=1, ts=512)
        return u, _mm_nn(name + "_proj", u, w, out_dtype=BF16)

    ua, ta = branch("branch_a", ya, C_ZA, NA_HEADS * HEAD, p["w_a"])
    ub, tb = branch("branch_b", yb, C_ZB, GQ_HEADS * HEAD, p["w_b"])
    uc, tc = branch("branch_c", yc, C_ZC, W, p["w_c"])

    mw = 512
    nm = D // mw
    goff = C_GATE * HEAD // mw
    bg = p["b_gate"][None, :]

    def merge(ta, tb, tc, ga, gb, gc, ba, bb, bc):
        sa = _sigmoid(ga.astype(F32) + ba)
        sb = _sigmoid(gb.astype(F32) + bb)
        sc = _sigmoid(gc.astype(F32) + bc)
        return sa * ta.astype(F32) + sb * tb.astype(F32) + sc * tc.astype(F32)

    (merged,) = _ew("merge", merge,
                    [(ta, "t", 0), (tb, "t", 0), (tc, "t", 0),
                     (proj, "t", goff), (proj, "t", goff + nm), (proj, "t", goff + 2 * nm),
                     (bg, "v", 0), (bg, "v", nm), (bg, "v", 2 * nm)],
                    [(D, BF16)], rows=S, width=mw, ncols=nm, ts=512)
    out = _mm_nn("proj_out", merged, p["w_out"], out_dtype=F32)
    gq = p["post_g"][None, :]
    (x_next,) = _ew("post_norm_residual", lambda x, o, g: x + _rms_stats(o)[0] * g,
                    [(x, "t", 0), (out, "t", 0), (gq, "v", 0)], [(D, F32)], rows=S, width=D, ncols=1, ts=ts)
    saved = dict(x=x, h=h, proj=proj, bias=bias, ya=ya, lse_a=lse_a, qb=qb, kb=kb, yb=yb, lse_b=lse_b,
                 geo_b=geo_b, saved_c=saved_c, yc=yc, lse_c=lse_c, ua=ua, ub=ub, uc=uc, ta=ta, tb=tb, tc=tc,
                 merged=merged, out=out)
    return x_next, saved, next_full


def _pair_sums(got):
    n = len(got) // 2
    return [_add_rows("grad_pair_sum", [mine, theirs]) for theirs, mine in zip(got[:n], got[n:])]


def _chip_sums(got):
    n = len(got) // 2
    return [_add_rows("grad_chip_sum", [own, t[0], t[1], t[2]]) for t, own in zip(got[:n], got[n:])]


def _layer_bwd(gy, p, s, cos, sin, pending=None):
    S, D = gy.shape
    ts = 256
    proj = s["proj"]
    grads = {}

    def post_bwd(o, dy, g):
        xhat, r = _rms_stats(o)
        return _rms_bwd(xhat, r, dy, g), dy * xhat

    dout, dg_post = _ew("post_norm_bwd", post_bwd, [(s["out"], "t", 0), (gy, "t", 0), (p["post_g"][None, :], "v", 0)],
                        [(D, BF16)], accs=("n",), rows=S, width=D, ncols=1, ts=ts)
    grads["post_g"] = dg_post[0]
    dmerged = _mm_nt("proj_out_bwd", dout, p["w_out"], out_dtype=BF16)
    grads["w_out"] = _mm_tn("proj_out_wgrad", s["merged"], dout, blocks=1).reshape(N_CHIPS, D // N_CHIPS, D)

    mw = 512
    nm = D // mw
    goff = C_GATE * HEAD // mw
    bg = p["b_gate"][None, :]

    def merge_bwd(dm, ta, tb, tc, ga, gb, gc, ba, bb, bc):
        dm = dm.astype(F32)
        res_t, res_g = [], []
        for t, g, b in ((ta, ga, ba), (tb, gb, bb), (tc, gc, bc)):
            sg = _sigmoid(g.astype(F32) + b)
            res_t.append(dm * sg)
            res_g.append(dm * t.astype(F32) * sg * (1.0 - sg))
        return (*res_t, *res_g, *res_g)

    mres = _ew("merge_bwd", merge_bwd,
               [(dmerged, "t", 0), (s["ta"], "t", 0), (s["tb"], "t", 0), (s["tc"], "t", 0),
                (proj, "t", goff), (proj, "t", goff + nm), (proj, "t", goff + 2 * nm),
                (bg, "v", 0), (bg, "v", nm), (bg, "v", 2 * nm)],
               [(D, BF16)] * 6, accs=("n", "n", "n"), rows=S, width=mw, ncols=nm, ts=512,
               carry=None if pending is None else _carry_pair_exchange(pending))
    dta, dtb, dtc, dga, dgb, dgc = mres[:6]
    grads["b_gate"] = jnp.concatenate([a[0] for a in mres[6:9]])
    pair = None if pending is None else _pair_sums(mres[9:])

    def silu_bwd(du, y, z):
        du, y, z = du.astype(F32), y.astype(F32), z.astype(F32)
        sg = _sigmoid(z)
        return du * (z * sg), du * y * (sg * (1.0 + z * (1.0 - sg)))

    def branch_bwd(name, dt, u, y, zoff, width, w):
        du = _mm_nt(name + "_proj_bwd", dt, w, out_dtype=BF16)
        gw = _mm_tn(name + "_wgrad", u, dt, blocks=N_CHIPS)
        unit = width // HEAD
        dy, dz = _ew(name + "_silu_bwd", silu_bwd, [(du, "t", 0), (y, "t", 0), (proj, "tf", zoff // unit)],
                     [(width, BF16), (width, BF16)], rows=S, width=width, ncols=1, ts=512)
        return dy, dz, gw

    W = DIL_HPG * HEAD
    dya, dza, grads["w_a"] = branch_bwd("branch_a", dta, s["ua"], s["ya"], C_ZA, NA_HEADS * HEAD, p["w_a"])
    dyb, dzb, grads["w_b"] = branch_bwd("branch_b", dtb, s["ub"], s["yb"], C_ZB, GQ_HEADS * HEAD, p["w_b"])
    dyc, dzc, grads["w_c"] = branch_bwd("branch_c", dtc, s["uc"], s["yc"], C_ZC, W, p["w_c"])

    dqa, dka, dva, dbias = _attn_bwd("na_bwd", proj, C_QA, proj, C_KA, proj, C_VA, s["ya"], dya, s["lse_a"],
                                     H=NA_HEADS, G=1, extra=s["bias"], **_na_geometry(S))
    grads["rpb"] = _na_bias_grad(dbias)

    geo_b = dict(s["geo_b"])
    geo_b["tq"], geo_b["kc"] = min(GQA_BWD_TILE[0], S), min(GQA_BWD_TILE[1], S)
    dqr, dkr, dvb, *recv = _attn_bwd("gqa_bwd", s["qb"], 0, s["kb"], 0, proj, C_VB, s["yb"], dyb, s["lse_b"],
                                     H=GQ_HEADS, G=GQ_HEADS // GKV_HEADS,
                                     carry=None if pending is None else _carry_chip_exchange(pair), **geo_b)
    half = None if pending is None else _chip_sums(recv)

    def qk_bwd(name, dyr, off, nh, g):
        return _ew(name, _qk_bwd, [(dyr, "t", 0), (proj, "t", off), (g[None, :], "vf", 0), (cos, "tf", 0),
                                   (sin, "tf", 0)],
                   [(nh * HEAD, BF16)], accs=("f",), rows=S, width=HEAD, ncols=nh, ts=512)

    dqb, dgq = qk_bwd("q_norm_rope_bwd", dqr, C_QB, GQ_HEADS, p["q_g"])
    dkb, dgk = qk_bwd("k_norm_rope_bwd", dkr, C_KB, GKV_HEADS, p["k_g"])
    grads["q_g"] = dgq[0]
    grads["k_g"] = dgk[0]

    dqc, dkc, dvc = [], [], []
    for g, (win, dil) in enumerate(DIL_GROUPS):
        qg, kg, vg = s["saved_c"][g]
        geo = _dil_geometry(S, dil)
        dq, dk, dv = _attn_bwd(f"dil{g}_bwd", qg, 0, kg, 0, vg, 0, _perm(s["yc"], dil), _perm(dyc, dil),
                               _perm(s["lse_c"], dil), H=DIL_HPG, G=1, extra=_dil_slopes(g, geo["kw"]), **geo)
        dqc.append(_unperm(dq, dil))
        dkc.append(_unperm(dk[DIL_REACH:-DIL_REACH], dil).astype(BF16))
        dvc.append(_unperm(dv[DIL_REACH:-DIL_REACH], dil).astype(BF16))

    dproj = jnp.concatenate(
        [dqa, dka.astype(BF16), dva.astype(BF16), dqb, dkb, dvb.astype(BF16), *dqc, *dkc, *dvc,
         dza, dzb, dzc, dga, dgb, dgc], axis=1)
    if pending is None:
        dh, done = _mm_nt("proj_in_bwd", dproj, p["w_in_t"], out_dtype=F32, w_transposed=True, tk_cap=2560), None
    else:
        dh, *done = _mm_nt("proj_in_bwd", dproj, p["w_in_t"], out_dtype=F32, w_transposed=True, tk_cap=2560,
                           carry=_carry_pair_complete(half))
    grads["w_in"] = _mm_tn("proj_in_wgrad", s["h"], dproj, blocks=N_CHIPS, ts=1024)

    def pre_bwd(x, dh, g, gy):
        xhat, r = _rms_stats(x)
        return gy + _rms_bwd(xhat, r, dh, g), dh * xhat

    dx, dg_pre = _ew("pre_norm_bwd", pre_bwd, [(s["x"], "t", 0), (dh, "t", 0), (p["pre_g"][None, :], "v", 0),
                                                 (gy, "t", 0)],
                     [(D, F32)], accs=("n",), rows=S, width=D, ncols=1, ts=ts)
    grads["pre_g"] = dg_pre[0]
    return dx, grads, done


BIG = ("w_in", "w_a", "w_b", "w_c", "w_out")
SMALL = ("pre_g", "b_gate", "q_g", "k_g", "rpb", "post_g")


def _reduce_big(gs):
    pair = _pair_sums(_exchange("grad_pair_exchange", _carry_pair_exchange(gs)))
    half = _chip_sums(_exchange("grad_chip_exchange", _carry_chip_exchange(pair)))
    return _exchange("grad_pair_complete", _carry_pair_complete(half))


def _pack(parts, rows_mult=8):
    flat = jnp.concatenate([a.reshape(-1) for a in parts])
    n = flat.shape[0]
    unit = rows_mult * LANE
    padded = -(-n // unit) * unit
    return jnp.pad(flat, (0, padded - n)).reshape(-1, LANE)


def kernel(x, pre_norm_g, w_in, b_gate, q_norm_g, k_norm_g, rpb, w_branch_a, w_branch_b, w_branch_c, w_out, post_norm_g, loss_target, m_pre_norm_g, m_w_in, m_b_gate, m_q_norm_g, m_k_norm_g, m_rpb, m_w_branch_a, m_w_branch_b, m_w_branch_c, m_w_out, m_post_norm_g, v_pre_norm_g, v_w_in, v_b_gate, v_q_norm_g, v_k_norm_g, v_rpb, v_w_branch_a, v_w_branch_b, v_w_branch_c, v_w_out, v_post_norm_g):
    n_layers = w_in.shape[0]
    S, D = x.shape[1], x.shape[2]
    big_w = dict(w_in=w_in, w_a=w_branch_a, w_b=w_branch_b, w_c=w_branch_c, w_out=w_out)
    big_m = dict(w_in=m_w_in, w_a=m_w_branch_a, w_b=m_w_branch_b, w_c=m_w_branch_c, w_out=m_w_out)
    big_v = dict(w_in=v_w_in, w_a=v_w_branch_a, w_b=v_w_branch_b, w_c=v_w_branch_c, w_out=v_w_out)
    small_w = dict(pre_g=pre_norm_g, b_gate=b_gate, q_g=q_norm_g, k_g=k_norm_g, rpb=rpb, post_g=post_norm_g)
    small_m = dict(pre_g=m_pre_norm_g, b_gate=m_b_gate, q_g=m_q_norm_g, k_g=m_k_norm_g, rpb=m_rpb,
                   post_g=m_post_norm_g)
    small_v = dict(pre_g=v_pre_norm_g, b_gate=v_b_gate, q_g=v_q_norm_g, k_g=v_k_norm_g, rpb=v_rpb,
                   post_g=v_post_norm_g)

    shards = [[big_w[n][l].astype(BF16) for n in BIG] for l in range(n_layers)]
    cos, sin = _rope_tables(S)

    def layer_params(l, full):
        p = dict(zip(BIG, full))
        p["w_in_t"] = jnp.swapaxes(p["w_in"], 1, 2).reshape(-1, D)
        p["w_out"] = p["w_out"].reshape(D, D)
        p.update(pre_g=pre_norm_g[l], b_gate=b_gate[l], q_g=q_norm_g[l], k_g=k_norm_g[l], rpb=rpb[l],
                 post_g=post_norm_g[l])
        return p

    full = [a[0] for a in _allgather_weights([a[None] for a in shards[0]])]
    act = x[0]
    saved, params = [], []
    for l in range(n_layers):
        params.append(layer_params(l, full))
        act, s, full = _layer_fwd(act, params[l], cos, sin, shards[l + 1] if l + 1 < n_layers else None)
        saved.append(s)

    dy, loss_cols = _ew("loss", lambda y, t: ((y - t) * (1.0 / D), (y - t) * (y - t)),
                        [(act, "t", 0), (loss_target[0], "t", 0)], [(D, F32)], accs=("n",),
                        rows=S, width=D, ncols=1, ts=256)

    big_g = {n: [None] * n_layers for n in BIG}
    small_g = {n: [None] * n_layers for n in SMALL}
    pending = None
    for l in reversed(range(n_layers)):
        dy, grads, done = _layer_bwd(dy, params[l], saved[l], cos, sin, pending)
        if done is not None:
            for n, r in zip(BIG, done):
                big_g[n][l + 1] = r
        pending = [grads[n] for n in BIG]
        for n in SMALL:
            small_g[n][l] = grads[n]
    for n, r in zip(BIG, _reduce_big(pending)):
        big_g[n][0] = r

    loss_rows = -(-D // (8 * LANE)) * 8
    parts = [_pack([loss_cols[0]])] + [_pack([jnp.stack(small_g[n]) for n in SMALL])]
    tot, loss = _allreduce_small(jnp.concatenate(parts), loss_rows, 0.5 / D)
    wvec = jnp.concatenate([jnp.zeros((loss_rows, LANE), F32), _pack([small_w[n] for n in SMALL])])
    mvec = jnp.concatenate([jnp.zeros((loss_rows, LANE), F32), _pack([small_m[n] for n in SMALL])])
    vvec = jnp.concatenate([jnp.ones((loss_rows, LANE), F32), _pack([small_v[n] for n in SMALL])])
    small_out = [tot] + _adamw("adamw_small", wvec, tot, mvec, vvec)

    def unpack(vec):
        flat = vec[loss_rows:].reshape(-1)
        res, pos = {}, 0
        for n in SMALL:
            size = int(np.prod(small_w[n].shape))
            res[n] = flat[pos:pos + size].reshape(small_w[n].shape)
            pos += size
        return res

    small_res = [unpack(v) for v in small_out]

    big_res = [{}, {}, {}, {}]
    for n in BIG:
        g = jnp.stack(big_g[n])
        big_res[0][n] = g
        for k, val in enumerate(_adamw("adamw_" + n, big_w[n], g, big_m[n], big_v[n])):
            big_res[k + 1][n] = val

    order = (("pre_g", small_res), ("w_in", big_res), ("b_gate", small_res), ("q_g", small_res),
             ("k_g", small_res), ("rpb", small_res), ("w_a", big_res), ("w_b", big_res), ("w_c", big_res),
             ("w_out", big_res), ("post_g", small_res))
    outs = [loss[0, 0], dy[None]]
    for k in range(4):
        outs.extend(src[k][n] for n, src in order)
    return tuple(outs)
```

```python
import functools

import numpy as np
import jax
import jax.numpy as jnp
from jax import lax
from jax.experimental import pallas as pl
from jax.experimental.pallas import tpu as pltpu

F32 = jnp.float32
BF16 = jnp.bfloat16
MESH = pl.DeviceIdType.MESH

HEAD = 128
GRID_W = 64
EPS = 1e-6
NEG = -1e30
NA_HEADS = 8
NA_WIN_R = 8
NA_WIN_C = 16
GQ_HEADS = 8
GKV_HEADS = 2
ROPE_BASE = 10000.0
DIL_GROUPS = ((128, 1), (512, 4), (2048, 16))
DIL_REACH = 64
DIL_HPG = 4
DIL_HEADS = 12
ALIBI_MAX_EXP = 8.0
ADAM_LR = 0.001
ADAM_B1 = 0.9
ADAM_B2 = 0.999
ADAM_EPS = 1e-08
ADAM_WD = 0.01
ADAM_STEP = 10

C_QA, C_KA, C_VA = 0, 8, 16
C_QB, C_KB, C_VB = 24, 32, 34
C_QC, C_KC, C_VC = 36, 48, 60
C_ZA, C_ZB, C_ZC = 72, 80, 88
C_GATE = 92

NA_QR = 4
NA_KR = NA_QR + NA_WIN_R
NA_VARIANTS = 3

GQA_FWD_TILE = (1024, 1024)
GQA_BWD_TILE = (1024, 1024)

VMEM_LIMIT_BYTES = 48 * 1024 * 1024
LANE = 128
N_CHIPS = 4
N_DEV = 8


def _params(sem=None):
    kw = dict(vmem_limit_bytes=VMEM_LIMIT_BYTES)
    if sem is not None:
        kw["dimension_semantics"] = sem
    return pltpu.CompilerParams(**kw)


def _tile(n, cap):
    units = n // LANE
    assert units * LANE == n, n
    best = 1
    for d in range(1, units + 1):
        if units % d == 0 and d * LANE <= cap:
            best = d
    return best * LANE


def _ew(name, fn, ins, outs, accs=(), *, rows, width, ncols, ts, carry=None):
    ts = min(ts, rows)
    assert rows % ts == 0
    nr = rows // ts
    ni, no = len(ins), len(outs)

    def imap(mode, off):
        if mode == "t":
            return lambda n, i: (i, off + n)
        if mode == "tf":
            return lambda n, i: (i, off)
        if mode == "v":
            return lambda n, i: (0, off + n)
        return lambda n, i: (0, off)

    in_specs = [pl.BlockSpec((ts if m in ("t", "tf") else 1, width), imap(m, o)) for (_, m, o) in ins]
    out_specs = [pl.BlockSpec((ts, width), lambda n, i: (i, n)) for _ in outs]
    out_shape = [jax.ShapeDtypeStruct((rows, c), d) for (c, d) in outs]
    for a in accs:
        if a == "n":
            out_specs.append(pl.BlockSpec((8, width), lambda n, i: (0, n)))
            out_shape.append(jax.ShapeDtypeStruct((8, width * ncols), F32))
        else:
            out_specs.append(pl.BlockSpec((8, width), lambda n, i: (0, 0)))
            out_shape.append(jax.ShapeDtypeStruct((8, width), F32))

    def body(*refs):
        res = fn(*[r[...] for r in refs[:ni]])
        if not isinstance(res, tuple):
            res = (res,)
        for r, val in zip(refs[ni:ni + no], res[:no]):
            r[...] = val.astype(r.dtype)
        n = pl.program_id(0)
        i = pl.program_id(1)
        for kind, r, val in zip(accs, refs[ni + no:], res[no:]):
            first = (i == 0) if kind == "n" else jnp.logical_and(i == 0, n == 0)

            @pl.when(first)
            def _(r=r):
                r[...] = jnp.zeros(r.shape, r.dtype)

            r[...] += jnp.broadcast_to(jnp.sum(val.astype(F32), axis=0, keepdims=True), r.shape)

    grid = (ncols, nr)
    body, c_in, c_in_specs, c_out, c_out_specs, c_scratch = _carried(body, ni, no + len(accs), grid, carry)
    res = pl.pallas_call(
        body, name=name, grid=grid, in_specs=in_specs + c_in_specs, out_specs=out_specs + c_out_specs,
        out_shape=out_shape + c_out, scratch_shapes=c_scratch,
        compiler_params=_params(("arbitrary", "arbitrary")),
    )(*[a for (a, _, _) in ins], *c_in)
    return res


def _mm_nn(name, a, w, *, out_dtype, tm=512, tn_cap=1024):
    M, K = a.shape
    blocked = w.ndim == 3
    NB = w.shape[-1]
    N = NB * (w.shape[0] if blocked else 1)
    tm = min(tm, M)
    tn = _tile(NB, tn_cap)
    per = NB // tn
    if blocked:
        w_spec = pl.BlockSpec((None, K, tn), lambda i, n: (n // per, 0, n % per))
    else:
        w_spec = pl.BlockSpec((K, tn), lambda i, n: (0, n))

    def body(a_ref, w_ref, o_ref):
        o_ref[...] = jnp.dot(a_ref[...], w_ref[...], preferred_element_type=F32).astype(o_ref.dtype)

    return pl.pallas_call(
        body, name=name, grid=(M // tm, N // tn),
        in_specs=[pl.BlockSpec((tm, K), lambda i, n: (i, 0)), w_spec],
        out_specs=pl.BlockSpec((tm, tn), lambda i, n: (i, n)),
        out_shape=jax.ShapeDtypeStruct((M, N), out_dtype),
        compiler_params=_params(("arbitrary", "arbitrary")),
    )(a, w)


def _mm_nt(name, a, w, *, out_dtype, tm=512, tk_cap=1024, carry=None, w_transposed=False):
    M, N = a.shape
    blocked = w.ndim == 3
    NB = w.shape[0] if w_transposed else w.shape[-1]
    K = w.shape[1] if w_transposed else w.shape[-2]
    assert N == NB * (w.shape[0] if blocked else 1)
    tm = min(tm, M)
    tk = _tile(NB, tk_cap)
    per = NB // tk
    nk = N // tk
    if w_transposed:
        w_spec = pl.BlockSpec((tk, K), lambda i, k: (k, 0))
        contract = (((1,), (0,)), ((), ()))
    elif blocked:
        w_spec = pl.BlockSpec((None, K, tk), lambda i, k: (k // per, 0, k % per))
        contract = (((1,), (1,)), ((), ()))
    else:
        w_spec = pl.BlockSpec((K, tk), lambda i, k: (0, k))
        contract = (((1,), (1,)), ((), ()))

    def body(a_ref, w_ref, o_ref, acc_ref):
        k = pl.program_id(1)

        @pl.when(k == 0)
        def _():
            acc_ref[...] = jnp.zeros(acc_ref.shape, F32)

        acc_ref[...] += lax.dot_general(a_ref[...], w_ref[...], contract, preferred_element_type=F32)

        @pl.when(k == nk - 1)
        def _():
            o_ref[...] = acc_ref[...].astype(o_ref.dtype)

    grid = (M // tm, nk)
    body, c_in, c_in_specs, c_out, c_out_specs, c_scratch = _carried(body, 2, 1, grid, carry)
    res = pl.pallas_call(
        body, name=name, grid=grid,
        in_specs=[pl.BlockSpec((tm, tk), lambda i, k: (i, k)), w_spec] + c_in_specs,
        out_specs=[pl.BlockSpec((tm, K), lambda i, k: (i, 0))] + c_out_specs,
        out_shape=[jax.ShapeDtypeStruct((M, K), out_dtype)] + c_out,
        scratch_shapes=[pltpu.VMEM((tm, K), F32)] + c_scratch,
        compiler_params=_params(("arbitrary", "arbitrary")),
    )(a, w, *c_in)
    return res[0] if carry is None else res


def _mm_tn(name, a, b, *, blocks, ts=512, tn_cap=1024):
    S, K = a.shape
    N = b.shape[1]
    NB = N // blocks
    ts = min(ts, S)
    tn = _tile(NB, tn_cap)
    per = NB // tn
    ns = S // ts

    def body(a_ref, b_ref, o_ref):
        s = pl.program_id(1)

        @pl.when(s == 0)
        def _():
            o_ref[...] = jnp.zeros(o_ref.shape, F32)

        o_ref[...] += lax.dot_general(a_ref[...], b_ref[...], (((0,), (0,)), ((), ())),
                                      preferred_element_type=F32)

    return pl.pallas_call(
        body, name=name, grid=(N // tn, ns),
        in_specs=[pl.BlockSpec((ts, K), lambda n, s: (s, 0)), pl.BlockSpec((ts, tn), lambda n, s: (s, n))],
        out_specs=pl.BlockSpec((None, K, tn), lambda n, s: (n // per, 0, n % per)),
        out_shape=jax.ShapeDtypeStruct((blocks, K, NB), F32),
        compiler_params=_params(("arbitrary", "arbitrary")),
    )(a, b)


def _aligned(off):
    return off if isinstance(off, int) else pl.multiple_of(off, LANE)


def _attn_bias(s, mode, extra_ref, i, tq, kc, dil, L):
    if mode == "table":
        return s + extra_ref[...]
    if mode == "dil":
        r = lax.broadcasted_iota(jnp.int32, (tq, kc), 0)
        tk = lax.broadcasted_iota(jnp.int32, (tq, kc), 1) - DIL_REACH
        dist = jnp.abs(r - tk)
        kin = (i * tq) % L + tk
        valid = (dist <= DIL_REACH) & (kin >= 0) & (kin < L)
        s = s - extra_ref[0:1, :] * (dil * dist).astype(F32)
        return jnp.where(valid, s, NEG)
    return s


def _attn_fwd(name, q, qo, k, ko, v, vo, *, H, G, S, tq, kw, kc, kstart, mode=None, extra=None,
              var=None, dil=1, L=1, out_dtype=BF16, prescaled=False, carry=None):
    nq = S // tq
    nch = kw // kc
    Sk = k.shape[0]
    scale = HEAD ** -0.5
    has_extra = mode is not None
    exp = jnp.exp2 if prescaled else jnp.exp
    log = jnp.log2 if prescaled else jnp.log

    def body(*refs):
        if has_extra:
            q_ref, k_ref, v_ref, e_ref, o_ref, l_ref = refs
        else:
            q_ref, k_ref, v_ref, o_ref, l_ref = refs
            e_ref = None
        i = pl.program_id(1)
        qb = q_ref[...]
        start = kstart(i)

        def chunk(j, carry):
            m, l, acc = carry
            off = _aligned(start + j * kc)
            kb = k_ref[pl.ds(off, kc), :]
            vb = v_ref[pl.ds(off, kc), :]
            s = lax.dot_general(qb, kb, (((1,), (1,)), ((), ())), preferred_element_type=F32)
            if not prescaled:
                s = s * scale
            s = _attn_bias(s, mode, e_ref, i, tq, kc, dil, L)
            mn = jnp.maximum(m, jnp.max(s, axis=1, keepdims=True))
            p = exp(s - mn)
            a = exp(m - mn)
            l = a * l + jnp.sum(p, axis=1, keepdims=True)
            acc = a * acc + jnp.dot(p.astype(BF16), vb, preferred_element_type=F32)
            return mn, l, acc

        init = (jnp.full((tq, 1), -3.0e38, F32), jnp.zeros((tq, 1), F32), jnp.zeros((tq, HEAD), F32))
        if nch == 1:
            m, l, acc = chunk(0, init)
        else:
            m, l, acc = lax.fori_loop(0, nch, chunk, init)
        o_ref[...] = (acc / l).astype(o_ref.dtype)
        l_ref[...] = jnp.broadcast_to(m + log(l), (tq, HEAD))

    in_specs = [
        pl.BlockSpec((tq, HEAD), lambda h, i: (i, qo + h)),
        pl.BlockSpec((Sk, HEAD), lambda h, i: (0, ko + h // G)),
        pl.BlockSpec((Sk, HEAD), lambda h, i: (0, vo + h // G)),
    ]
    args = [q, k, v]
    if mode == "table":
        in_specs.append(pl.BlockSpec((None, None, tq, kw), lambda h, i: (var(i), h, 0, 0)))
        args.append(extra)
    elif mode == "dil":
        in_specs.append(pl.BlockSpec((None, 8, kw), lambda h, i: (h, 0, 0)))
        args.append(extra)
    grid = (H, nq)
    body, c_in, c_in_specs, c_out, c_out_specs, c_scratch = _carried(body, len(args), 2, grid, carry)
    return pl.pallas_call(
        body, name=name, grid=grid, in_specs=in_specs + c_in_specs,
        out_specs=[pl.BlockSpec((tq, HEAD), lambda h, i: (i, h)),
                   pl.BlockSpec((tq, HEAD), lambda h, i: (i, h))] + c_out_specs,
        out_shape=[jax.ShapeDtypeStruct((S, H * HEAD), out_dtype),
                   jax.ShapeDtypeStruct((S, H * HEAD), F32)] + c_out,
        scratch_shapes=c_scratch,
        compiler_params=_params(("arbitrary", "arbitrary")),
    )(*args, *c_in)


def _attn_bwd(name, q, qo, k, ko, v, vo, o, do, lse, *, H, G, S, tq, kw, kc, kstart, mode=None, extra=None,
              var=None, dil=1, L=1, prescaled=False, carry=None):
    nq = S // tq
    nch = kw // kc
    Sk = k.shape[0]
    HK = H // G
    scale = HEAD ** -0.5
    dk_mult = float(np.log(2.0)) if prescaled else scale
    exp = jnp.exp2 if prescaled else jnp.exp
    has_extra = mode is not None
    table = mode == "table"
    if table:
        assert nq >= NA_VARIANTS and nch == 1

    def body(*refs):
        refs = list(refs)
        q_ref, k_ref, v_ref, o_ref, do_ref, l_ref = refs[:6]
        e_ref = refs[6] if has_extra else None
        outs = refs[7:] if has_extra else refs[6:]
        dq_ref, dk_ref, dv_ref = outs[:3]
        h = pl.program_id(0)
        i = pl.program_id(1)

        @pl.when(jnp.logical_and(h % G == 0, i == 0))
        def _():
            dk_ref[...] = jnp.zeros(dk_ref.shape, F32)
            dv_ref[...] = jnp.zeros(dv_ref.shape, F32)

        if table:
            db_ref = outs[3]

            @pl.when((i == 0) | (i == 1) | (i == nq - 1))
            def _():
                db_ref[...] = jnp.zeros(db_ref.shape, F32)

        qb = q_ref[...]
        dob = do_ref[...]
        delta = jnp.sum(o_ref[...].astype(F32) * dob.astype(F32), axis=1, keepdims=True)
        lse_b = l_ref[:, 0:1]
        start = kstart(i)

        def chunk(j, dq):
            off = _aligned(start + j * kc)
            kb = k_ref[pl.ds(off, kc), :]
            vb = v_ref[pl.ds(off, kc), :]
            s = lax.dot_general(qb, kb, (((1,), (1,)), ((), ())), preferred_element_type=F32)
            if not prescaled:
                s = s * scale
            s = _attn_bias(s, mode, e_ref, i, tq, kc, dil, L)
            p = exp(s - lse_b)
            dp = lax.dot_general(dob, vb, (((1,), (1,)), ((), ())), preferred_element_type=F32)
            ds = p * (dp - delta)
            dsb = ds.astype(BF16)
            dk_ref[pl.ds(off, kc), :] += dk_mult * lax.dot_general(
                dsb, qb, (((0,), (0,)), ((), ())), preferred_element_type=F32)
            dv_ref[pl.ds(off, kc), :] += lax.dot_general(
                p.astype(BF16), dob, (((0,), (0,)), ((), ())), preferred_element_type=F32)
            if table:
                db_ref[...] += ds
            return dq + jnp.dot(dsb, kb, preferred_element_type=F32)

        dq0 = jnp.zeros((tq, HEAD), F32)
        dq = chunk(0, dq0) if nch == 1 else lax.fori_loop(0, nch, chunk, dq0)
        dq_ref[...] = (dq * scale).astype(dq_ref.dtype)

    in_specs = [
        pl.BlockSpec((tq, HEAD), lambda h, i: (i, qo + h)),
        pl.BlockSpec((Sk, HEAD), lambda h, i: (0, ko + h // G)),
        pl.BlockSpec((Sk, HEAD), lambda h, i: (0, vo + h // G)),
        pl.BlockSpec((tq, HEAD), lambda h, i: (i, h)),
        pl.BlockSpec((tq, HEAD), lambda h, i: (i, h)),
        pl.BlockSpec((tq, HEAD), lambda h, i: (i, h)),
    ]
    args = [q, k, v, o, do, lse]
    out_specs = [
        pl.BlockSpec((tq, HEAD), lambda h, i: (i, h)),
        pl.BlockSpec((Sk, HEAD), lambda h, i: (0, h // G)),
        pl.BlockSpec((Sk, HEAD), lambda h, i: (0, h // G)),
    ]
    out_shape = [
        jax.ShapeDtypeStruct((S, H * HEAD), BF16),
        jax.ShapeDtypeStruct((Sk, HK * HEAD), F32),
        jax.ShapeDtypeStruct((Sk, HK * HEAD), F32),
    ]
    if table:
        in_specs.append(pl.BlockSpec((None, None, tq, kw), lambda h, i: (var(i), h, 0, 0)))
        args.append(extra)
        out_specs.append(pl.BlockSpec((None, None, tq, kw), lambda h, i: (var(i), h, 0, 0)))
        out_shape.append(jax.ShapeDtypeStruct(extra.shape, F32))
    elif mode == "dil":
        in_specs.append(pl.BlockSpec((None, 8, kw), lambda h, i: (h, 0, 0)))
        args.append(extra)
    grid = (H, nq)
    body, c_in, c_in_specs, c_out, c_out_specs, c_scratch = _carried(body, len(args), len(out_shape), grid, carry)
    return pl.pallas_call(
        body, name=name, grid=grid, in_specs=in_specs + c_in_specs, out_specs=out_specs + c_out_specs,
        out_shape=out_shape + c_out, scratch_shapes=c_scratch,
        compiler_params=_params(("arbitrary", "arbitrary")),
    )(*args, *c_in)


def _na_constants():
    pairs = NA_VARIANTS * NA_QR * NA_KR
    sel = np.zeros((pairs, 16), np.float32)
    row_ok = np.zeros((pairs, 1), np.float32)
    for var in range(NA_VARIANTS):
        for qr in range(NA_QR):
            for kr in range(NA_KR):
                p = (var * NA_QR + qr) * NA_KR + kr
                first_key = (0, qr, NA_QR)[var]
                dr = kr - qr + (NA_WIN_R - 1) - (0, NA_QR, 2 * NA_QR)[var]
                if first_key <= kr < first_key + NA_WIN_R:
                    assert 0 <= dr < 2 * NA_WIN_R - 1
                    sel[p, dr] = 1.0
                    row_ok[p, 0] = 1.0
    j = np.arange(GRID_W)[:, None]
    c = np.arange(GRID_W)[None, :]
    cs = np.clip(j - NA_WIN_C // 2, 0, GRID_W - NA_WIN_C)
    col_ok = ((c >= cs) & (c < cs + NA_WIN_C)).astype(np.float32).reshape(1, GRID_W * GRID_W)
    dc = np.clip(c - j + NA_WIN_C - 1, 0, 2 * NA_WIN_C - 2).reshape(-1)
    toe = np.zeros((32, GRID_W * GRID_W), np.float32)
    toe[dc, np.arange(GRID_W * GRID_W)] = 1.0
    return sel, row_ok, col_ok, toe


def _split3(x):
    hi = x.astype(BF16)
    r1 = x - hi.astype(F32)
    mid = r1.astype(BF16)
    lo = (r1 - mid.astype(F32)).astype(BF16)
    return hi, mid, lo


def _dot_sel_left(onehot_bf16, x):
    return sum(jnp.dot(onehot_bf16, t, preferred_element_type=F32) for t in _split3(x))


def _dot_sel_right(x, onehot_bf16):
    return sum(jnp.dot(t, onehot_bf16, preferred_element_type=F32) for t in _split3(x))


def _na_bias_build(rpb_l):
    sel, row_ok, col_ok, toe = _na_constants()
    pairs = sel.shape[0]
    ww = GRID_W * GRID_W
    t = jnp.pad(rpb_l, ((0, 0), (0, 1), (0, 1)))

    def body(t_ref, sel_ref, toe_ref, rok_ref, cok_ref, o_ref):
        rows = _dot_sel_right(t_ref[...], toe_ref[...])
        val = _dot_sel_left(sel_ref[...], rows)
        ok = rok_ref[...] * cok_ref[...]
        o_ref[...] = jnp.where(ok > 0.5, val, NEG)

    full = lambda shape: pl.BlockSpec(shape, lambda h: (0,) * len(shape))
    out = pl.pallas_call(
        body, name="na_bias_build", grid=(NA_HEADS,),
        in_specs=[pl.BlockSpec((None, 16, 32), lambda h: (h, 0, 0)), full((pairs, 16)), full((32, ww)),
                  full((pairs, 1)), full((1, ww))],
        out_specs=pl.BlockSpec((None, pairs, ww), lambda h: (h, 0, 0)),
        out_shape=jax.ShapeDtypeStruct((NA_HEADS, pairs, ww), F32),
        compiler_params=_params(("arbitrary",)),
    )(t, jnp.asarray(sel, BF16), jnp.asarray(toe, BF16), jnp.asarray(row_ok), jnp.asarray(col_ok))
    out = out.reshape(NA_HEADS, NA_VARIANTS, NA_QR, NA_KR, GRID_W, GRID_W)
    out = out.transpose(1, 0, 2, 4, 3, 5)
    return out.reshape(NA_VARIANTS, NA_HEADS, NA_QR * GRID_W, NA_KR * GRID_W)


def _na_bias_grad(dbias):
    sel, _, _, toe = _na_constants()
    pairs = sel.shape[0]
    ww = GRID_W * GRID_W
    d = dbias.reshape(NA_VARIANTS, NA_HEADS, NA_QR, GRID_W, NA_KR, GRID_W).transpose(1, 0, 2, 4, 3, 5)
    d = d.reshape(NA_HEADS, pairs, ww)

    def body(d_ref, selt_ref, toet_ref, o_ref):
        rows = _dot_sel_left(selt_ref[...], d_ref[...])
        o_ref[...] = _dot_sel_right(rows, toet_ref[...])

    full = lambda shape: pl.BlockSpec(shape, lambda h: (0,) * len(shape))
    out = pl.pallas_call(
        body, name="na_bias_grad", grid=(NA_HEADS,),
        in_specs=[pl.BlockSpec((None, pairs, ww), lambda h: (h, 0, 0)), full((16, pairs)), full((ww, 32))],
        out_specs=pl.BlockSpec((None, 16, 32), lambda h: (h, 0, 0)),
        out_shape=jax.ShapeDtypeStruct((NA_HEADS, 16, 32), F32),
        compiler_params=_params(("arbitrary",)),
    )(d, jnp.asarray(sel.T, BF16), jnp.asarray(toe.T, BF16))
    return out[:, :2 * NA_WIN_R - 1, :2 * NA_WIN_C - 1]


def _sigmoid(x):
    return 1.0 / (1.0 + jnp.exp(-x))


def _swap32(y):
    lane = lax.broadcasted_iota(jnp.int32, y.shape, 1)
    up = pltpu.roll(y, HEAD - 32, 1)
    down = pltpu.roll(y, 32, 1)
    return jnp.where((lane // 32) % 2 == 0, up, down)


def _rope_tables(S):
    t = jnp.arange(S, dtype=jnp.int32)
    freqs = ROPE_BASE ** (-jnp.arange(32, dtype=F32) / 32)
    ar = (t // GRID_W).astype(F32)[:, None] * freqs[None, :]
    ac = (t % GRID_W).astype(F32)[:, None] * freqs[None, :]
    cos = jnp.concatenate([jnp.cos(ar), jnp.cos(ar), jnp.cos(ac), jnp.cos(ac)], axis=1)
    sin = jnp.concatenate([-jnp.sin(ar), jnp.sin(ar), -jnp.sin(ac), jnp.sin(ac)], axis=1)
    return cos, sin


def _rms_stats(x):
    x = x.astype(F32)
    r = lax.rsqrt(jnp.mean(x * x, axis=-1, keepdims=True) + EPS)
    return x * r, r


def _rms_bwd(xhat, r, dy, g):
    dxh = dy * g
    return r * (dxh - xhat * jnp.mean(dxh * xhat, axis=-1, keepdims=True))


def _qk_fwd(x, g, cos, sin):
    xhat, _ = _rms_stats(x)
    y = xhat * g
    return y * cos + _swap32(y) * sin


def _qk_bwd(dyr, x, g, cos, sin):
    dyr = dyr.astype(F32)
    dy = dyr * cos + _swap32(dyr * sin)
    xhat, r = _rms_stats(x)
    return _rms_bwd(xhat, r, dy, g), dy * xhat


def _perm(a, dil):
    if dil == 1:
        return a
    S = a.shape[0]
    return a.reshape(S // dil, dil, *a.shape[1:]).swapaxes(0, 1).reshape(a.shape)


def _unperm(a, dil):
    if dil == 1:
        return a
    S = a.shape[0]
    return a.reshape(dil, S // dil, *a.shape[1:]).swapaxes(0, 1).reshape(a.shape)


def _place():
    x, y, c = lax.axis_index("x"), lax.axis_index("y"), lax.axis_index("c")
    chips = [(1 - x, y), (x, 1 - y), (1 - x, 1 - y)]
    return x, y, c, chips


def _remote(src, dst, send, recv, dev):
    return pltpu.make_async_remote_copy(src_ref=src, dst_ref=dst, send_sem=send, recv_sem=recv,
                                        device_id=dev, device_id_type=MESH)


ANY = pl.BlockSpec(memory_space=pl.ANY)


class _Xfer:
    def __init__(self, send, recv):
        self.send, self.recv = send, recv

    def start(self):
        self.send().start()

    def wait(self):
        self.recv().wait_recv()
        self.send().wait_send()


def _sem_shapes(nsem):
    return [pltpu.SemaphoreType.DMA((nsem,))] * 2


def _carried(body, n_in, n_out, grid, carry):
    if carry is None:
        return body, [], [], [], [], []
    ins, out_shapes, plan, nsem = carry
    ci, co = len(ins), len(out_shapes)

    def new_body(*refs):
        a, cin = refs[:n_in], refs[n_in:n_in + ci]
        b = refs[n_in + ci:n_in + ci + n_out]
        cout = refs[n_in + ci + n_out:n_in + ci + n_out + co]
        rest = refs[n_in + ci + n_out + co:]
        scratch, sems = rest[:-2], rest[-2:]
        ids = [pl.program_id(d) for d in range(len(grid))]
        first = functools.reduce(jnp.logical_and, [i == 0 for i in ids])
        last = functools.reduce(jnp.logical_and, [i == g - 1 for i, g in zip(ids, grid)])

        @pl.when(first)
        def _():
            for t in plan(cin, cout, *sems):
                t.start()

        body(*a, *b, *scratch)

        @pl.when(last)
        def _():
            for t in plan(cin, cout, *sems):
                t.wait()

    return new_body, list(ins), [ANY] * ci, list(out_shapes), [ANY] * co, _sem_shapes(nsem)


def _exchange(name, carry):
    ins, out_shapes, plan, nsem = carry
    n = len(ins)

    def body(*refs):
        items = plan(refs[:n], refs[n:n + len(out_shapes)], *refs[-2:])
        for t in items:
            t.start()
        for t in items:
            t.wait()

    return pl.pallas_call(body, name=name, in_specs=[ANY] * n, out_specs=[ANY] * len(out_shapes),
                          out_shape=list(out_shapes), scratch_shapes=_sem_shapes(nsem))(*ins)


def _plan_gather(ins, outs, send, recv):
    x, y, c, chips = _place()
    me = 2 * x + y
    items = []
    for w, (src, dst) in enumerate(zip(ins, outs)):
        for r, (px, py) in enumerate(chips):
            k = 3 * w + r
            to = functools.partial(_remote, src, dst.at[me], send.at[k], recv.at[k], (px, py, c))
            frm = functools.partial(_remote, src, dst.at[2 * px + py], send.at[k], recv.at[k], (px, py, c))
            items.append(_Xfer(send=to, recv=frm))
    return items


def _carry_gather(shards):
    shapes = [jax.ShapeDtypeStruct((N_CHIPS,) + a.shape, a.dtype) for a in shards]
    return (shards, shapes, _plan_gather, 3 * len(shards))


def _plan_pair_exchange(ins, outs, send, recv):
    x, y, c, _ = _place()
    items = []
    for w, (src, dst) in enumerate(zip(ins, outs)):
        hr = src.shape[1] // 2
        theirs = pl.ds(pl.multiple_of((1 - c) * hr, 8), hr)
        cp = functools.partial(_remote, src.at[:, theirs, :], dst, send.at[w], recv.at[w], (x, y, 1 - c))
        items.append(_Xfer(send=cp, recv=cp))
    return items


def _carry_pair_exchange(gs):
    shapes = [jax.ShapeDtypeStruct((a.shape[0], a.shape[1] // 2, a.shape[2]), a.dtype) for a in gs]
    return (gs, shapes, _plan_pair_exchange, len(gs))


def _plan_chip_exchange(ins, outs, send, recv):
    x, y, c, chips = _place()
    items = []
    for w, (src, dst) in enumerate(zip(ins, outs)):
        for r, (px, py) in enumerate(chips):
            k = 3 * w + r
            cp = functools.partial(_remote, src.at[2 * px + py], dst.at[r], send.at[k], recv.at[k], (px, py, c))
            items.append(_Xfer(send=cp, recv=cp))
    return items


def _carry_chip_exchange(qs):
    shapes = [jax.ShapeDtypeStruct((3,) + a.shape[1:], a.dtype) for a in qs]
    return (qs, shapes, _plan_chip_exchange, 3 * len(qs))


def _plan_pair_complete(ins, outs, send, recv):
    x, y, c, _ = _place()
    items = []
    for w, (src, dst) in enumerate(zip(ins, outs)):
        cp = functools.partial(_remote, src, dst, send.at[w], recv.at[w], (x, y, 1 - c))
        items.append(_Xfer(send=cp, recv=cp))
    return items


def _carry_pair_complete(rs):
    shapes = [jax.ShapeDtypeStruct(a.shape, a.dtype) for a in rs]
    return (rs, shapes, _plan_pair_complete, len(rs))


def _join_halves(mine, other):
    first = lax.axis_index("c") == 0
    return jnp.concatenate([jnp.where(first, mine, other), jnp.where(first, other, mine)])


def _allgather_weights(ws):
    n = len(ws)

    def body(*refs):
        ins, outs = refs[:n], refs[n:2 * n]
        send, recv = refs[2 * n:]
        x, y, c, chips = _place()
        me = 2 * x + y
        sib = (x, y, 1 - c)
        sends = []
        for w in range(n):
            hr = ins[w].shape[0] // 2
            mine = pl.ds(pl.multiple_of(c * hr, 16), hr)
            for r, (px, py) in enumerate(chips):
                cp = _remote(ins[w].at[mine, :], outs[w].at[me, mine, :], send.at[w, r], recv.at[w, r], (px, py, c))
                cp.start()
                sends.append(cp)
        for w in range(n):
            hr = ins[w].shape[0] // 2
            mine = pl.ds(pl.multiple_of(c * hr, 16), hr)
            for r, (px, py) in enumerate(chips):
                blk = outs[w].at[2 * px + py, mine, :]
                _remote(blk, blk, send.at[w, r], recv.at[w, r], (px, py, c)).wait_recv()
                cp = _remote(blk, blk, send.at[w, 3 + r], recv.at[w, 3 + r], sib)
                cp.start()
                sends.append(cp)
        for w in range(n):
            hr = ins[w].shape[0] // 2
            other = pl.ds(pl.multiple_of((1 - c) * hr, 16), hr)
            for r, (px, py) in enumerate(chips):
                blk = outs[w].at[2 * px + py, other, :]
                _remote(blk, blk, send.at[w, 3 + r], recv.at[w, 3 + r], sib).wait_recv()
        for cp in sends:
            cp.wait_send()

    out_shape = [jax.ShapeDtypeStruct((N_CHIPS,) + a.shape, a.dtype) for a in ws]
    return pl.pallas_call(
        body, name="allgather_weights", in_specs=[ANY] * n, out_specs=[ANY] * n, out_shape=out_shape,
        scratch_shapes=[pltpu.SemaphoreType.DMA((n, 6)), pltpu.SemaphoreType.DMA((n, 6))],
    )(*ws)


def _allreduce_small(vec, loss_rows, loss_scale):
    rows = vec.shape[0]

    def body(v_ref, tot_ref, loss_ref, gat_ref, send, recv):
        x, y, c, _ = _place()
        me = 4 * x + 2 * y + c
        gat_ref[me] = v_ref[...]
        peers = []
        for r in range(1, N_DEV):
            px = 1 - x if r & 4 else x
            py = 1 - y if r & 2 else y
            pc = 1 - c if r & 1 else c
            peers.append((px, py, pc))
            _remote(v_ref, gat_ref.at[me], send.at[r - 1], recv.at[r - 1], (px, py, pc)).start()
        for r, (px, py, pc) in enumerate(peers):
            _remote(v_ref, gat_ref.at[4 * px + 2 * py + pc], send.at[r], recv.at[r], (px, py, pc)).wait_recv()
        for r, (px, py, pc) in enumerate(peers):
            _remote(v_ref, gat_ref.at[me], send.at[r], recv.at[r], (px, py, pc)).wait_send()
        tot = gat_ref[0]
        for d in range(1, N_DEV):
            tot = tot + gat_ref[d]
        tot_ref[...] = tot
        loss_ref[...] = jnp.full(loss_ref.shape, loss_scale, F32) * jnp.sum(tot[:loss_rows])

    vm = pl.BlockSpec(memory_space=pltpu.VMEM)
    tot, loss = pl.pallas_call(
        body, name="allreduce_small", in_specs=[vm], out_specs=[vm, vm],
        out_shape=[jax.ShapeDtypeStruct((rows, LANE), F32), jax.ShapeDtypeStruct((8, LANE), F32)],
        scratch_shapes=[pltpu.VMEM((N_DEV, rows, LANE), F32), pltpu.SemaphoreType.DMA((N_DEV - 1,)),
                        pltpu.SemaphoreType.DMA((N_DEV - 1,))],
        compiler_params=_params(),
    )(vec)
    return tot, loss


def _add_rows(name, terms):
    shape = terms[0].shape
    C = shape[-1]
    flat = [t.reshape(-1, C) for t in terms]
    rows = flat[0].shape[0]
    width = _tile(C, 1024)

    def fn(*vals):
        tot = vals[0]
        for val in vals[1:]:
            tot = tot + val
        return tot

    (out,) = _ew(name, fn, [(t, "t", 0) for t in flat], [(C, F32)], rows=rows, width=width, ncols=C // width,
                 ts=512)
    return out.reshape(shape)


def _adamw(name, w, g, m, v):
    shape = w.shape
    C = shape[-1]
    flat = [t.reshape(-1, C) for t in (w, g, m, v)]
    rows = flat[0].shape[0]
    width = _tile(C, 1024)

    def fn(w, g, m, v):
        m2 = ADAM_B1 * m + (1.0 - ADAM_B1) * g
        v2 = ADAM_B2 * v + (1.0 - ADAM_B2) * (g * g)
        m_hat = m2 / (1.0 - ADAM_B1 ** ADAM_STEP)
        v_hat = v2 / (1.0 - ADAM_B2 ** ADAM_STEP)
        delta = -ADAM_LR * (m_hat / (jnp.sqrt(v_hat) + ADAM_EPS) + ADAM_WD * w)
        return delta, m2, v2

    outs = _ew(name, fn, [(t, "t", 0) for t in flat], [(C, F32)] * 3, rows=rows, width=width,
               ncols=C // width, ts=min(512, rows))
    return [o.reshape(shape) for o in outs]


def _na_geometry(S):
    rows = S // GRID_W
    nq = rows // NA_QR
    tq = NA_QR * GRID_W
    kw = NA_KR * GRID_W

    def kstart(i):
        return jnp.clip(NA_QR * i - NA_WIN_R // 2, 0, rows - NA_KR) * GRID_W

    def var(i):
        return jnp.where(i == 0, 0, jnp.where(i == nq - 1, 2, 1))

    return dict(S=S, tq=tq, kw=kw, kc=kw, kstart=kstart, var=var, mode="table")


def _dil_geometry(S, dil):
    L = S // dil
    tq = min(512, L)
    kw = tq + 2 * DIL_REACH
    return dict(S=S, tq=tq, kw=kw, kc=kw, kstart=lambda i: i * tq, mode="dil", dil=dil, L=L)


def _dil_slopes(g, kw):
    idx = jnp.arange(1, DIL_HEADS + 1, dtype=F32)[g * DIL_HPG:(g + 1) * DIL_HPG]
    slopes = 2.0 ** (-ALIBI_MAX_EXP * idx / DIL_HEADS)
    return jnp.broadcast_to(slopes[:, None, None], (DIL_HPG, 8, kw))


def _layer_fwd(x, p, cos, sin, next_shards=None):
    S, D = x.shape
    ts = 256
    gp = p["pre_g"][None, :]
    (h,) = _ew("pre_norm", lambda x, g: _rms_stats(x)[0] * g, [(x, "t", 0), (gp, "v", 0)], [(D, BF16)],
               rows=S, width=D, ncols=1, ts=ts)
    proj = _mm_nn("proj_in", h, p["w_in"], out_dtype=BF16)

    bias = _na_bias_build(p["rpb"])
    ya, lse_a = _attn_fwd("na_fwd", proj, C_QA, proj, C_KA, proj, C_VA, H=NA_HEADS, G=1, extra=bias,
                          **_na_geometry(S))

    def qk(name, off, nh, g, mult):
        fn = lambda x, g, cos, sin: _qk_fwd(x, g, cos, sin) * mult
        (out,) = _ew(name, fn, [(proj, "t", off), (g[None, :], "vf", 0), (cos, "tf", 0), (sin, "tf", 0)],
                     [(nh * HEAD, BF16)], rows=S, width=HEAD, ncols=nh, ts=512)
        return out

    qb = qk("q_norm_rope", C_QB, GQ_HEADS, p["q_g"], HEAD ** -0.5 * float(np.log2(np.e)))
    kb = qk("k_norm_rope", C_KB, GKV_HEADS, p["k_g"], 1.0)
    geo_b = dict(S=S, tq=min(GQA_FWD_TILE[0], S), kw=S, kc=min(GQA_FWD_TILE[1], S), kstart=lambda i: 0,
                 prescaled=True)
    carry = None if next_shards is None else _carry_gather(next_shards)
    yb, lse_b, *next_full = _attn_fwd("gqa_fwd", qb, 0, kb, 0, proj, C_VB, H=GQ_HEADS, G=GQ_HEADS // GKV_HEADS,
                                      carry=carry, **geo_b)

    W = DIL_HPG * HEAD
    og, lg, saved_c = [], [], []
    for g, (win, dil) in enumerate(DIL_GROUPS):
        assert (win // 2) // dil == DIL_REACH
        sl = lambda off: proj[:, (off + g * DIL_HPG) * HEAD:(off + (g + 1) * DIL_HPG) * HEAD]
        qg = _perm(sl(C_QC), dil)
        kg = jnp.pad(_perm(sl(C_KC), dil), ((DIL_REACH, DIL_REACH), (0, 0)))
        vg = jnp.pad(_perm(sl(C_VC), dil), ((DIL_REACH, DIL_REACH), (0, 0)))
        geo = _dil_geometry(S, dil)
        o, l = _attn_fwd(f"dil{g}_fwd", qg, 0, kg, 0, vg, 0, H=DIL_HPG, G=1, extra=_dil_slopes(g, geo["kw"]),
                         out_dtype=F32, **geo)
        og.append(_unperm(o, dil))
        lg.append(_unperm(l, dil))
        saved_c.append((qg, kg, vg))

    def combine(o1, o2, o3, l1, l2, l3):
        m = jnp.maximum(jnp.maximum(l1, l2), l3)
        e1, e2, e3 = jnp.exp(l1 - m), jnp.exp(l2 - m), jnp.exp(l3 - m)
        tot = e1 + e2 + e3
        return (e1 * o1 + e2 * o2 + e3 * o3) / tot, m + jnp.log(tot)

    yc, lse_c = _ew("dil_combine", combine, [(a, "t", 0) for a in og + lg], [(W, BF16), (W, F32)],
                    rows=S, width=W, ncols=1, ts=512)

    def gate(y, z):
        z = z.astype(F32)
        return y.astype(F32) * (z * _sigmoid(z))

    def branch(name, y, zoff, width, w):
        unit = width // HEAD
        (u,) = _ew(name + "_silu", gate, [(y, "t", 0), (proj, "tf", zoff // unit)], [(width, BF16)],
                   rows=S, width=width, ncols=1, ts=512)
        return u, _mm_nn(name + "_proj", u, w, out_dtype=BF16)

    ua, ta = branch("branch_a", ya, C_ZA, NA_HEADS * HEAD, p["w_a"])
    ub, tb = branch("branch_b", yb, C_ZB, GQ_HEADS * HEAD, p["w_b"])
    uc, tc = branch("branch_c", yc, C_ZC, W, p["w_c"])

    mw = 512
    nm = D // mw
    goff = C_GATE * HEAD // mw
    bg = p["b_gate"][None, :]

    def merge(ta, tb, tc, ga, gb, gc, ba, bb, bc):
        sa = _sigmoid(ga.astype(F32) + ba)
        sb = _sigmoid(gb.astype(F32) + bb)
        sc = _sigmoid(gc.astype(F32) + bc)
        return sa * ta.astype(F32) + sb * tb.astype(F32) + sc * tc.astype(F32)

    (merged,) = _ew("merge", merge,
                    [(ta, "t", 0), (tb, "t", 0), (tc, "t", 0),
                     (proj, "t", goff), (proj, "t", goff + nm), (proj, "t", goff + 2 * nm),
                     (bg, "v", 0), (bg, "v", nm), (bg, "v", 2 * nm)],
                    [(D, BF16)], rows=S, width=mw, ncols=nm, ts=512)
    out = _mm_nn("proj_out", merged, p["w_out"], out_dtype=F32)
    gq = p["post_g"][None, :]
    (x_next,) = _ew("post_norm_residual", lambda x, o, g: x + _rms_stats(o)[0] * g,
                    [(x, "t", 0), (out, "t", 0), (gq, "v", 0)], [(D, F32)], rows=S, width=D, ncols=1, ts=ts)
    saved = dict(x=x, h=h, proj=proj, bias=bias, ya=ya, lse_a=lse_a, qb=qb, kb=kb, yb=yb, lse_b=lse_b,
                 geo_b=geo_b, saved_c=saved_c, yc=yc, lse_c=lse_c, ua=ua, ub=ub, uc=uc, ta=ta, tb=tb, tc=tc,
                 merged=merged, out=out)
    return x_next, saved, (_fill_own(next_full, next_shards) if next_full else None)


def _pair_sums(gs, got):
    c = lax.axis_index("c")
    out = []
    for g, t in zip(gs, got):
        hr = g.shape[1] // 2
        out.append(_add_rows("grad_pair_sum", [lax.dynamic_slice_in_dim(g, c * hr, hr, axis=1), t]))
    return out


def _chip_sums(pair, recv):
    me = 2 * lax.axis_index("x") + lax.axis_index("y")
    out = []
    for q, t in zip(pair, recv):
        own = lax.dynamic_index_in_dim(q, me, axis=0, keepdims=False)
        out.append(_add_rows("grad_chip_sum", [own, t[0], t[1], t[2]]))
    return out


def _fill_own(gathered, shards):
    me = 2 * lax.axis_index("x") + lax.axis_index("y")
    return [lax.dynamic_update_slice_in_dim(g, a[None], me, axis=0) for g, a in zip(gathered, shards)]


def _layer_bwd(gy, p, s, cos, sin, pending=None):
    S, D = gy.shape
    ts = 256
    proj = s["proj"]
    grads = {}

    def post_bwd(o, dy, g):
        xhat, r = _rms_stats(o)
        return _rms_bwd(xhat, r, dy, g), dy * xhat

    dout, dg_post = _ew("post_norm_bwd", post_bwd, [(s["out"], "t", 0), (gy, "t", 0), (p["post_g"][None, :], "v", 0)],
                        [(D, BF16)], accs=("n",), rows=S, width=D, ncols=1, ts=ts)
    grads["post_g"] = dg_post[0]
    dmerged = _mm_nt("proj_out_bwd", dout, p["w_out"], out_dtype=BF16)
    grads["w_out"] = _mm_tn("proj_out_wgrad", s["merged"], dout, blocks=1).reshape(N_CHIPS, D // N_CHIPS, D)

    mw = 512
    nm = D // mw
    goff = C_GATE * HEAD // mw
    bg = p["b_gate"][None, :]

    def merge_bwd(dm, ta, tb, tc, ga, gb, gc, ba, bb, bc):
        dm = dm.astype(F32)
        res_t, res_g = [], []
        for t, g, b in ((ta, ga, ba), (tb, gb, bb), (tc, gc, bc)):
            sg = _sigmoid(g.astype(F32) + b)
            res_t.append(dm * sg)
            res_g.append(dm * t.astype(F32) * sg * (1.0 - sg))
        return (*res_t, *res_g, *res_g)

    mres = _ew("merge_bwd", merge_bwd,
               [(dmerged, "t", 0), (s["ta"], "t", 0), (s["tb"], "t", 0), (s["tc"], "t", 0),
                (proj, "t", goff), (proj, "t", goff + nm), (proj, "t", goff + 2 * nm),
                (bg, "v", 0), (bg, "v", nm), (bg, "v", 2 * nm)],
               [(D, BF16)] * 6, accs=("n", "n", "n"), rows=S, width=mw, ncols=nm, ts=512,
               carry=None if pending is None else _carry_pair_exchange(pending))
    dta, dtb, dtc, dga, dgb, dgc = mres[:6]
    grads["b_gate"] = jnp.concatenate([a[0] for a in mres[6:9]])
    pair = None if pending is None else _pair_sums(pending, mres[9:])

    def silu_bwd(du, y, z):
        du, y, z = du.astype(F32), y.astype(F32), z.astype(F32)
        sg = _sigmoid(z)
        return du * (z * sg), du * y * (sg * (1.0 + z * (1.0 - sg)))

    def branch_bwd(name, dt, u, y, zoff, width, w):
        du = _mm_nt(name + "_proj_bwd", dt, w, out_dtype=BF16)
        gw = _mm_tn(name + "_wgrad", u, dt, blocks=N_CHIPS)
        unit = width // HEAD
        dy, dz = _ew(name + "_silu_bwd", silu_bwd, [(du, "t", 0), (y, "t", 0), (proj, "tf", zoff // unit)],
                     [(width, BF16), (width, BF16)], rows=S, width=width, ncols=1, ts=512)
        return dy, dz, gw

    W = DIL_HPG * HEAD
    dya, dza, grads["w_a"] = branch_bwd("branch_a", dta, s["ua"], s["ya"], C_ZA, NA_HEADS * HEAD, p["w_a"])
    dyb, dzb, grads["w_b"] = branch_bwd("branch_b", dtb, s["ub"], s["yb"], C_ZB, GQ_HEADS * HEAD, p["w_b"])
    dyc, dzc, grads["w_c"] = branch_bwd("branch_c", dtc, s["uc"], s["yc"], C_ZC, W, p["w_c"])

    dqa, dka, dva, dbias = _attn_bwd("na_bwd", proj, C_QA, proj, C_KA, proj, C_VA, s["ya"], dya, s["lse_a"],
                                     H=NA_HEADS, G=1, extra=s["bias"], **_na_geometry(S))
    grads["rpb"] = _na_bias_grad(dbias)

    geo_b = dict(s["geo_b"])
    geo_b["tq"], geo_b["kc"] = min(GQA_BWD_TILE[0], S), min(GQA_BWD_TILE[1], S)
    dqr, dkr, dvb, *recv = _attn_bwd("gqa_bwd", s["qb"], 0, s["kb"], 0, proj, C_VB, s["yb"], dyb, s["lse_b"],
                                     H=GQ_HEADS, G=GQ_HEADS // GKV_HEADS,
                                     carry=None if pending is None else _carry_chip_exchange(pair), **geo_b)
    half = None if pending is None else _chip_sums(pair, recv)

    def qk_bwd(name, dyr, off, nh, g):
        return _ew(name, _qk_bwd, [(dyr, "t", 0), (proj, "t", off), (g[None, :], "vf", 0), (cos, "tf", 0),
                                   (sin, "tf", 0)],
                   [(nh * HEAD, BF16)], accs=("f",), rows=S, width=HEAD, ncols=nh, ts=512)

    dqb, dgq = qk_bwd("q_norm_rope_bwd", dqr, C_QB, GQ_HEADS, p["q_g"])
    dkb, dgk = qk_bwd("k_norm_rope_bwd", dkr, C_KB, GKV_HEADS, p["k_g"])
    grads["q_g"] = dgq[0]
    grads["k_g"] = dgk[0]

    dqc, dkc, dvc = [], [], []
    for g, (win, dil) in enumerate(DIL_GROUPS):
        qg, kg, vg = s["saved_c"][g]
        geo = _dil_geometry(S, dil)
        dq, dk, dv = _attn_bwd(f"dil{g}_bwd", qg, 0, kg, 0, vg, 0, _perm(s["yc"], dil), _perm(dyc, dil),
                               _perm(s["lse_c"], dil), H=DIL_HPG, G=1, extra=_dil_slopes(g, geo["kw"]), **geo)
        dqc.append(_unperm(dq, dil))
        dkc.append(_unperm(dk[DIL_REACH:-DIL_REACH], dil).astype(BF16))
        dvc.append(_unperm(dv[DIL_REACH:-DIL_REACH], dil).astype(BF16))

    dproj = jnp.concatenate(
        [dqa, dka.astype(BF16), dva.astype(BF16), dqb, dkb, dvb.astype(BF16), *dqc, *dkc, *dvc,
         dza, dzb, dzc, dga, dgb, dgc], axis=1)
    if pending is None:
        dh, done = _mm_nt("proj_in_bwd", dproj, p["w_in_t"], out_dtype=F32, w_transposed=True, tk_cap=2560), None
    else:
        dh, *other = _mm_nt("proj_in_bwd", dproj, p["w_in_t"], out_dtype=F32, w_transposed=True, tk_cap=2560,
                            carry=_carry_pair_complete(half))
        done = [_join_halves(m, o) for m, o in zip(half, other)]
    grads["w_in"] = _mm_tn("proj_in_wgrad", s["h"], dproj, blocks=N_CHIPS, ts=1024)

    def pre_bwd(x, dh, g, gy):
        xhat, r = _rms_stats(x)
        return gy + _rms_bwd(xhat, r, dh, g), dh * xhat

    dx, dg_pre = _ew("pre_norm_bwd", pre_bwd, [(s["x"], "t", 0), (dh, "t", 0), (p["pre_g"][None, :], "v", 0),
                                                 (gy, "t", 0)],
                     [(D, F32)], accs=("n",), rows=S, width=D, ncols=1, ts=ts)
    grads["pre_g"] = dg_pre[0]
    return dx, grads, done


BIG = ("w_in", "w_a", "w_b", "w_c", "w_out")
SMALL = ("pre_g", "b_gate", "q_g", "k_g", "rpb", "post_g")


def _reduce_big(gs):
    pair = _pair_sums(gs, _exchange("grad_pair_exchange", _carry_pair_exchange(gs)))
    half = _chip_sums(pair, _exchange("grad_chip_exchange", _carry_chip_exchange(pair)))
    other = _exchange("grad_pair_complete", _carry_pair_complete(half))
    return [_join_halves(m, o) for m, o in zip(half, other)]


def _pack(parts, rows_mult=8):
    flat = jnp.concatenate([a.reshape(-1) for a in parts])
    n = flat.shape[0]
    unit = rows_mult * LANE
    padded = -(-n // unit) * unit
    return jnp.pad(flat, (0, padded - n)).reshape(-1, LANE)


def kernel(x, pre_norm_g, w_in, b_gate, q_norm_g, k_norm_g, rpb, w_branch_a, w_branch_b, w_branch_c, w_out, post_norm_g, loss_target, m_pre_norm_g, m_w_in, m_b_gate, m_q_norm_g, m_k_norm_g, m_rpb, m_w_branch_a, m_w_branch_b, m_w_branch_c, m_w_out, m_post_norm_g, v_pre_norm_g, v_w_in, v_b_gate, v_q_norm_g, v_k_norm_g, v_rpb, v_w_branch_a, v_w_branch_b, v_w_branch_c, v_w_out, v_post_norm_g):
    n_layers = w_in.shape[0]
    S, D = x.shape[1], x.shape[2]
    big_w = dict(w_in=w_in, w_a=w_branch_a, w_b=w_branch_b, w_c=w_branch_c, w_out=w_out)
    big_m = dict(w_in=m_w_in, w_a=m_w_branch_a, w_b=m_w_branch_b, w_c=m_w_branch_c, w_out=m_w_out)
    big_v = dict(w_in=v_w_in, w_a=v_w_branch_a, w_b=v_w_branch_b, w_c=v_w_branch_c, w_out=v_w_out)
    small_w = dict(pre_g=pre_norm_g, b_gate=b_gate, q_g=q_norm_g, k_g=k_norm_g, rpb=rpb, post_g=post_norm_g)
    small_m = dict(pre_g=m_pre_norm_g, b_gate=m_b_gate, q_g=m_q_norm_g, k_g=m_k_norm_g, rpb=m_rpb,
                   post_g=m_post_norm_g)
    small_v = dict(pre_g=v_pre_norm_g, b_gate=v_b_gate, q_g=v_q_norm_g, k_g=v_k_norm_g, rpb=v_rpb,
                   post_g=v_post_norm_g)

    shards = [[big_w[n][l].astype(BF16) for n in BIG] for l in range(n_layers)]
    cos, sin = _rope_tables(S)

    def layer_params(l, full):
        p = dict(zip(BIG, full))
        p["w_in_t"] = jnp.swapaxes(p["w_in"], 1, 2).reshape(-1, D)
        p["w_out"] = p["w_out"].reshape(D, D)
        p.update(pre_g=pre_norm_g[l], b_gate=b_gate[l], q_g=q_norm_g[l], k_g=k_norm_g[l], rpb=rpb[l],
                 post_g=post_norm_g[l])
        return p

    full = _fill_own(_allgather_weights(shards[0]), shards[0])
    act = x[0]
    saved, params = [], []
    for l in range(n_layers):
        params.append(layer_params(l, full))
        act, s, full = _layer_fwd(act, params[l], cos, sin, shards[l + 1] if l + 1 < n_layers else None)
        saved.append(s)

    dy, loss_cols = _ew("loss", lambda y, t: ((y - t) * (1.0 / D), (y - t) * (y - t)),
                        [(act, "t", 0), (loss_target[0], "t", 0)], [(D, F32)], accs=("n",),
                        rows=S, width=D, ncols=1, ts=256)

    big_g = {n: [None] * n_layers for n in BIG}
    small_g = {n: [None] * n_layers for n in SMALL}
    pending = None
    for l in reversed(range(n_layers)):
        dy, grads, done = _layer_bwd(dy, params[l], saved[l], cos, sin, pending)
        if done is not None:
            for n, r in zip(BIG, done):
                big_g[n][l + 1] = r
        pending = [grads[n] for n in BIG]
        for n in SMALL:
            small_g[n][l] = grads[n]
    for n, r in zip(BIG, _reduce_big(pending)):
        big_g[n][0] = r

    loss_rows = -(-D // (8 * LANE)) * 8
    parts = [_pack([loss_cols[0]])] + [_pack([jnp.stack(small_g[n]) for n in SMALL])]
    tot, loss = _allreduce_small(jnp.concatenate(parts), loss_rows, 0.5 / D)
    wvec = jnp.concatenate([jnp.zeros((loss_rows, LANE), F32), _pack([small_w[n] for n in SMALL])])
    mvec = jnp.concatenate([jnp.zeros((loss_rows, LANE), F32), _pack([small_m[n] for n in SMALL])])
    vvec = jnp.concatenate([jnp.ones((loss_rows, LANE), F32), _pack([small_v[n] for n in SMALL])])
    small_out = [tot] + _adamw("adamw_small", wvec, tot, mvec, vvec)

    def unpack(vec):
        flat = vec[loss_rows:].reshape(-1)
        res, pos = {}, 0
        for n in SMALL:
            size = int(np.prod(small_w[n].shape))
            res[n] = flat[pos:pos + size].reshape(small_w[n].shape)
            pos += size
        return res

    small_res = [unpack(v) for v in small_out]

    big_res = [{}, {}, {}, {}]
    for n in BIG:
        g = jnp.stack(big_g[n])
        big_res[0][n] = g
        for k, val in enumerate(_adamw("adamw_" + n, big_w[n], g, big_m[n], big_v[n])):
            big_res[k + 1][n] = val

    order = (("pre_g", small_res), ("w_in", big_res), ("b_gate", small_res), ("q_g", small_res),
             ("k_g", small_res), ("rpb", small_res), ("w_a", big_res), ("w_b", big_res), ("w_c", big_res),
             ("w_out", big_res), ("post_g", small_res))
    outs = [loss[0, 0], dy[None]]
    for k in range(4):
        outs.extend(src[k][n] for n, src in order)
    return tuple(outs)
```

```python
import functools

import numpy as np
import jax
import jax.numpy as jnp
from jax import lax
from jax.experimental import pallas as pl
from jax.experimental.pallas import tpu as pltpu

F32 = jnp.float32
BF16 = jnp.bfloat16
MESH = pl.DeviceIdType.MESH

HEAD = 128
GRID_W = 64
EPS = 1e-6
NEG = -1e30
NA_HEADS = 8
NA_WIN_R = 8
NA_WIN_C = 16
GQ_HEADS = 8
GKV_HEADS = 2
ROPE_BASE = 10000.0
DIL_GROUPS = ((128, 1), (512, 4), (2048, 16))
DIL_REACH = 64
DIL_HPG = 4
DIL_HEADS = 12
ALIBI_MAX_EXP = 8.0
ADAM_LR = 0.001
ADAM_B1 = 0.9
ADAM_B2 = 0.999
ADAM_EPS = 1e-08
ADAM_WD = 0.01
ADAM_STEP = 10

C_QA, C_KA, C_VA = 0, 8, 16
C_QB, C_KB, C_VB = 24, 32, 34
C_QC, C_KC, C_VC = 36, 48, 60
C_ZA, C_ZB, C_ZC = 72, 80, 88
C_GATE = 92

NA_QR = 4
NA_KR = NA_QR + NA_WIN_R
NA_VARIANTS = 3

GQA_FWD_TILE = (1024, 1024)
GQA_BWD_TILE = (1024, 1024)

VMEM_LIMIT_BYTES = 48 * 1024 * 1024
LANE = 128
N_CHIPS = 4
N_DEV = 8


def _params(sem=None):
    kw = dict(vmem_limit_bytes=VMEM_LIMIT_BYTES)
    if sem is not None:
        kw["dimension_semantics"] = sem
    return pltpu.CompilerParams(**kw)


def _tile(n, cap):
    units = n // LANE
    assert units * LANE == n, n
    best = 1
    for d in range(1, units + 1):
        if units % d == 0 and d * LANE <= cap:
            best = d
    return best * LANE


def _ew(name, fn, ins, outs, accs=(), *, rows, width, ncols, ts, carry=None):
    ts = min(ts, rows)
    assert rows % ts == 0
    nr = rows // ts
    ni, no = len(ins), len(outs)

    def imap(mode, off):
        if mode == "t":
            return lambda n, i: (i, off + n)
        if mode == "tf":
            return lambda n, i: (i, off)
        if mode == "v":
            return lambda n, i: (0, off + n)
        return lambda n, i: (0, off)

    in_specs = [pl.BlockSpec((ts if m in ("t", "tf") else 1, width), imap(m, o)) for (_, m, o) in ins]
    out_specs = [pl.BlockSpec((ts, width), lambda n, i: (i, n)) for _ in outs]
    out_shape = [jax.ShapeDtypeStruct((rows, c), d) for (c, d) in outs]
    for a in accs:
        if a == "n":
            out_specs.append(pl.BlockSpec((8, width), lambda n, i: (0, n)))
            out_shape.append(jax.ShapeDtypeStruct((8, width * ncols), F32))
        else:
            out_specs.append(pl.BlockSpec((8, width), lambda n, i: (0, 0)))
            out_shape.append(jax.ShapeDtypeStruct((8, width), F32))

    def body(*refs):
        res = fn(*[r[...] for r in refs[:ni]])
        if not isinstance(res, tuple):
            res = (res,)
        for r, val in zip(refs[ni:ni + no], res[:no]):
            r[...] = val.astype(r.dtype)
        n = pl.program_id(0)
        i = pl.program_id(1)
        for kind, r, val in zip(accs, refs[ni + no:], res[no:]):
            first = (i == 0) if kind == "n" else jnp.logical_and(i == 0, n == 0)

            @pl.when(first)
            def _(r=r):
                r[...] = jnp.zeros(r.shape, r.dtype)

            r[...] += jnp.broadcast_to(jnp.sum(val.astype(F32), axis=0, keepdims=True), r.shape)

    grid = (ncols, nr)
    body, c_in, c_in_specs, c_out, c_out_specs, c_scratch = _carried(body, ni, no + len(accs), grid, carry)
    res = pl.pallas_call(
        body, name=name, grid=grid, in_specs=in_specs + c_in_specs, out_specs=out_specs + c_out_specs,
        out_shape=out_shape + c_out, scratch_shapes=c_scratch,
        compiler_params=_params(("arbitrary", "arbitrary")),
    )(*[a for (a, _, _) in ins], *c_in)
    return res


def _mm_nn(name, a, w, *, out_dtype, tm=512, tn_cap=1024):
    M, K = a.shape
    blocked = w.ndim == 3
    NB = w.shape[-1]
    N = NB * (w.shape[0] if blocked else 1)
    tm = min(tm, M)
    tn = _tile(NB, tn_cap)
    per = NB // tn
    if blocked:
        w_spec = pl.BlockSpec((None, K, tn), lambda i, n: (n // per, 0, n % per))
    else:
        w_spec = pl.BlockSpec((K, tn), lambda i, n: (0, n))

    def body(a_ref, w_ref, o_ref):
        o_ref[...] = jnp.dot(a_ref[...], w_ref[...], preferred_element_type=F32).astype(o_ref.dtype)

    return pl.pallas_call(
        body, name=name, grid=(M // tm, N // tn),
        in_specs=[pl.BlockSpec((tm, K), lambda i, n: (i, 0)), w_spec],
        out_specs=pl.BlockSpec((tm, tn), lambda i, n: (i, n)),
        out_shape=jax.ShapeDtypeStruct((M, N), out_dtype),
        compiler_params=_params(("arbitrary", "arbitrary")),
    )(a, w)


def _mm_deep(name, a, w, *, out_dtype, tm=512, tk_cap=1024, carry=None):
    M, N = a.shape
    K = w.shape[1]
    assert w.shape[0] == N
    tm = min(tm, M)
    tk = _tile(N, tk_cap)
    nk = N // tk
    w_spec = pl.BlockSpec((tk, K), lambda i, k: (k, 0))

    def body(a_ref, w_ref, o_ref, acc_ref):
        k = pl.program_id(1)

        @pl.when(k == 0)
        def _():
            acc_ref[...] = jnp.zeros(acc_ref.shape, F32)

        acc_ref[...] += jnp.dot(a_ref[...], w_ref[...], preferred_element_type=F32)

        @pl.when(k == nk - 1)
        def _():
            o_ref[...] = acc_ref[...].astype(o_ref.dtype)

    grid = (M // tm, nk)
    body, c_in, c_in_specs, c_out, c_out_specs, c_scratch = _carried(body, 2, 1, grid, carry)
    res = pl.pallas_call(
        body, name=name, grid=grid,
        in_specs=[pl.BlockSpec((tm, tk), lambda i, k: (i, k)), w_spec] + c_in_specs,
        out_specs=[pl.BlockSpec((tm, K), lambda i, k: (i, 0))] + c_out_specs,
        out_shape=[jax.ShapeDtypeStruct((M, K), out_dtype)] + c_out,
        scratch_shapes=[pltpu.VMEM((tm, K), F32)] + c_scratch,
        compiler_params=_params(("arbitrary", "arbitrary")),
    )(a, w, *c_in)
    return res[0] if carry is None else res


def _mm_tn(name, a, b, *, blocks, ts=512, tn_cap=1024):
    S, K = a.shape
    N = b.shape[1]
    NB = N // blocks
    ts = min(ts, S)
    tn = _tile(NB, tn_cap)
    per = NB // tn
    ns = S // ts

    def body(a_ref, b_ref, o_ref):
        s = pl.program_id(1)

        @pl.when(s == 0)
        def _():
            o_ref[...] = jnp.zeros(o_ref.shape, F32)

        o_ref[...] += lax.dot_general(a_ref[...], b_ref[...], (((0,), (0,)), ((), ())),
                                      preferred_element_type=F32)

    return pl.pallas_call(
        body, name=name, grid=(N // tn, ns),
        in_specs=[pl.BlockSpec((ts, K), lambda n, s: (s, 0)), pl.BlockSpec((ts, tn), lambda n, s: (s, n))],
        out_specs=pl.BlockSpec((None, K, tn), lambda n, s: (n // per, 0, n % per)),
        out_shape=jax.ShapeDtypeStruct((blocks, K, NB), F32),
        compiler_params=_params(("arbitrary", "arbitrary")),
    )(a, b)


def _aligned(off):
    return off if isinstance(off, int) else pl.multiple_of(off, LANE)


def _attn_bias(s, mode, extra_ref, i, tq, kc, dil, L):
    if mode == "table":
        return s + extra_ref[...]
    if mode == "dil":
        r = lax.broadcasted_iota(jnp.int32, (tq, kc), 0)
        tk = lax.broadcasted_iota(jnp.int32, (tq, kc), 1) - DIL_REACH
        dist = jnp.abs(r - tk)
        kin = (i * tq) % L + tk
        valid = (dist <= DIL_REACH) & (kin >= 0) & (kin < L)
        s = s - extra_ref[0:1, :] * (dil * dist).astype(F32)
        return jnp.where(valid, s, NEG)
    return s


def _attn_fwd(name, q, qo, k, ko, v, vo, *, H, G, S, tq, kw, kc, kstart, mode=None, extra=None,
              var=None, dil=1, L=1, out_dtype=BF16, prescaled=False, carry=None):
    nq = S // tq
    nch = kw // kc
    Sk = k.shape[0]
    scale = HEAD ** -0.5
    has_extra = mode is not None
    exp = jnp.exp2 if prescaled else jnp.exp
    log = jnp.log2 if prescaled else jnp.log

    def body(*refs):
        if has_extra:
            q_ref, k_ref, v_ref, e_ref, o_ref, l_ref = refs
        else:
            q_ref, k_ref, v_ref, o_ref, l_ref = refs
            e_ref = None
        i = pl.program_id(1)
        qb = q_ref[...]
        start = kstart(i)

        def chunk(j, carry):
            m, l, acc = carry
            off = _aligned(start + j * kc)
            kb = k_ref[pl.ds(off, kc), :]
            vb = v_ref[pl.ds(off, kc), :]
            s = lax.dot_general(qb, kb, (((1,), (1,)), ((), ())), preferred_element_type=F32)
            if not prescaled:
                s = s * scale
            s = _attn_bias(s, mode, e_ref, i, tq, kc, dil, L)
            mn = jnp.maximum(m, jnp.max(s, axis=1, keepdims=True))
            p = exp(s - mn)
            a = exp(m - mn)
            l = a * l + jnp.sum(p, axis=1, keepdims=True)
            acc = a * acc + jnp.dot(p.astype(BF16), vb, preferred_element_type=F32)
            return mn, l, acc

        init = (jnp.full((tq, 1), -3.0e38, F32), jnp.zeros((tq, 1), F32), jnp.zeros((tq, HEAD), F32))
        if nch == 1:
            m, l, acc = chunk(0, init)
        else:
            m, l, acc = lax.fori_loop(0, nch, chunk, init)
        o_ref[...] = (acc / l).astype(o_ref.dtype)
        l_ref[...] = jnp.broadcast_to(m + log(l), (tq, HEAD))

    in_specs = [
        pl.BlockSpec((tq, HEAD), lambda h, i: (i, qo + h)),
        pl.BlockSpec((Sk, HEAD), lambda h, i: (0, ko + h // G)),
        pl.BlockSpec((Sk, HEAD), lambda h, i: (0, vo + h // G)),
    ]
    args = [q, k, v]
    if mode == "table":
        in_specs.append(pl.BlockSpec((None, None, tq, kw), lambda h, i: (var(i), h, 0, 0)))
        args.append(extra)
    elif mode == "dil":
        in_specs.append(pl.BlockSpec((None, 8, kw), lambda h, i: (h, 0, 0)))
        args.append(extra)
    grid = (H, nq)
    body, c_in, c_in_specs, c_out, c_out_specs, c_scratch = _carried(body, len(args), 2, grid, carry)
    return pl.pallas_call(
        body, name=name, grid=grid, in_specs=in_specs + c_in_specs,
        out_specs=[pl.BlockSpec((tq, HEAD), lambda h, i: (i, h)),
                   pl.BlockSpec((tq, HEAD), lambda h, i: (i, h))] + c_out_specs,
        out_shape=[jax.ShapeDtypeStruct((S, H * HEAD), out_dtype),
                   jax.ShapeDtypeStruct((S, H * HEAD), F32)] + c_out,
        scratch_shapes=c_scratch,
        compiler_params=_params(("arbitrary", "arbitrary")),
    )(*args, *c_in)


def _attn_bwd(name, q, qo, k, ko, v, vo, o, do, lse, *, H, G, S, tq, kw, kc, kstart, mode=None, extra=None,
              var=None, dil=1, L=1, prescaled=False, carry=None):
    nq = S // tq
    nch = kw // kc
    Sk = k.shape[0]
    HK = H // G
    scale = HEAD ** -0.5
    dk_mult = float(np.log(2.0)) if prescaled else scale
    exp = jnp.exp2 if prescaled else jnp.exp
    has_extra = mode is not None
    table = mode == "table"
    if table:
        assert nq >= NA_VARIANTS and nch == 1

    def body(*refs):
        refs = list(refs)
        q_ref, k_ref, v_ref, o_ref, do_ref, l_ref = refs[:6]
        e_ref = refs[6] if has_extra else None
        outs = refs[7:] if has_extra else refs[6:]
        dq_ref, dk_ref, dv_ref = outs[:3]
        h = pl.program_id(0)
        i = pl.program_id(1)

        @pl.when(jnp.logical_and(h % G == 0, i == 0))
        def _():
            dk_ref[...] = jnp.zeros(dk_ref.shape, F32)
            dv_ref[...] = jnp.zeros(dv_ref.shape, F32)

        if table:
            db_ref = outs[3]

            @pl.when((i == 0) | (i == 1) | (i == nq - 1))
            def _():
                db_ref[...] = jnp.zeros(db_ref.shape, F32)

        qb = q_ref[...]
        dob = do_ref[...]
        delta = jnp.sum(o_ref[...].astype(F32) * dob.astype(F32), axis=1, keepdims=True)
        lse_b = l_ref[:, 0:1]
        start = kstart(i)

        def chunk(j, dq):
            off = _aligned(start + j * kc)
            kb = k_ref[pl.ds(off, kc), :]
            vb = v_ref[pl.ds(off, kc), :]
            s = lax.dot_general(qb, kb, (((1,), (1,)), ((), ())), preferred_element_type=F32)
            if not prescaled:
                s = s * scale
            s = _attn_bias(s, mode, e_ref, i, tq, kc, dil, L)
            p = exp(s - lse_b)
            dp = lax.dot_general(dob, vb, (((1,), (1,)), ((), ())), preferred_element_type=F32)
            ds = p * (dp - delta)
            dsb = ds.astype(BF16)
            dk_ref[pl.ds(off, kc), :] += dk_mult * lax.dot_general(
                dsb, qb, (((0,), (0,)), ((), ())), preferred_element_type=F32)
            dv_ref[pl.ds(off, kc), :] += lax.dot_general(
                p.astype(BF16), dob, (((0,), (0,)), ((), ())), preferred_element_type=F32)
            if table:
                db_ref[...] += ds
            return dq + jnp.dot(dsb, kb, preferred_element_type=F32)

        dq0 = jnp.zeros((tq, HEAD), F32)
        dq = chunk(0, dq0) if nch == 1 else lax.fori_loop(0, nch, chunk, dq0)
        dq_ref[...] = (dq * scale).astype(dq_ref.dtype)

    in_specs = [
        pl.BlockSpec((tq, HEAD), lambda h, i: (i, qo + h)),
        pl.BlockSpec((Sk, HEAD), lambda h, i: (0, ko + h // G)),
        pl.BlockSpec((Sk, HEAD), lambda h, i: (0, vo + h // G)),
        pl.BlockSpec((tq, HEAD), lambda h, i: (i, h)),
        pl.BlockSpec((tq, HEAD), lambda h, i: (i, h)),
        pl.BlockSpec((tq, HEAD), lambda h, i: (i, h)),
    ]
    args = [q, k, v, o, do, lse]
    out_specs = [
        pl.BlockSpec((tq, HEAD), lambda h, i: (i, h)),
        pl.BlockSpec((Sk, HEAD), lambda h, i: (0, h // G)),
        pl.BlockSpec((Sk, HEAD), lambda h, i: (0, h // G)),
    ]
    out_shape = [
        jax.ShapeDtypeStruct((S, H * HEAD), BF16),
        jax.ShapeDtypeStruct((Sk, HK * HEAD), F32),
        jax.ShapeDtypeStruct((Sk, HK * HEAD), F32),
    ]
    if table:
        in_specs.append(pl.BlockSpec((None, None, tq, kw), lambda h, i: (var(i), h, 0, 0)))
        args.append(extra)
        out_specs.append(pl.BlockSpec((None, None, tq, kw), lambda h, i: (var(i), h, 0, 0)))
        out_shape.append(jax.ShapeDtypeStruct(extra.shape, F32))
    elif mode == "dil":
        in_specs.append(pl.BlockSpec((None, 8, kw), lambda h, i: (h, 0, 0)))
        args.append(extra)
    grid = (H, nq)
    body, c_in, c_in_specs, c_out, c_out_specs, c_scratch = _carried(body, len(args), len(out_shape), grid, carry)
    return pl.pallas_call(
        body, name=name, grid=grid, in_specs=in_specs + c_in_specs, out_specs=out_specs + c_out_specs,
        out_shape=out_shape + c_out, scratch_shapes=c_scratch,
        compiler_params=_params(("arbitrary", "arbitrary")),
    )(*args, *c_in)


def _na_constants():
    pairs = NA_VARIANTS * NA_QR * NA_KR
    sel = np.zeros((pairs, 16), np.float32)
    row_ok = np.zeros((pairs, 1), np.float32)
    for var in range(NA_VARIANTS):
        for qr in range(NA_QR):
            for kr in range(NA_KR):
                p = (var * NA_QR + qr) * NA_KR + kr
                first_key = (0, qr, NA_QR)[var]
                dr = kr - qr + (NA_WIN_R - 1) - (0, NA_QR, 2 * NA_QR)[var]
                if first_key <= kr < first_key + NA_WIN_R:
                    assert 0 <= dr < 2 * NA_WIN_R - 1
                    sel[p, dr] = 1.0
                    row_ok[p, 0] = 1.0
    j = np.arange(GRID_W)[:, None]
    c = np.arange(GRID_W)[None, :]
    cs = np.clip(j - NA_WIN_C // 2, 0, GRID_W - NA_WIN_C)
    col_ok = ((c >= cs) & (c < cs + NA_WIN_C)).astype(np.float32).reshape(1, GRID_W * GRID_W)
    dc = np.clip(c - j + NA_WIN_C - 1, 0, 2 * NA_WIN_C - 2).reshape(-1)
    toe = np.zeros((32, GRID_W * GRID_W), np.float32)
    toe[dc, np.arange(GRID_W * GRID_W)] = 1.0
    return sel, row_ok, col_ok, toe


def _split3(x):
    hi = x.astype(BF16)
    r1 = x - hi.astype(F32)
    mid = r1.astype(BF16)
    lo = (r1 - mid.astype(F32)).astype(BF16)
    return hi, mid, lo


def _dot_sel_left(onehot_bf16, x):
    return sum(jnp.dot(onehot_bf16, t, preferred_element_type=F32) for t in _split3(x))


def _dot_sel_right(x, onehot_bf16):
    return sum(jnp.dot(t, onehot_bf16, preferred_element_type=F32) for t in _split3(x))


def _na_bias_build(rpb_l):
    sel, row_ok, col_ok, toe = _na_constants()
    pairs = sel.shape[0]
    ww = GRID_W * GRID_W
    t = jnp.pad(rpb_l, ((0, 0), (0, 1), (0, 1)))

    def body(t_ref, sel_ref, toe_ref, rok_ref, cok_ref, o_ref):
        rows = _dot_sel_right(t_ref[...], toe_ref[...])
        val = _dot_sel_left(sel_ref[...], rows)
        ok = rok_ref[...] * cok_ref[...]
        o_ref[...] = jnp.where(ok > 0.5, val, NEG)

    full = lambda shape: pl.BlockSpec(shape, lambda h: (0,) * len(shape))
    out = pl.pallas_call(
        body, name="na_bias_build", grid=(NA_HEADS,),
        in_specs=[pl.BlockSpec((None, 16, 32), lambda h: (h, 0, 0)), full((pairs, 16)), full((32, ww)),
                  full((pairs, 1)), full((1, ww))],
        out_specs=pl.BlockSpec((None, pairs, ww), lambda h: (h, 0, 0)),
        out_shape=jax.ShapeDtypeStruct((NA_HEADS, pairs, ww), F32),
        compiler_params=_params(("arbitrary",)),
    )(t, jnp.asarray(sel, BF16), jnp.asarray(toe, BF16), jnp.asarray(row_ok), jnp.asarray(col_ok))
    out = out.reshape(NA_HEADS, NA_VARIANTS, NA_QR, NA_KR, GRID_W, GRID_W)
    out = out.transpose(1, 0, 2, 4, 3, 5)
    return out.reshape(NA_VARIANTS, NA_HEADS, NA_QR * GRID_W, NA_KR * GRID_W)


def _na_bias_grad(dbias):
    sel, _, _, toe = _na_constants()
    pairs = sel.shape[0]
    ww = GRID_W * GRID_W
    d = dbias.reshape(NA_VARIANTS, NA_HEADS, NA_QR, GRID_W, NA_KR, GRID_W).transpose(1, 0, 2, 4, 3, 5)
    d = d.reshape(NA_HEADS, pairs, ww)

    def body(d_ref, selt_ref, toet_ref, o_ref):
        rows = _dot_sel_left(selt_ref[...], d_ref[...])
        o_ref[...] = _dot_sel_right(rows, toet_ref[...])

    full = lambda shape: pl.BlockSpec(shape, lambda h: (0,) * len(shape))
    out = pl.pallas_call(
        body, name="na_bias_grad", grid=(NA_HEADS,),
        in_specs=[pl.BlockSpec((None, pairs, ww), lambda h: (h, 0, 0)), full((16, pairs)), full((ww, 32))],
        out_specs=pl.BlockSpec((None, 16, 32), lambda h: (h, 0, 0)),
        out_shape=jax.ShapeDtypeStruct((NA_HEADS, 16, 32), F32),
        compiler_params=_params(("arbitrary",)),
    )(d, jnp.asarray(sel.T, BF16), jnp.asarray(toe.T, BF16))
    return out[:, :2 * NA_WIN_R - 1, :2 * NA_WIN_C - 1]


def _sigmoid(x):
    return 1.0 / (1.0 + jnp.exp(-x))


def _swap32(y):
    lane = lax.broadcasted_iota(jnp.int32, y.shape, 1)
    up = pltpu.roll(y, HEAD - 32, 1)
    down = pltpu.roll(y, 32, 1)
    return jnp.where((lane // 32) % 2 == 0, up, down)


def _rope_tables(S):
    t = jnp.arange(S, dtype=jnp.int32)
    freqs = ROPE_BASE ** (-jnp.arange(32, dtype=F32) / 32)
    ar = (t // GRID_W).astype(F32)[:, None] * freqs[None, :]
    ac = (t % GRID_W).astype(F32)[:, None] * freqs[None, :]
    cos = jnp.concatenate([jnp.cos(ar), jnp.cos(ar), jnp.cos(ac), jnp.cos(ac)], axis=1)
    sin = jnp.concatenate([-jnp.sin(ar), jnp.sin(ar), -jnp.sin(ac), jnp.sin(ac)], axis=1)
    return cos, sin


def _rms_stats(x):
    x = x.astype(F32)
    r = lax.rsqrt(jnp.mean(x * x, axis=-1, keepdims=True) + EPS)
    return x * r, r


def _rms_bwd(xhat, r, dy, g):
    dxh = dy * g
    return r * (dxh - xhat * jnp.mean(dxh * xhat, axis=-1, keepdims=True))


def _qk_fwd(x, g, cos, sin):
    xhat, _ = _rms_stats(x)
    y = xhat * g
    return y * cos + _swap32(y) * sin


def _qk_bwd(dyr, x, g, cos, sin):
    dyr = dyr.astype(F32)
    dy = dyr * cos + _swap32(dyr * sin)
    xhat, r = _rms_stats(x)
    return _rms_bwd(xhat, r, dy, g), dy * xhat


def _perm(a, dil):
    if dil == 1:
        return a
    S = a.shape[0]
    return a.reshape(S // dil, dil, *a.shape[1:]).swapaxes(0, 1).reshape(a.shape)


def _unperm(a, dil):
    if dil == 1:
        return a
    S = a.shape[0]
    return a.reshape(dil, S // dil, *a.shape[1:]).swapaxes(0, 1).reshape(a.shape)


def _place():
    x, y, c = lax.axis_index("x"), lax.axis_index("y"), lax.axis_index("c")
    chips = [(1 - x, y), (x, 1 - y), (1 - x, 1 - y)]
    return x, y, c, chips


def _remote(src, dst, send, recv, dev):
    return pltpu.make_async_remote_copy(src_ref=src, dst_ref=dst, send_sem=send, recv_sem=recv,
                                        device_id=dev, device_id_type=MESH)


ANY = pl.BlockSpec(memory_space=pl.ANY)


class _Xfer:
    def __init__(self, send, recv):
        self.send, self.recv = send, recv

    def start(self):
        self.send().start()

    def wait(self):
        self.recv().wait_recv()
        self.send().wait_send()


def _sem_shapes(nsem):
    return [pltpu.SemaphoreType.DMA((nsem,))] * 2


def _carried(body, n_in, n_out, grid, carry):
    if carry is None:
        return body, [], [], [], [], []
    ins, out_shapes, plan, nsem = carry
    ci, co = len(ins), len(out_shapes)

    def new_body(*refs):
        a, cin = refs[:n_in], refs[n_in:n_in + ci]
        b = refs[n_in + ci:n_in + ci + n_out]
        cout = refs[n_in + ci + n_out:n_in + ci + n_out + co]
        rest = refs[n_in + ci + n_out + co:]
        scratch, sems = rest[:-2], rest[-2:]
        ids = [pl.program_id(d) for d in range(len(grid))]
        first = functools.reduce(jnp.logical_and, [i == 0 for i in ids])
        last = functools.reduce(jnp.logical_and, [i == g - 1 for i, g in zip(ids, grid)])

        @pl.when(first)
        def _():
            for t in plan(cin, cout, *sems):
                t.start()

        body(*a, *b, *scratch)

        @pl.when(last)
        def _():
            for t in plan(cin, cout, *sems):
                t.wait()

    return new_body, list(ins), [ANY] * ci, list(out_shapes), [ANY] * co, _sem_shapes(nsem)


def _exchange(name, carry):
    ins, out_shapes, plan, nsem = carry
    n = len(ins)

    def body(*refs):
        items = plan(refs[:n], refs[n:n + len(out_shapes)], *refs[-2:])
        for t in items:
            t.start()
        for t in items:
            t.wait()

    return pl.pallas_call(body, name=name, in_specs=[ANY] * n, out_specs=[ANY] * len(out_shapes),
                          out_shape=list(out_shapes), scratch_shapes=_sem_shapes(nsem))(*ins)


def _plan_gather(ins, outs, send, recv):
    x, y, c, chips = _place()
    me = 2 * x + y
    items = []
    for w, (src, dst) in enumerate(zip(ins, outs)):
        for r, (px, py) in enumerate(chips):
            k = 3 * w + r
            to = functools.partial(_remote, src, dst.at[me], send.at[k], recv.at[k], (px, py, c))
            frm = functools.partial(_remote, src, dst.at[2 * px + py], send.at[k], recv.at[k], (px, py, c))
            items.append(_Xfer(send=to, recv=frm))
    return items


def _carry_gather(shards):
    shapes = [jax.ShapeDtypeStruct((N_CHIPS,) + a.shape, a.dtype) for a in shards]
    return (shards, shapes, _plan_gather, 3 * len(shards))


def _plan_pair_exchange(ins, outs, send, recv):
    x, y, c, _ = _place()
    items = []
    for w, (src, dst) in enumerate(zip(ins, outs)):
        hr = src.shape[1] // 2
        theirs = pl.ds(pl.multiple_of((1 - c) * hr, 8), hr)
        cp = functools.partial(_remote, src.at[:, theirs, :], dst, send.at[w], recv.at[w], (x, y, 1 - c))
        items.append(_Xfer(send=cp, recv=cp))
    return items


def _carry_pair_exchange(gs):
    shapes = [jax.ShapeDtypeStruct((a.shape[0], a.shape[1] // 2, a.shape[2]), a.dtype) for a in gs]
    return (gs, shapes, _plan_pair_exchange, len(gs))


def _plan_chip_exchange(ins, outs, send, recv, base=0):
    x, y, c, chips = _place()
    items = []
    for w, (src, dst) in enumerate(zip(ins, outs)):
        for r, (px, py) in enumerate(chips):
            k = base + 3 * w + r
            cp = functools.partial(_remote, src.at[2 * px + py], dst.at[r], send.at[k], recv.at[k], (px, py, c))
            items.append(_Xfer(send=cp, recv=cp))
    return items


def _carry_chip_exchange(qs):
    shapes = [jax.ShapeDtypeStruct((3,) + a.shape[1:], a.dtype) for a in qs]
    return (qs, shapes, _plan_chip_exchange, 3 * len(qs))


def _plan_pair_complete(ins, outs, send, recv):
    x, y, c, _ = _place()
    items = []
    for w, (src, dst) in enumerate(zip(ins, outs)):
        cp = functools.partial(_remote, src, dst, send.at[w], recv.at[w], (x, y, 1 - c))
        items.append(_Xfer(send=cp, recv=cp))
    return items


def _carry_pair_complete(rs):
    shapes = [jax.ShapeDtypeStruct(a.shape, a.dtype) for a in rs]
    return (rs, shapes, _plan_pair_complete, len(rs))


def _carry_complete_and_chip_exchange(rs, qs):
    ins_a, shapes_a, plan_a, n_a = _carry_pair_complete(rs)
    ins_b, shapes_b, plan_b, n_b = _carry_chip_exchange(qs)

    def plan(ins, outs, send, recv):
        return (plan_a(ins[:len(ins_a)], outs[:len(shapes_a)], send, recv)
                + plan_b(ins[len(ins_a):], outs[len(shapes_a):], send, recv, base=n_a))

    return (list(ins_a) + list(ins_b), shapes_a + shapes_b, plan, n_a + n_b)


def _join_halves(mine, other):
    first = lax.axis_index("c") == 0
    return jnp.concatenate([jnp.where(first, mine, other), jnp.where(first, other, mine)])


def _allgather_weights(ws):
    n = len(ws)

    def body(*refs):
        ins, outs = refs[:n], refs[n:2 * n]
        send, recv = refs[2 * n:]
        x, y, c, chips = _place()
        me = 2 * x + y
        sib = (x, y, 1 - c)
        sends = []
        for w in range(n):
            hr = ins[w].shape[0] // 2
            mine = pl.ds(pl.multiple_of(c * hr, 16), hr)
            for r, (px, py) in enumerate(chips):
                cp = _remote(ins[w].at[mine, :], outs[w].at[me, mine, :], send.at[w, r], recv.at[w, r], (px, py, c))
                cp.start()
                sends.append(cp)
        for w in range(n):
            hr = ins[w].shape[0] // 2
            mine = pl.ds(pl.multiple_of(c * hr, 16), hr)
            for r, (px, py) in enumerate(chips):
                blk = outs[w].at[2 * px + py, mine, :]
                _remote(blk, blk, send.at[w, r], recv.at[w, r], (px, py, c)).wait_recv()
                cp = _remote(blk, blk, send.at[w, 3 + r], recv.at[w, 3 + r], sib)
                cp.start()
                sends.append(cp)
        for w in range(n):
            hr = ins[w].shape[0] // 2
            other = pl.ds(pl.multiple_of((1 - c) * hr, 16), hr)
            for r, (px, py) in enumerate(chips):
                blk = outs[w].at[2 * px + py, other, :]
                _remote(blk, blk, send.at[w, 3 + r], recv.at[w, 3 + r], sib).wait_recv()
        for cp in sends:
            cp.wait_send()

    out_shape = [jax.ShapeDtypeStruct((N_CHIPS,) + a.shape, a.dtype) for a in ws]
    return pl.pallas_call(
        body, name="allgather_weights", in_specs=[ANY] * n, out_specs=[ANY] * n, out_shape=out_shape,
        scratch_shapes=[pltpu.SemaphoreType.DMA((n, 6)), pltpu.SemaphoreType.DMA((n, 6))],
    )(*ws)


def _allreduce_small(vec, loss_rows, loss_scale):
    rows = vec.shape[0]

    def body(v_ref, tot_ref, loss_ref, gat_ref, send, recv):
        x, y, c, _ = _place()
        me = 4 * x + 2 * y + c
        gat_ref[me] = v_ref[...]
        peers = []
        for r in range(1, N_DEV):
            px = 1 - x if r & 4 else x
            py = 1 - y if r & 2 else y
            pc = 1 - c if r & 1 else c
            peers.append((px, py, pc))
            _remote(v_ref, gat_ref.at[me], send.at[r - 1], recv.at[r - 1], (px, py, pc)).start()
        for r, (px, py, pc) in enumerate(peers):
            _remote(v_ref, gat_ref.at[4 * px + 2 * py + pc], send.at[r], recv.at[r], (px, py, pc)).wait_recv()
        for r, (px, py, pc) in enumerate(peers):
            _remote(v_ref, gat_ref.at[me], send.at[r], recv.at[r], (px, py, pc)).wait_send()
        tot = gat_ref[0]
        for d in range(1, N_DEV):
            tot = tot + gat_ref[d]
        tot_ref[...] = tot
        loss_ref[...] = jnp.full(loss_ref.shape, loss_scale, F32) * jnp.sum(tot[:loss_rows])

    vm = pl.BlockSpec(memory_space=pltpu.VMEM)
    tot, loss = pl.pallas_call(
        body, name="allreduce_small", in_specs=[vm], out_specs=[vm, vm],
        out_shape=[jax.ShapeDtypeStruct((rows, LANE), F32), jax.ShapeDtypeStruct((8, LANE), F32)],
        scratch_shapes=[pltpu.VMEM((N_DEV, rows, LANE), F32), pltpu.SemaphoreType.DMA((N_DEV - 1,)),
                        pltpu.SemaphoreType.DMA((N_DEV - 1,))],
        compiler_params=_params(),
    )(vec)
    return tot, loss


def _add_rows(name, terms):
    shape = terms[0].shape
    C = shape[-1]
    flat = [t.reshape(-1, C) for t in terms]
    rows = flat[0].shape[0]
    width = _tile(C, 1024)

    def fn(*vals):
        tot = vals[0]
        for val in vals[1:]:
            tot = tot + val
        return tot

    (out,) = _ew(name, fn, [(t, "t", 0) for t in flat], [(C, F32)], rows=rows, width=width, ncols=C // width,
                 ts=512)
    return out.reshape(shape)


def _adamw(name, w, g, m, v):
    shape = w.shape
    C = shape[-1]
    flat = [t.reshape(-1, C) for t in (w, g, m, v)]
    rows = flat[0].shape[0]
    width = _tile(C, 1024)

    def fn(w, g, m, v):
        m2 = ADAM_B1 * m + (1.0 - ADAM_B1) * g
        v2 = ADAM_B2 * v + (1.0 - ADAM_B2) * (g * g)
        m_hat = m2 / (1.0 - ADAM_B1 ** ADAM_STEP)
        v_hat = v2 / (1.0 - ADAM_B2 ** ADAM_STEP)
        delta = -ADAM_LR * (m_hat / (jnp.sqrt(v_hat) + ADAM_EPS) + ADAM_WD * w)
        return delta, m2, v2

    outs = _ew(name, fn, [(t, "t", 0) for t in flat], [(C, F32)] * 3, rows=rows, width=width,
               ncols=C // width, ts=min(512, rows))
    return [o.reshape(shape) for o in outs]


def _na_geometry(S):
    rows = S // GRID_W
    nq = rows // NA_QR
    tq = NA_QR * GRID_W
    kw = NA_KR * GRID_W

    def kstart(i):
        return jnp.clip(NA_QR * i - NA_WIN_R // 2, 0, rows - NA_KR) * GRID_W

    def var(i):
        return jnp.where(i == 0, 0, jnp.where(i == nq - 1, 2, 1))

    return dict(S=S, tq=tq, kw=kw, kc=kw, kstart=kstart, var=var, mode="table")


def _dil_geometry(S, dil):
    L = S // dil
    tq = min(512, L)
    kw = tq + 2 * DIL_REACH
    return dict(S=S, tq=tq, kw=kw, kc=kw, kstart=lambda i: i * tq, mode="dil", dil=dil, L=L)


def _dil_slopes(g, kw):
    idx = jnp.arange(1, DIL_HEADS + 1, dtype=F32)[g * DIL_HPG:(g + 1) * DIL_HPG]
    slopes = 2.0 ** (-ALIBI_MAX_EXP * idx / DIL_HEADS)
    return jnp.broadcast_to(slopes[:, None, None], (DIL_HPG, 8, kw))


def _layer_fwd(x, p, cos, sin, next_shards=None):
    S, D = x.shape
    ts = 256
    gp = p["pre_g"][None, :]
    (h,) = _ew("pre_norm", lambda x, g: _rms_stats(x)[0] * g, [(x, "t", 0), (gp, "v", 0)], [(D, BF16)],
               rows=S, width=D, ncols=1, ts=ts)
    proj = _mm_nn("proj_in", h, p["w_in"], out_dtype=BF16)

    bias = _na_bias_build(p["rpb"])
    ya, lse_a = _attn_fwd("na_fwd", proj, C_QA, proj, C_KA, proj, C_VA, H=NA_HEADS, G=1, extra=bias,
                          **_na_geometry(S))

    def qk(name, off, nh, g, mult):
        fn = lambda x, g, cos, sin: _qk_fwd(x, g, cos, sin) * mult
        (out,) = _ew(name, fn, [(proj, "t", off), (g[None, :], "vf", 0), (cos, "tf", 0), (sin, "tf", 0)],
                     [(nh * HEAD, BF16)], rows=S, width=HEAD, ncols=nh, ts=512)
        return out

    qb = qk("q_norm_rope", C_QB, GQ_HEADS, p["q_g"], HEAD ** -0.5 * float(np.log2(np.e)))
    kb = qk("k_norm_rope", C_KB, GKV_HEADS, p["k_g"], 1.0)
    geo_b = dict(S=S, tq=min(GQA_FWD_TILE[0], S), kw=S, kc=min(GQA_FWD_TILE[1], S), kstart=lambda i: 0,
                 prescaled=True)
    carry = None if next_shards is None else _carry_gather(next_shards)
    yb, lse_b, *next_full = _attn_fwd("gqa_fwd", qb, 0, kb, 0, proj, C_VB, H=GQ_HEADS, G=GQ_HEADS // GKV_HEADS,
                                      carry=carry, **geo_b)

    W = DIL_HPG * HEAD
    og, lg, saved_c = [], [], []
    for g, (win, dil) in enumerate(DIL_GROUPS):
        assert (win // 2) // dil == DIL_REACH
        sl = lambda off: proj[:, (off + g * DIL_HPG) * HEAD:(off + (g + 1) * DIL_HPG) * HEAD]
        qg = _perm(sl(C_QC), dil)
        kg = jnp.pad(_perm(sl(C_KC), dil), ((DIL_REACH, DIL_REACH), (0, 0)))
        vg = jnp.pad(_perm(sl(C_VC), dil), ((DIL_REACH, DIL_REACH), (0, 0)))
        geo = _dil_geometry(S, dil)
        o, l = _attn_fwd(f"dil{g}_fwd", qg, 0, kg, 0, vg, 0, H=DIL_HPG, G=1, extra=_dil_slopes(g, geo["kw"]),
                         out_dtype=F32, **geo)
        og.append(_unperm(o, dil))
        lg.append(_unperm(l, dil))
        saved_c.append((qg, kg, vg))

    def combine(o1, o2, o3, l1, l2, l3):
        m = jnp.maximum(jnp.maximum(l1, l2), l3)
        e1, e2, e3 = jnp.exp(l1 - m), jnp.exp(l2 - m), jnp.exp(l3 - m)
        tot = e1 + e2 + e3
        return (e1 * o1 + e2 * o2 + e3 * o3) / tot, m + jnp.log(tot)

    yc, lse_c = _ew("dil_combine", combine, [(a, "t", 0) for a in og + lg], [(W, BF16), (W, F32)],
                    rows=S, width=W, ncols=1, ts=512)

    def gate(y, z):
        z = z.astype(F32)
        return y.astype(F32) * (z * _sigmoid(z))

    def branch(name, y, zoff, width, w):
        unit = width // HEAD
        (u,) = _ew(name + "_silu", gate, [(y, "t", 0), (proj, "tf", zoff // unit)], [(width, BF16)],
                   rows=S, width=width, ncols=1, ts=512)
        return u, _mm_nn(name + "_proj", u, w, out_dtype=BF16)

    ua, ta = branch("branch_a", ya, C_ZA, NA_HEADS * HEAD, p["w_a"])
    ub, tb = branch("branch_b", yb, C_ZB, GQ_HEADS * HEAD, p["w_b"])
    uc, tc = branch("branch_c", yc, C_ZC, W, p["w_c"])

    mw = 512
    nm = D // mw
    goff = C_GATE * HEAD // mw
    bg = p["b_gate"][None, :]

    def merge(ta, tb, tc, ga, gb, gc, ba, bb, bc):
        sa = _sigmoid(ga.astype(F32) + ba)
        sb = _sigmoid(gb.astype(F32) + bb)
        sc = _sigmoid(gc.astype(F32) + bc)
        return sa * ta.astype(F32) + sb * tb.astype(F32) + sc * tc.astype(F32)

    (merged,) = _ew("merge", merge,
                    [(ta, "t", 0), (tb, "t", 0), (tc, "t", 0),
                     (proj, "t", goff), (proj, "t", goff + nm), (proj, "t", goff + 2 * nm),
                     (bg, "v", 0), (bg, "v", nm), (bg, "v", 2 * nm)],
                    [(D, BF16)], rows=S, width=mw, ncols=nm, ts=512)
    out = _mm_nn("proj_out", merged, p["w_out"], out_dtype=F32)
    gq = p["post_g"][None, :]
    (x_next,) = _ew("post_norm_residual", lambda x, o, g: x + _rms_stats(o)[0] * g,
                    [(x, "t", 0), (out, "t", 0), (gq, "v", 0)], [(D, F32)], rows=S, width=D, ncols=1, ts=ts)
    saved = dict(x=x, h=h, proj=proj, bias=bias, ya=ya, lse_a=lse_a, qb=qb, kb=kb, yb=yb, lse_b=lse_b,
                 geo_b=geo_b, saved_c=saved_c, yc=yc, lse_c=lse_c, ua=ua, ub=ub, uc=uc, ta=ta, tb=tb, tc=tc,
                 merged=merged, out=out)
    return x_next, saved, (_fill_own(next_full, next_shards) if next_full else None)


def _pair_sums(gs, got):
    c = lax.axis_index("c")
    out = []
    for g, t in zip(gs, got):
        hr = g.shape[1] // 2
        out.append(_add_rows("grad_pair_sum", [lax.dynamic_slice_in_dim(g, c * hr, hr, axis=1), t]))
    return out


def _chip_sums(pair, recv):
    me = 2 * lax.axis_index("x") + lax.axis_index("y")
    out = []
    for q, t in zip(pair, recv):
        own = lax.dynamic_index_in_dim(q, me, axis=0, keepdims=False)
        out.append(_add_rows("grad_chip_sum", [own, t[0], t[1], t[2]]))
    return out


def _fill_own(gathered, shards):
    me = 2 * lax.axis_index("x") + lax.axis_index("y")
    return [lax.dynamic_update_slice_in_dim(g, a[None], me, axis=0) for g, a in zip(gathered, shards)]


def _layer_bwd(gy, p, s, cos, sin, pending=None, reduce_own=False):
    S, D = gy.shape
    ts = 256
    proj = s["proj"]
    grads = {}

    def post_bwd(o, dy, g):
        xhat, r = _rms_stats(o)
        return _rms_bwd(xhat, r, dy, g), dy * xhat

    dout, dg_post = _ew("post_norm_bwd", post_bwd, [(s["out"], "t", 0), (gy, "t", 0), (p["post_g"][None, :], "v", 0)],
                        [(D, BF16)], accs=("n",), rows=S, width=D, ncols=1, ts=ts)
    grads["post_g"] = dg_post[0]
    dmerged = _mm_nn("proj_out_bwd", dout, p["w_out_t"], out_dtype=BF16)
    grads["w_out"] = _mm_tn("proj_out_wgrad", s["merged"], dout, blocks=1).reshape(N_CHIPS, D // N_CHIPS, D)

    mw = 512
    nm = D // mw
    goff = C_GATE * HEAD // mw
    bg = p["b_gate"][None, :]

    def merge_bwd(dm, ta, tb, tc, ga, gb, gc, ba, bb, bc):
        dm = dm.astype(F32)
        res_t, res_g = [], []
        for t, g, b in ((ta, ga, ba), (tb, gb, bb), (tc, gc, bc)):
            sg = _sigmoid(g.astype(F32) + b)
            res_t.append(dm * sg)
            res_g.append(dm * t.astype(F32) * sg * (1.0 - sg))
        return (*res_t, *res_g, *res_g)

    mres = _ew("merge_bwd", merge_bwd,
               [(dmerged, "t", 0), (s["ta"], "t", 0), (s["tb"], "t", 0), (s["tc"], "t", 0),
                (proj, "t", goff), (proj, "t", goff + nm), (proj, "t", goff + 2 * nm),
                (bg, "v", 0), (bg, "v", nm), (bg, "v", 2 * nm)],
               [(D, BF16)] * 6, accs=("n", "n", "n"), rows=S, width=mw, ncols=nm, ts=512,
               carry=None if pending is None else _carry_pair_exchange(pending))
    dta, dtb, dtc, dga, dgb, dgc = mres[:6]
    grads["b_gate"] = jnp.concatenate([a[0] for a in mres[6:9]])
    pair = None if pending is None else _pair_sums(pending, mres[9:])

    def silu_bwd(du, y, z):
        du, y, z = du.astype(F32), y.astype(F32), z.astype(F32)
        sg = _sigmoid(z)
        return du * (z * sg), du * y * (sg * (1.0 + z * (1.0 - sg)))

    def branch_bwd(name, dt, u, y, zoff, width, w_t):
        du = _mm_nn(name + "_proj_bwd", dt, w_t, out_dtype=BF16)
        gw = _mm_tn(name + "_wgrad", u, dt, blocks=N_CHIPS)
        unit = width // HEAD
        dy, dz = _ew(name + "_silu_bwd", silu_bwd, [(du, "t", 0), (y, "t", 0), (proj, "tf", zoff // unit)],
                     [(width, BF16), (width, BF16)], rows=S, width=width, ncols=1, ts=512)
        return dy, dz, gw

    W = DIL_HPG * HEAD
    dya, dza, grads["w_a"] = branch_bwd("branch_a", dta, s["ua"], s["ya"], C_ZA, NA_HEADS * HEAD, p["w_a_t"])
    dyb, dzb, grads["w_b"] = branch_bwd("branch_b", dtb, s["ub"], s["yb"], C_ZB, GQ_HEADS * HEAD, p["w_b_t"])
    dyc, dzc, grads["w_c"] = branch_bwd("branch_c", dtc, s["uc"], s["yc"], C_ZC, W, p["w_c_t"])

    dqa, dka, dva, dbias = _attn_bwd("na_bwd", proj, C_QA, proj, C_KA, proj, C_VA, s["ya"], dya, s["lse_a"],
                                     H=NA_HEADS, G=1, extra=s["bias"], **_na_geometry(S))
    grads["rpb"] = _na_bias_grad(dbias)

    geo_b = dict(s["geo_b"])
    geo_b["tq"], geo_b["kc"] = min(GQA_BWD_TILE[0], S), min(GQA_BWD_TILE[1], S)
    dqr, dkr, dvb, *recv = _attn_bwd("gqa_bwd", s["qb"], 0, s["kb"], 0, proj, C_VB, s["yb"], dyb, s["lse_b"],
                                     H=GQ_HEADS, G=GQ_HEADS // GKV_HEADS,
                                     carry=None if pending is None else _carry_chip_exchange(pair), **geo_b)
    half = None if pending is None else _chip_sums(pair, recv)

    def qk_bwd(name, dyr, off, nh, g):
        return _ew(name, _qk_bwd, [(dyr, "t", 0), (proj, "t", off), (g[None, :], "vf", 0), (cos, "tf", 0),
                                   (sin, "tf", 0)],
                   [(nh * HEAD, BF16)], accs=("f",), rows=S, width=HEAD, ncols=nh, ts=512)

    dqb, dgq = qk_bwd("q_norm_rope_bwd", dqr, C_QB, GQ_HEADS, p["q_g"])
    dkb, dgk = qk_bwd("k_norm_rope_bwd", dkr, C_KB, GKV_HEADS, p["k_g"])
    grads["q_g"] = dgq[0]
    grads["k_g"] = dgk[0]

    dqc, dkc, dvc = [], [], []
    for g, (win, dil) in enumerate(DIL_GROUPS):
        qg, kg, vg = s["saved_c"][g]
        geo = _dil_geometry(S, dil)
        dq, dk, dv = _attn_bwd(f"dil{g}_bwd", qg, 0, kg, 0, vg, 0, _perm(s["yc"], dil), _perm(dyc, dil),
                               _perm(s["lse_c"], dil), H=DIL_HPG, G=1, extra=_dil_slopes(g, geo["kw"]), **geo)
        dqc.append(_unperm(dq, dil))
        dkc.append(_unperm(dk[DIL_REACH:-DIL_REACH], dil).astype(BF16))
        dvc.append(_unperm(dv[DIL_REACH:-DIL_REACH], dil).astype(BF16))

    dproj = jnp.concatenate(
        [dqa, dka.astype(BF16), dva.astype(BF16), dqb, dkb, dvb.astype(BF16), *dqc, *dkc, *dvc,
         dza, dzb, dzc, dga, dgb, dgc], axis=1)
    grads["w_in"] = _mm_tn("proj_in_wgrad", s["h"], dproj, blocks=N_CHIPS, ts=1024)
    if reduce_own:
        assert pending is not None
        own = [grads[n] for n in BIG]
        own_pair = _pair_sums(own, _exchange("grad_pair_exchange", _carry_pair_exchange(own)))
        carry = _carry_complete_and_chip_exchange(half, own_pair)
    else:
        carry = None if pending is None else _carry_pair_complete(half)
    res = _mm_deep("proj_in_bwd", dproj, p["w_in_t"], out_dtype=F32, tk_cap=2560, carry=carry)
    done, own_done = None, None
    if carry is None:
        dh = res
    else:
        dh, rest = res[0], res[1:]
        done = [_join_halves(m, o) for m, o in zip(half, rest[:len(half)])]
    if reduce_own:
        own_half = _chip_sums(own_pair, rest[len(half):])
        other = _exchange("grad_pair_complete", _carry_pair_complete(own_half))
        own_done = [_join_halves(m, o) for m, o in zip(own_half, other)]

    def pre_bwd(x, dh, g, gy):
        xhat, r = _rms_stats(x)
        return gy + _rms_bwd(xhat, r, dh, g), dh * xhat

    dx, dg_pre = _ew("pre_norm_bwd", pre_bwd, [(s["x"], "t", 0), (dh, "t", 0), (p["pre_g"][None, :], "v", 0),
                                                 (gy, "t", 0)],
                     [(D, F32)], accs=("n",), rows=S, width=D, ncols=1, ts=ts)
    grads["pre_g"] = dg_pre[0]
    return dx, grads, done, own_done


BIG = ("w_in", "w_a", "w_b", "w_c", "w_out")
SMALL = ("pre_g", "b_gate", "q_g", "k_g", "rpb", "post_g")


def _pack(parts, rows_mult=8):
    flat = jnp.concatenate([a.reshape(-1) for a in parts])
    n = flat.shape[0]
    unit = rows_mult * LANE
    padded = -(-n // unit) * unit
    return jnp.pad(flat, (0, padded - n)).reshape(-1, LANE)


def kernel(x, pre_norm_g, w_in, b_gate, q_norm_g, k_norm_g, rpb, w_branch_a, w_branch_b, w_branch_c, w_out, post_norm_g, loss_target, m_pre_norm_g, m_w_in, m_b_gate, m_q_norm_g, m_k_norm_g, m_rpb, m_w_branch_a, m_w_branch_b, m_w_branch_c, m_w_out, m_post_norm_g, v_pre_norm_g, v_w_in, v_b_gate, v_q_norm_g, v_k_norm_g, v_rpb, v_w_branch_a, v_w_branch_b, v_w_branch_c, v_w_out, v_post_norm_g):
    n_layers = w_in.shape[0]
    S, D = x.shape[1], x.shape[2]
    big_w = dict(w_in=w_in, w_a=w_branch_a, w_b=w_branch_b, w_c=w_branch_c, w_out=w_out)
    big_m = dict(w_in=m_w_in, w_a=m_w_branch_a, w_b=m_w_branch_b, w_c=m_w_branch_c, w_out=m_w_out)
    big_v = dict(w_in=v_w_in, w_a=v_w_branch_a, w_b=v_w_branch_b, w_c=v_w_branch_c, w_out=v_w_out)
    small_w = dict(pre_g=pre_norm_g, b_gate=b_gate, q_g=q_norm_g, k_g=k_norm_g, rpb=rpb, post_g=post_norm_g)
    small_m = dict(pre_g=m_pre_norm_g, b_gate=m_b_gate, q_g=m_q_norm_g, k_g=m_k_norm_g, rpb=m_rpb,
                   post_g=m_post_norm_g)
    small_v = dict(pre_g=v_pre_norm_g, b_gate=v_b_gate, q_g=v_q_norm_g, k_g=v_k_norm_g, rpb=v_rpb,
                   post_g=v_post_norm_g)

    shards = [[big_w[n][l].astype(BF16) for n in BIG] for l in range(n_layers)]
    cos, sin = _rope_tables(S)

    def layer_params(l, full):
        p = dict(zip(BIG, full))
        for n in ("w_in", "w_a", "w_b", "w_c"):
            p[n + "_t"] = jnp.swapaxes(p[n], 1, 2).reshape(-1, p[n].shape[1])
        p["w_out"] = p["w_out"].reshape(D, D)
        p["w_out_t"] = p["w_out"].T
        p.update(pre_g=pre_norm_g[l], b_gate=b_gate[l], q_g=q_norm_g[l], k_g=k_norm_g[l], rpb=rpb[l],
                 post_g=post_norm_g[l])
        return p

    full = _fill_own(_allgather_weights(shards[0]), shards[0])
    act = x[0]
    saved, params = [], []
    for l in range(n_layers):
        params.append(layer_params(l, full))
        act, s, full = _layer_fwd(act, params[l], cos, sin, shards[l + 1] if l + 1 < n_layers else None)
        saved.append(s)

    dy, loss_cols = _ew("loss", lambda y, t: ((y - t) * (1.0 / D), (y - t) * (y - t)),
                        [(act, "t", 0), (loss_target[0], "t", 0)], [(D, F32)], accs=("n",),
                        rows=S, width=D, ncols=1, ts=256)

    big_g = {n: [None] * n_layers for n in BIG}
    small_g = {n: [None] * n_layers for n in SMALL}
    pending = None
    for l in reversed(range(n_layers)):
        dy, grads, done, own_done = _layer_bwd(dy, params[l], saved[l], cos, sin, pending, reduce_own=(l == 0))
        if done is not None:
            for n, r in zip(BIG, done):
                big_g[n][l + 1] = r
        pending = [grads[n] for n in BIG]
        for n in SMALL:
            small_g[n][l] = grads[n]
    for n, r in zip(BIG, own_done):
        big_g[n][0] = r

    loss_rows = -(-D // (8 * LANE)) * 8
    parts = [_pack([loss_cols[0]])] + [_pack([jnp.stack(small_g[n]) for n in SMALL])]
    tot, loss = _allreduce_small(jnp.concatenate(parts), loss_rows, 0.5 / D)
    wvec = jnp.concatenate([jnp.zeros((loss_rows, LANE), F32), _pack([small_w[n] for n in SMALL])])
    mvec = jnp.concatenate([jnp.zeros((loss_rows, LANE), F32), _pack([small_m[n] for n in SMALL])])
    vvec = jnp.concatenate([jnp.ones((loss_rows, LANE), F32), _pack([small_v[n] for n in SMALL])])
    small_out = [tot] + _adamw("adamw_small", wvec, tot, mvec, vvec)

    def unpack(vec):
        flat = vec[loss_rows:].reshape(-1)
        res, pos = {}, 0
        for n in SMALL:
            size = int(np.prod(small_w[n].shape))
            res[n] = flat[pos:pos + size].reshape(small_w[n].shape)
            pos += size
        return res

    small_res = [unpack(v) for v in small_out]

    big_res = [{}, {}, {}, {}]
    for n in BIG:
        g = jnp.stack(big_g[n])
        big_res[0][n] = g
        for k, val in enumerate(_adamw("adamw_" + n, big_w[n], g, big_m[n], big_v[n])):
            big_res[k + 1][n] = val

    order = (("pre_g", small_res), ("w_in", big_res), ("b_gate", small_res), ("q_g", small_res),
             ("k_g", small_res), ("rpb", small_res), ("w_a", big_res), ("w_b", big_res), ("w_c", big_res),
             ("w_out", big_res), ("post_g", small_res))
    outs = [loss[0, 0], dy[None]]
    for k in range(4):
        outs.extend(src[k][n] for n, src in order)
    return tuple(outs)
```

```python
import functools

import numpy as np
import jax
import jax.numpy as jnp
from jax import lax
from jax.experimental import pallas as pl
from jax.experimental.pallas import tpu as pltpu

F32 = jnp.float32
BF16 = jnp.bfloat16
MESH = pl.DeviceIdType.MESH

HEAD = 128
GRID_W = 64
EPS = 1e-6
NEG = -1e30
NA_HEADS = 8
NA_WIN_R = 8
NA_WIN_C = 16
GQ_HEADS = 8
GKV_HEADS = 2
ROPE_BASE = 10000.0
DIL_GROUPS = ((128, 1), (512, 4), (2048, 16))
DIL_REACH = 64
DIL_HPG = 4
DIL_HEADS = 12
ALIBI_MAX_EXP = 8.0
ADAM_LR = 0.001
ADAM_B1 = 0.9
ADAM_B2 = 0.999
ADAM_EPS = 1e-08
ADAM_WD = 0.01
ADAM_STEP = 10

C_QA, C_KA, C_VA = 0, 8, 16
C_QB, C_KB, C_VB = 24, 32, 34
C_QC, C_KC, C_VC = 36, 48, 60
C_ZA, C_ZB, C_ZC = 72, 80, 88
C_GATE = 92

NA_QR = 4
NA_KR = NA_QR + NA_WIN_R
NA_VARIANTS = 3

GQA_FWD_TILE = (1024, 1024)
GQA_BWD_TILE = (1024, 1024)

VMEM_LIMIT_BYTES = 48 * 1024 * 1024
LANE = 128
N_CHIPS = 4
N_DEV = 8


def _params(sem=None):
    kw = dict(vmem_limit_bytes=VMEM_LIMIT_BYTES)
    if sem is not None:
        kw["dimension_semantics"] = sem
    return pltpu.CompilerParams(**kw)


def _tile(n, cap):
    units = n // LANE
    assert units * LANE == n, n
    best = 1
    for d in range(1, units + 1):
        if units % d == 0 and d * LANE <= cap:
            best = d
    return best * LANE


def _ew(name, fn, ins, outs, accs=(), *, rows, width, ncols, ts, carry=None):
    ts = min(ts, rows)
    assert rows % ts == 0
    nr = rows // ts
    ni, no = len(ins), len(outs)

    def imap(mode, off):
        if mode == "t":
            return lambda n, i: (i, off + n)
        if mode == "tf":
            return lambda n, i: (i, off)
        if mode == "v":
            return lambda n, i: (0, off + n)
        return lambda n, i: (0, off)

    in_specs = [pl.BlockSpec((ts if m in ("t", "tf") else 1, width), imap(m, o)) for (_, m, o) in ins]
    out_specs = [pl.BlockSpec((ts, width), lambda n, i: (i, n)) for _ in outs]
    out_shape = [jax.ShapeDtypeStruct((rows, c), d) for (c, d) in outs]
    for a in accs:
        if a == "n":
            out_specs.append(pl.BlockSpec((8, width), lambda n, i: (0, n)))
            out_shape.append(jax.ShapeDtypeStruct((8, width * ncols), F32))
        else:
            out_specs.append(pl.BlockSpec((8, width), lambda n, i: (0, 0)))
            out_shape.append(jax.ShapeDtypeStruct((8, width), F32))

    def body(*refs):
        res = fn(*[r[...] for r in refs[:ni]])
        if not isinstance(res, tuple):
            res = (res,)
        for r, val in zip(refs[ni:ni + no], res[:no]):
            r[...] = val.astype(r.dtype)
        n = pl.program_id(0)
        i = pl.program_id(1)
        for kind, r, val in zip(accs, refs[ni + no:], res[no:]):
            first = (i == 0) if kind == "n" else jnp.logical_and(i == 0, n == 0)

            @pl.when(first)
            def _(r=r):
                r[...] = jnp.zeros(r.shape, r.dtype)

            r[...] += jnp.broadcast_to(jnp.sum(val.astype(F32), axis=0, keepdims=True), r.shape)

    grid = (ncols, nr)
    body, c_in, c_in_specs, c_out, c_out_specs, c_scratch = _carried(body, ni, no + len(accs), grid, carry)
    res = pl.pallas_call(
        body, name=name, grid=grid, in_specs=in_specs + c_in_specs, out_specs=out_specs + c_out_specs,
        out_shape=out_shape + c_out, scratch_shapes=c_scratch,
        compiler_params=_params(("arbitrary", "arbitrary")),
    )(*[a for (a, _, _) in ins], *c_in)
    return res


def _mm_nn(name, a, w, *, out_dtype, tm=1024, tn_cap=1024):
    M, K = a.shape
    blocked = w.ndim == 3
    NB = w.shape[-1]
    N = NB * (w.shape[0] if blocked else 1)
    tm = min(tm, M)
    tn = _tile(NB, tn_cap)
    per = NB // tn
    if blocked:
        w_spec = pl.BlockSpec((None, K, tn), lambda i, n: (n // per, 0, n % per))
    else:
        w_spec = pl.BlockSpec((K, tn), lambda i, n: (0, n))

    def body(a_ref, w_ref, o_ref):
        o_ref[...] = jnp.dot(a_ref[...], w_ref[...], preferred_element_type=F32).astype(o_ref.dtype)

    return pl.pallas_call(
        body, name=name, grid=(M // tm, N // tn),
        in_specs=[pl.BlockSpec((tm, K), lambda i, n: (i, 0)), w_spec],
        out_specs=pl.BlockSpec((tm, tn), lambda i, n: (i, n)),
        out_shape=jax.ShapeDtypeStruct((M, N), out_dtype),
        compiler_params=_params(("arbitrary", "arbitrary")),
    )(a, w)


def _mm_deep(name, a, w, *, out_dtype, tm=512, tk_cap=1024, carry=None):
    M, N = a.shape
    K = w.shape[1]
    assert w.shape[0] == N
    tm = min(tm, M)
    tk = _tile(N, tk_cap)
    nk = N // tk
    w_spec = pl.BlockSpec((tk, K), lambda i, k: (k, 0))

    def body(a_ref, w_ref, o_ref, acc_ref):
        k = pl.program_id(1)

        @pl.when(k == 0)
        def _():
            acc_ref[...] = jnp.zeros(acc_ref.shape, F32)

        acc_ref[...] += jnp.dot(a_ref[...], w_ref[...], preferred_element_type=F32)

        @pl.when(k == nk - 1)
        def _():
            o_ref[...] = acc_ref[...].astype(o_ref.dtype)

    grid = (M // tm, nk)
    body, c_in, c_in_specs, c_out, c_out_specs, c_scratch = _carried(body, 2, 1, grid, carry)
    res = pl.pallas_call(
        body, name=name, grid=grid,
        in_specs=[pl.BlockSpec((tm, tk), lambda i, k: (i, k)), w_spec] + c_in_specs,
        out_specs=[pl.BlockSpec((tm, K), lambda i, k: (i, 0))] + c_out_specs,
        out_shape=[jax.ShapeDtypeStruct((M, K), out_dtype)] + c_out,
        scratch_shapes=[pltpu.VMEM((tm, K), F32)] + c_scratch,
        compiler_params=_params(("arbitrary", "arbitrary")),
    )(a, w, *c_in)
    return res[0] if carry is None else res


def _mm_tn(name, a, b, *, blocks, ts=512, tn_cap=1024):
    S, K = a.shape
    N = b.shape[1]
    NB = N // blocks
    ts = min(ts, S)
    tn = _tile(NB, tn_cap)
    per = NB // tn
    ns = S // ts

    def body(a_ref, b_ref, o_ref):
        s = pl.program_id(1)

        @pl.when(s == 0)
        def _():
            o_ref[...] = jnp.zeros(o_ref.shape, F32)

        o_ref[...] += lax.dot_general(a_ref[...], b_ref[...], (((0,), (0,)), ((), ())),
                                      preferred_element_type=F32)

    return pl.pallas_call(
        body, name=name, grid=(N // tn, ns),
        in_specs=[pl.BlockSpec((ts, K), lambda n, s: (s, 0)), pl.BlockSpec((ts, tn), lambda n, s: (s, n))],
        out_specs=pl.BlockSpec((None, K, tn), lambda n, s: (n // per, 0, n % per)),
        out_shape=jax.ShapeDtypeStruct((blocks, K, NB), F32),
        compiler_params=_params(("arbitrary", "arbitrary")),
    )(a, b)


def _aligned(off):
    return off if isinstance(off, int) else pl.multiple_of(off, LANE)


def _attn_bias(s, mode, extra_ref, i, tq, kc, dil, L):
    if mode == "table":
        return s + extra_ref[...]
    if mode == "dil":
        r = lax.broadcasted_iota(jnp.int32, (tq, kc), 0)
        tk = lax.broadcasted_iota(jnp.int32, (tq, kc), 1) - DIL_REACH
        dist = jnp.abs(r - tk)
        kin = (i * tq) % L + tk
        valid = (dist <= DIL_REACH) & (kin >= 0) & (kin < L)
        s = s - extra_ref[0:1, :] * (dil * dist).astype(F32)
        return jnp.where(valid, s, NEG)
    return s


def _attn_fwd(name, q, qo, k, ko, v, vo, *, H, G, S, tq, kw, kc, kstart, mode=None, extra=None,
              var=None, dil=1, L=1, out_dtype=BF16, prescaled=False, carry=None):
    nq = S // tq
    nch = kw // kc
    Sk = k.shape[0]
    scale = HEAD ** -0.5
    has_extra = mode is not None
    exp = jnp.exp2 if prescaled else jnp.exp
    log = jnp.log2 if prescaled else jnp.log

    def body(*refs):
        if has_extra:
            q_ref, k_ref, v_ref, e_ref, o_ref, l_ref = refs
        else:
            q_ref, k_ref, v_ref, o_ref, l_ref = refs
            e_ref = None
        i = pl.program_id(1)
        qb = q_ref[...]
        start = kstart(i)

        def chunk(j, carry):
            m, l, acc = carry
            off = _aligned(start + j * kc)
            kb = k_ref[pl.ds(off, kc), :]
            vb = v_ref[pl.ds(off, kc), :]
            s = lax.dot_general(qb, kb, (((1,), (1,)), ((), ())), preferred_element_type=F32)
            if not prescaled:
                s = s * scale
            s = _attn_bias(s, mode, e_ref, i, tq, kc, dil, L)
            mn = jnp.maximum(m, jnp.max(s, axis=1, keepdims=True))
            p = exp(s - mn)
            a = exp(m - mn)
            l = a * l + jnp.sum(p, axis=1, keepdims=True)
            acc = a * acc + jnp.dot(p.astype(BF16), vb, preferred_element_type=F32)
            return mn, l, acc

        init = (jnp.full((tq, 1), -3.0e38, F32), jnp.zeros((tq, 1), F32), jnp.zeros((tq, HEAD), F32))
        if nch == 1:
            m, l, acc = chunk(0, init)
        else:
            m, l, acc = lax.fori_loop(0, nch, chunk, init)
        o_ref[...] = (acc / l).astype(o_ref.dtype)
        l_ref[...] = jnp.broadcast_to(m + log(l), (tq, HEAD))

    in_specs = [
        pl.BlockSpec((tq, HEAD), lambda h, i: (i, qo + h)),
        pl.BlockSpec((Sk, HEAD), lambda h, i: (0, ko + h // G)),
        pl.BlockSpec((Sk, HEAD), lambda h, i: (0, vo + h // G)),
    ]
    args = [q, k, v]
    if mode == "table":
        in_specs.append(pl.BlockSpec((None, None, tq, kw), lambda h, i: (var(i), h, 0, 0)))
        args.append(extra)
    elif mode == "dil":
        in_specs.append(pl.BlockSpec((None, 8, kw), lambda h, i: (h, 0, 0)))
        args.append(extra)
    grid = (H, nq)
    body, c_in, c_in_specs, c_out, c_out_specs, c_scratch = _carried(body, len(args), 2, grid, carry)
    return pl.pallas_call(
        body, name=name, grid=grid, in_specs=in_specs + c_in_specs,
        out_specs=[pl.BlockSpec((tq, HEAD), lambda h, i: (i, h)),
                   pl.BlockSpec((tq, HEAD), lambda h, i: (i, h))] + c_out_specs,
        out_shape=[jax.ShapeDtypeStruct((S, H * HEAD), out_dtype),
                   jax.ShapeDtypeStruct((S, H * HEAD), F32)] + c_out,
        scratch_shapes=c_scratch,
        compiler_params=_params(("arbitrary", "arbitrary")),
    )(*args, *c_in)


def _attn_bwd(name, q, qo, k, ko, v, vo, o, do, lse, *, H, G, S, tq, kw, kc, kstart, mode=None, extra=None,
              var=None, dil=1, L=1, prescaled=False, carry=None):
    nq = S // tq
    nch = kw // kc
    Sk = k.shape[0]
    HK = H // G
    scale = HEAD ** -0.5
    dk_mult = float(np.log(2.0)) if prescaled else scale
    exp = jnp.exp2 if prescaled else jnp.exp
    has_extra = mode is not None
    table = mode == "table"
    if table:
        assert nq >= NA_VARIANTS and nch == 1

    def body(*refs):
        refs = list(refs)
        q_ref, k_ref, v_ref, o_ref, do_ref, l_ref = refs[:6]
        e_ref = refs[6] if has_extra else None
        outs = refs[7:] if has_extra else refs[6:]
        dq_ref, dk_ref, dv_ref = outs[:3]
        h = pl.program_id(0)
        i = pl.program_id(1)

        @pl.when(jnp.logical_and(h % G == 0, i == 0))
        def _():
            dk_ref[...] = jnp.zeros(dk_ref.shape, F32)
            dv_ref[...] = jnp.zeros(dv_ref.shape, F32)

        if table:
            db_ref = outs[3]

            @pl.when((i == 0) | (i == 1) | (i == nq - 1))
            def _():
                db_ref[...] = jnp.zeros(db_ref.shape, F32)

        qb = q_ref[...]
        dob = do_ref[...]
        delta = jnp.sum(o_ref[...].astype(F32) * dob.astype(F32), axis=1, keepdims=True)
        lse_b = l_ref[:, 0:1]
        start = kstart(i)

        def chunk(j, dq):
            off = _aligned(start + j * kc)
            kb = k_ref[pl.ds(off, kc), :]
            vb = v_ref[pl.ds(off, kc), :]
            s = lax.dot_general(qb, kb, (((1,), (1,)), ((), ())), preferred_element_type=F32)
            if not prescaled:
                s = s * scale
            s = _attn_bias(s, mode, e_ref, i, tq, kc, dil, L)
            p = exp(s - lse_b)
            dp = lax.dot_general(dob, vb, (((1,), (1,)), ((), ())), preferred_element_type=F32)
            ds = p * (dp - delta)
            dsb = ds.astype(BF16)
            dk_ref[pl.ds(off, kc), :] += dk_mult * lax.dot_general(
                dsb, qb, (((0,), (0,)), ((), ())), preferred_element_type=F32)
            dv_ref[pl.ds(off, kc), :] += lax.dot_general(
                p.astype(BF16), dob, (((0,), (0,)), ((), ())), preferred_element_type=F32)
            if table:
                db_ref[...] += ds
            return dq + jnp.dot(dsb, kb, preferred_element_type=F32)

        dq0 = jnp.zeros((tq, HEAD), F32)
        dq = chunk(0, dq0) if nch == 1 else lax.fori_loop(0, nch, chunk, dq0)
        dq_ref[...] = (dq * scale).astype(dq_ref.dtype)

    in_specs = [
        pl.BlockSpec((tq, HEAD), lambda h, i: (i, qo + h)),
        pl.BlockSpec((Sk, HEAD), lambda h, i: (0, ko + h // G)),
        pl.BlockSpec((Sk, HEAD), lambda h, i: (0, vo + h // G)),
        pl.BlockSpec((tq, HEAD), lambda h, i: (i, h)),
        pl.BlockSpec((tq, HEAD), lambda h, i: (i, h)),
        pl.BlockSpec((tq, HEAD), lambda h, i: (i, h)),
    ]
    args = [q, k, v, o, do, lse]
    out_specs = [
        pl.BlockSpec((tq, HEAD), lambda h, i: (i, h)),
        pl.BlockSpec((Sk, HEAD), lambda h, i: (0, h // G)),
        pl.BlockSpec((Sk, HEAD), lambda h, i: (0, h // G)),
    ]
    out_shape = [
        jax.ShapeDtypeStruct((S, H * HEAD), BF16),
        jax.ShapeDtypeStruct((Sk, HK * HEAD), F32),
        jax.ShapeDtypeStruct((Sk, HK * HEAD), F32),
    ]
    if table:
        in_specs.append(pl.BlockSpec((None, None, tq, kw), lambda h, i: (var(i), h, 0, 0)))
        args.append(extra)
        out_specs.append(pl.BlockSpec((None, None, tq, kw), lambda h, i: (var(i), h, 0, 0)))
        out_shape.append(jax.ShapeDtypeStruct(extra.shape, F32))
    elif mode == "dil":
        in_specs.append(pl.BlockSpec((None, 8, kw), lambda h, i: (h, 0, 0)))
        args.append(extra)
    grid = (H, nq)
    body, c_in, c_in_specs, c_out, c_out_specs, c_scratch = _carried(body, len(args), len(out_shape), grid, carry)
    return pl.pallas_call(
        body, name=name, grid=grid, in_specs=in_specs + c_in_specs, out_specs=out_specs + c_out_specs,
        out_shape=out_shape + c_out, scratch_shapes=c_scratch,
        compiler_params=_params(("arbitrary", "arbitrary")),
    )(*args, *c_in)


def _na_constants():
    pairs = NA_VARIANTS * NA_QR * NA_KR
    sel = np.zeros((pairs, 16), np.float32)
    row_ok = np.zeros((pairs, 1), np.float32)
    for var in range(NA_VARIANTS):
        for qr in range(NA_QR):
            for kr in range(NA_KR):
                p = (var * NA_QR + qr) * NA_KR + kr
                first_key = (0, qr, NA_QR)[var]
                dr = kr - qr + (NA_WIN_R - 1) - (0, NA_QR, 2 * NA_QR)[var]
                if first_key <= kr < first_key + NA_WIN_R:
                    assert 0 <= dr < 2 * NA_WIN_R - 1
                    sel[p, dr] = 1.0
                    row_ok[p, 0] = 1.0
    j = np.arange(GRID_W)[:, None]
    c = np.arange(GRID_W)[None, :]
    cs = np.clip(j - NA_WIN_C // 2, 0, GRID_W - NA_WIN_C)
    col_ok = ((c >= cs) & (c < cs + NA_WIN_C)).astype(np.float32).reshape(1, GRID_W * GRID_W)
    dc = np.clip(c - j + NA_WIN_C - 1, 0, 2 * NA_WIN_C - 2).reshape(-1)
    toe = np.zeros((32, GRID_W * GRID_W), np.float32)
    toe[dc, np.arange(GRID_W * GRID_W)] = 1.0
    return sel, row_ok, col_ok, toe


def _split3(x):
    hi = x.astype(BF16)
    r1 = x - hi.astype(F32)
    mid = r1.astype(BF16)
    lo = (r1 - mid.astype(F32)).astype(BF16)
    return hi, mid, lo


def _dot_sel_left(onehot_bf16, x):
    return sum(jnp.dot(onehot_bf16, t, preferred_element_type=F32) for t in _split3(x))


def _dot_sel_right(x, onehot_bf16):
    return sum(jnp.dot(t, onehot_bf16, preferred_element_type=F32) for t in _split3(x))


def _na_bias_build(rpb_l):
    sel, row_ok, col_ok, toe = _na_constants()
    pairs = sel.shape[0]
    ww = GRID_W * GRID_W
    t = jnp.pad(rpb_l, ((0, 0), (0, 1), (0, 1)))

    def body(t_ref, sel_ref, toe_ref, rok_ref, cok_ref, o_ref):
        rows = _dot_sel_right(t_ref[...], toe_ref[...])
        val = _dot_sel_left(sel_ref[...], rows)
        ok = rok_ref[...] * cok_ref[...]
        o_ref[...] = jnp.where(ok > 0.5, val, NEG)

    full = lambda shape: pl.BlockSpec(shape, lambda h: (0,) * len(shape))
    out = pl.pallas_call(
        body, name="na_bias_build", grid=(NA_HEADS,),
        in_specs=[pl.BlockSpec((None, 16, 32), lambda h: (h, 0, 0)), full((pairs, 16)), full((32, ww)),
                  full((pairs, 1)), full((1, ww))],
        out_specs=pl.BlockSpec((None, pairs, ww), lambda h: (h, 0, 0)),
        out_shape=jax.ShapeDtypeStruct((NA_HEADS, pairs, ww), F32),
        compiler_params=_params(("arbitrary",)),
    )(t, jnp.asarray(sel, BF16), jnp.asarray(toe, BF16), jnp.asarray(row_ok), jnp.asarray(col_ok))
    out = out.reshape(NA_HEADS, NA_VARIANTS, NA_QR, NA_KR, GRID_W, GRID_W)
    out = out.transpose(1, 0, 2, 4, 3, 5)
    return out.reshape(NA_VARIANTS, NA_HEADS, NA_QR * GRID_W, NA_KR * GRID_W)


def _na_bias_grad(dbias):
    sel, _, _, toe = _na_constants()
    pairs = sel.shape[0]
    ww = GRID_W * GRID_W
    d = dbias.reshape(NA_VARIANTS, NA_HEADS, NA_QR, GRID_W, NA_KR, GRID_W).transpose(1, 0, 2, 4, 3, 5)
    d = d.reshape(NA_HEADS, pairs, ww)

    def body(d_ref, selt_ref, toet_ref, o_ref):
        rows = _dot_sel_left(selt_ref[...], d_ref[...])
        o_ref[...] = _dot_sel_right(rows, toet_ref[...])

    full = lambda shape: pl.BlockSpec(shape, lambda h: (0,) * len(shape))
    out = pl.pallas_call(
        body, name="na_bias_grad", grid=(NA_HEADS,),
        in_specs=[pl.BlockSpec((None, pairs, ww), lambda h: (h, 0, 0)), full((16, pairs)), full((ww, 32))],
        out_specs=pl.BlockSpec((None, 16, 32), lambda h: (h, 0, 0)),
        out_shape=jax.ShapeDtypeStruct((NA_HEADS, 16, 32), F32),
        compiler_params=_params(("arbitrary",)),
    )(d, jnp.asarray(sel.T, BF16), jnp.asarray(toe.T, BF16))
    return out[:, :2 * NA_WIN_R - 1, :2 * NA_WIN_C - 1]


def _sigmoid(x):
    return 1.0 / (1.0 + jnp.exp(-x))


def _swap32(y):
    lane = lax.broadcasted_iota(jnp.int32, y.shape, 1)
    up = pltpu.roll(y, HEAD - 32, 1)
    down = pltpu.roll(y, 32, 1)
    return jnp.where((lane // 32) % 2 == 0, up, down)


def _rope_tables(S):
    t = jnp.arange(S, dtype=jnp.int32)
    freqs = ROPE_BASE ** (-jnp.arange(32, dtype=F32) / 32)
    ar = (t // GRID_W).astype(F32)[:, None] * freqs[None, :]
    ac = (t % GRID_W).astype(F32)[:, None] * freqs[None, :]
    cos = jnp.concatenate([jnp.cos(ar), jnp.cos(ar), jnp.cos(ac), jnp.cos(ac)], axis=1)
    sin = jnp.concatenate([-jnp.sin(ar), jnp.sin(ar), -jnp.sin(ac), jnp.sin(ac)], axis=1)
    return cos, sin


def _rms_stats(x):
    x = x.astype(F32)
    r = lax.rsqrt(jnp.mean(x * x, axis=-1, keepdims=True) + EPS)
    return x * r, r


def _rms_bwd(xhat, r, dy, g):
    dxh = dy * g
    return r * (dxh - xhat * jnp.mean(dxh * xhat, axis=-1, keepdims=True))


def _qk_fwd(x, g, cos, sin):
    xhat, _ = _rms_stats(x)
    y = xhat * g
    return y * cos + _swap32(y) * sin


def _qk_bwd(dyr, x, g, cos, sin):
    dyr = dyr.astype(F32)
    dy = dyr * cos + _swap32(dyr * sin)
    xhat, r = _rms_stats(x)
    return _rms_bwd(xhat, r, dy, g), dy * xhat


def _perm(a, dil):
    if dil == 1:
        return a
    S = a.shape[0]
    return a.reshape(S // dil, dil, *a.shape[1:]).swapaxes(0, 1).reshape(a.shape)


def _unperm(a, dil):
    if dil == 1:
        return a
    S = a.shape[0]
    return a.reshape(dil, S // dil, *a.shape[1:]).swapaxes(0, 1).reshape(a.shape)


def _place():
    x, y, c = lax.axis_index("x"), lax.axis_index("y"), lax.axis_index("c")
    chips = [(1 - x, y), (x, 1 - y), (1 - x, 1 - y)]
    return x, y, c, chips


def _remote(src, dst, send, recv, dev):
    return pltpu.make_async_remote_copy(src_ref=src, dst_ref=dst, send_sem=send, recv_sem=recv,
                                        device_id=dev, device_id_type=MESH)


ANY = pl.BlockSpec(memory_space=pl.ANY)


class _Xfer:
    def __init__(self, send, recv):
        self.send, self.recv = send, recv

    def start(self):
        self.send().start()

    def wait(self):
        self.recv().wait_recv()
        self.send().wait_send()


def _sem_shapes(nsem):
    return [pltpu.SemaphoreType.DMA((nsem,))] * 2


def _carried(body, n_in, n_out, grid, carry):
    if carry is None:
        return body, [], [], [], [], []
    ins, out_shapes, plan, nsem = carry
    ci, co = len(ins), len(out_shapes)

    def new_body(*refs):
        a, cin = refs[:n_in], refs[n_in:n_in + ci]
        b = refs[n_in + ci:n_in + ci + n_out]
        cout = refs[n_in + ci + n_out:n_in + ci + n_out + co]
        rest = refs[n_in + ci + n_out + co:]
        scratch, sems = rest[:-2], rest[-2:]
        ids = [pl.program_id(d) for d in range(len(grid))]
        first = functools.reduce(jnp.logical_and, [i == 0 for i in ids])
        last = functools.reduce(jnp.logical_and, [i == g - 1 for i, g in zip(ids, grid)])

        @pl.when(first)
        def _():
            for t in plan(cin, cout, *sems):
                t.start()

        body(*a, *b, *scratch)

        @pl.when(last)
        def _():
            for t in plan(cin, cout, *sems):
                t.wait()

    return new_body, list(ins), [ANY] * ci, list(out_shapes), [ANY] * co, _sem_shapes(nsem)


def _exchange(name, carry):
    ins, out_shapes, plan, nsem = carry
    n = len(ins)

    def body(*refs):
        items = plan(refs[:n], refs[n:n + len(out_shapes)], *refs[-2:])
        for t in items:
            t.start()
        for t in items:
            t.wait()

    return pl.pallas_call(body, name=name, in_specs=[ANY] * n, out_specs=[ANY] * len(out_shapes),
                          out_shape=list(out_shapes), scratch_shapes=_sem_shapes(nsem))(*ins)


def _plan_gather(ins, outs, send, recv):
    x, y, c, chips = _place()
    me = 2 * x + y
    items = []
    for w, (src, dst) in enumerate(zip(ins, outs)):
        for r, (px, py) in enumerate(chips):
            k = 3 * w + r
            to = functools.partial(_remote, src, dst.at[me], send.at[k], recv.at[k], (px, py, c))
            frm = functools.partial(_remote, src, dst.at[2 * px + py], send.at[k], recv.at[k], (px, py, c))
            items.append(_Xfer(send=to, recv=frm))
    return items


def _carry_gather(shards):
    shapes = [jax.ShapeDtypeStruct((N_CHIPS,) + a.shape, a.dtype) for a in shards]
    return (shards, shapes, _plan_gather, 3 * len(shards))


def _plan_pair_exchange(ins, outs, send, recv):
    x, y, c, _ = _place()
    items = []
    for w, (src, dst) in enumerate(zip(ins, outs)):
        hr = src.shape[1] // 2
        theirs = pl.ds(pl.multiple_of((1 - c) * hr, 8), hr)
        cp = functools.partial(_remote, src.at[:, theirs, :], dst, send.at[w], recv.at[w], (x, y, 1 - c))
        items.append(_Xfer(send=cp, recv=cp))
    return items


def _carry_pair_exchange(gs):
    shapes = [jax.ShapeDtypeStruct((a.shape[0], a.shape[1] // 2, a.shape[2]), a.dtype) for a in gs]
    return (gs, shapes, _plan_pair_exchange, len(gs))


def _plan_chip_exchange(ins, outs, send, recv, base=0):
    x, y, c, chips = _place()
    items = []
    for w, (src, dst) in enumerate(zip(ins, outs)):
        for r, (px, py) in enumerate(chips):
            k = base + 3 * w + r
            cp = functools.partial(_remote, src.at[2 * px + py], dst.at[r], send.at[k], recv.at[k], (px, py, c))
            items.append(_Xfer(send=cp, recv=cp))
    return items


def _carry_chip_exchange(qs):
    shapes = [jax.ShapeDtypeStruct((3,) + a.shape[1:], a.dtype) for a in qs]
    return (qs, shapes, _plan_chip_exchange, 3 * len(qs))


def _plan_pair_complete(ins, outs, send, recv):
    x, y, c, _ = _place()
    items = []
    for w, (src, dst) in enumerate(zip(ins, outs)):
        cp = functools.partial(_remote, src, dst, send.at[w], recv.at[w], (x, y, 1 - c))
        items.append(_Xfer(send=cp, recv=cp))
    return items


def _carry_pair_complete(rs):
    shapes = [jax.ShapeDtypeStruct(a.shape, a.dtype) for a in rs]
    return (rs, shapes, _plan_pair_complete, len(rs))


def _carry_complete_and_chip_exchange(rs, qs):
    ins_a, shapes_a, plan_a, n_a = _carry_pair_complete(rs)
    ins_b, shapes_b, plan_b, n_b = _carry_chip_exchange(qs)

    def plan(ins, outs, send, recv):
        return (plan_a(ins[:len(ins_a)], outs[:len(shapes_a)], send, recv)
                + plan_b(ins[len(ins_a):], outs[len(shapes_a):], send, recv, base=n_a))

    return (list(ins_a) + list(ins_b), shapes_a + shapes_b, plan, n_a + n_b)


def _join_halves(mine, other):
    first = lax.axis_index("c") == 0
    return jnp.concatenate([jnp.where(first, mine, other), jnp.where(first, other, mine)])


def _allgather_weights(ws):
    n = len(ws)

    def body(*refs):
        ins, outs = refs[:n], refs[n:2 * n]
        send, recv = refs[2 * n:]
        x, y, c, chips = _place()
        me = 2 * x + y
        sib = (x, y, 1 - c)
        sends = []
        for w in range(n):
            hr = ins[w].shape[0] // 2
            mine = pl.ds(pl.multiple_of(c * hr, 16), hr)
            for r, (px, py) in enumerate(chips):
                cp = _remote(ins[w].at[mine, :], outs[w].at[me, mine, :], send.at[w, r], recv.at[w, r], (px, py, c))
                cp.start()
                sends.append(cp)
        for w in range(n):
            hr = ins[w].shape[0] // 2
            mine = pl.ds(pl.multiple_of(c * hr, 16), hr)
            for r, (px, py) in enumerate(chips):
                blk = outs[w].at[2 * px + py, mine, :]
                _remote(blk, blk, send.at[w, r], recv.at[w, r], (px, py, c)).wait_recv()
                cp = _remote(blk, blk, send.at[w, 3 + r], recv.at[w, 3 + r], sib)
                cp.start()
                sends.append(cp)
        for w in range(n):
            hr = ins[w].shape[0] // 2
            other = pl.ds(pl.multiple_of((1 - c) * hr, 16), hr)
            for r, (px, py) in enumerate(chips):
                blk = outs[w].at[2 * px + py, other, :]
                _remote(blk, blk, send.at[w, 3 + r], recv.at[w, 3 + r], sib).wait_recv()
        for cp in sends:
            cp.wait_send()

    out_shape = [jax.ShapeDtypeStruct((N_CHIPS,) + a.shape, a.dtype) for a in ws]
    return pl.pallas_call(
        body, name="allgather_weights", in_specs=[ANY] * n, out_specs=[ANY] * n, out_shape=out_shape,
        scratch_shapes=[pltpu.SemaphoreType.DMA((n, 6)), pltpu.SemaphoreType.DMA((n, 6))],
    )(*ws)


def _allreduce_small(vec, loss_rows, loss_scale):
    rows = vec.shape[0]

    def body(v_ref, tot_ref, loss_ref, gat_ref, send, recv):
        x, y, c, _ = _place()
        me = 4 * x + 2 * y + c
        gat_ref[me] = v_ref[...]
        peers = []
        for r in range(1, N_DEV):
            px = 1 - x if r & 4 else x
            py = 1 - y if r & 2 else y
            pc = 1 - c if r & 1 else c
            peers.append((px, py, pc))
            _remote(v_ref, gat_ref.at[me], send.at[r - 1], recv.at[r - 1], (px, py, pc)).start()
        for r, (px, py, pc) in enumerate(peers):
            _remote(v_ref, gat_ref.at[4 * px + 2 * py + pc], send.at[r], recv.at[r], (px, py, pc)).wait_recv()
        for r, (px, py, pc) in enumerate(peers):
            _remote(v_ref, gat_ref.at[me], send.at[r], recv.at[r], (px, py, pc)).wait_send()
        tot = gat_ref[0]
        for d in range(1, N_DEV):
            tot = tot + gat_ref[d]
        tot_ref[...] = tot
        loss_ref[...] = jnp.full(loss_ref.shape, loss_scale, F32) * jnp.sum(tot[:loss_rows])

    vm = pl.BlockSpec(memory_space=pltpu.VMEM)
    tot, loss = pl.pallas_call(
        body, name="allreduce_small", in_specs=[vm], out_specs=[vm, vm],
        out_shape=[jax.ShapeDtypeStruct((rows, LANE), F32), jax.ShapeDtypeStruct((8, LANE), F32)],
        scratch_shapes=[pltpu.VMEM((N_DEV, rows, LANE), F32), pltpu.SemaphoreType.DMA((N_DEV - 1,)),
                        pltpu.SemaphoreType.DMA((N_DEV - 1,))],
        compiler_params=_params(),
    )(vec)
    return tot, loss


def _add_rows(name, terms):
    shape = terms[0].shape
    C = shape[-1]
    flat = [t.reshape(-1, C) for t in terms]
    rows = flat[0].shape[0]
    width = _tile(C, 1024)

    def fn(*vals):
        tot = vals[0]
        for val in vals[1:]:
            tot = tot + val
        return tot

    (out,) = _ew(name, fn, [(t, "t", 0) for t in flat], [(C, F32)], rows=rows, width=width, ncols=C // width,
                 ts=512)
    return out.reshape(shape)


def _adamw(name, w, g, m, v):
    shape = w.shape
    C = shape[-1]
    flat = [t.reshape(-1, C) for t in (w, g, m, v)]
    rows = flat[0].shape[0]
    width = _tile(C, 1024)

    def fn(w, g, m, v):
        m2 = ADAM_B1 * m + (1.0 - ADAM_B1) * g
        v2 = ADAM_B2 * v + (1.0 - ADAM_B2) * (g * g)
        m_hat = m2 / (1.0 - ADAM_B1 ** ADAM_STEP)
        v_hat = v2 / (1.0 - ADAM_B2 ** ADAM_STEP)
        delta = -ADAM_LR * (m_hat / (jnp.sqrt(v_hat) + ADAM_EPS) + ADAM_WD * w)
        return delta, m2, v2

    outs = _ew(name, fn, [(t, "t", 0) for t in flat], [(C, F32)] * 3, rows=rows, width=width,
               ncols=C // width, ts=min(512, rows))
    return [o.reshape(shape) for o in outs]


def _na_geometry(S):
    rows = S // GRID_W
    nq = rows // NA_QR
    tq = NA_QR * GRID_W
    kw = NA_KR * GRID_W

    def kstart(i):
        return jnp.clip(NA_QR * i - NA_WIN_R // 2, 0, rows - NA_KR) * GRID_W

    def var(i):
        return jnp.where(i == 0, 0, jnp.where(i == nq - 1, 2, 1))

    return dict(S=S, tq=tq, kw=kw, kc=kw, kstart=kstart, var=var, mode="table")


def _dil_geometry(S, dil):
    L = S // dil
    tq = min(512, L)
    kw = tq + 2 * DIL_REACH
    return dict(S=S, tq=tq, kw=kw, kc=kw, kstart=lambda i: i * tq, mode="dil", dil=dil, L=L)


def _dil_slopes(g, kw):
    idx = jnp.arange(1, DIL_HEADS + 1, dtype=F32)[g * DIL_HPG:(g + 1) * DIL_HPG]
    slopes = 2.0 ** (-ALIBI_MAX_EXP * idx / DIL_HEADS)
    return jnp.broadcast_to(slopes[:, None, None], (DIL_HPG, 8, kw))


def _layer_fwd(x, p, cos, sin, next_shards=None):
    S, D = x.shape
    ts = 256
    gp = p["pre_g"][None, :]
    (h,) = _ew("pre_norm", lambda x, g: _rms_stats(x)[0] * g, [(x, "t", 0), (gp, "v", 0)], [(D, BF16)],
               rows=S, width=D, ncols=1, ts=ts)
    proj = _mm_nn("proj_in", h, p["w_in"], out_dtype=BF16)

    bias = _na_bias_build(p["rpb"])
    ya, lse_a = _attn_fwd("na_fwd", proj, C_QA, proj, C_KA, proj, C_VA, H=NA_HEADS, G=1, extra=bias,
                          **_na_geometry(S))

    def qk(name, off, nh, g, mult):
        fn = lambda x, g, cos, sin: _qk_fwd(x, g, cos, sin) * mult
        (out,) = _ew(name, fn, [(proj, "t", off), (g[None, :], "vf", 0), (cos, "tf", 0), (sin, "tf", 0)],
                     [(nh * HEAD, BF16)], rows=S, width=HEAD, ncols=nh, ts=512)
        return out

    qb = qk("q_norm_rope", C_QB, GQ_HEADS, p["q_g"], HEAD ** -0.5 * float(np.log2(np.e)))
    kb = qk("k_norm_rope", C_KB, GKV_HEADS, p["k_g"], 1.0)
    geo_b = dict(S=S, tq=min(GQA_FWD_TILE[0], S), kw=S, kc=min(GQA_FWD_TILE[1], S), kstart=lambda i: 0,
                 prescaled=True)
    carry = None if next_shards is None else _carry_gather(next_shards)
    yb, lse_b, *next_full = _attn_fwd("gqa_fwd", qb, 0, kb, 0, proj, C_VB, H=GQ_HEADS, G=GQ_HEADS // GKV_HEADS,
                                      carry=carry, **geo_b)

    W = DIL_HPG * HEAD
    og, lg, saved_c = [], [], []
    for g, (win, dil) in enumerate(DIL_GROUPS):
        assert (win // 2) // dil == DIL_REACH
        sl = lambda off: proj[:, (off + g * DIL_HPG) * HEAD:(off + (g + 1) * DIL_HPG) * HEAD]
        qg = _perm(sl(C_QC), dil)
        kg = jnp.pad(_perm(sl(C_KC), dil), ((DIL_REACH, DIL_REACH), (0, 0)))
        vg = jnp.pad(_perm(sl(C_VC), dil), ((DIL_REACH, DIL_REACH), (0, 0)))
        geo = _dil_geometry(S, dil)
        o, l = _attn_fwd(f"dil{g}_fwd", qg, 0, kg, 0, vg, 0, H=DIL_HPG, G=1, extra=_dil_slopes(g, geo["kw"]),
                         out_dtype=F32, **geo)
        og.append(_unperm(o, dil))
        lg.append(_unperm(l, dil))
        saved_c.append((qg, kg, vg))

    def combine(o1, o2, o3, l1, l2, l3):
        m = jnp.maximum(jnp.maximum(l1, l2), l3)
        e1, e2, e3 = jnp.exp(l1 - m), jnp.exp(l2 - m), jnp.exp(l3 - m)
        tot = e1 + e2 + e3
        return (e1 * o1 + e2 * o2 + e3 * o3) / tot, m + jnp.log(tot)

    yc, lse_c = _ew("dil_combine", combine, [(a, "t", 0) for a in og + lg], [(W, BF16), (W, F32)],
                    rows=S, width=W, ncols=1, ts=512)

    def gate(y, z):
        z = z.astype(F32)
        return y.astype(F32) * (z * _sigmoid(z))

    def branch(name, y, zoff, width, w):
        unit = width // HEAD
        (u,) = _ew(name + "_silu", gate, [(y, "t", 0), (proj, "tf", zoff // unit)], [(width, BF16)],
                   rows=S, width=width, ncols=1, ts=512)
        return u, _mm_nn(name + "_proj", u, w, out_dtype=BF16)

    ua, ta = branch("branch_a", ya, C_ZA, NA_HEADS * HEAD, p["w_a"])
    ub, tb = branch("branch_b", yb, C_ZB, GQ_HEADS * HEAD, p["w_b"])
    uc, tc = branch("branch_c", yc, C_ZC, W, p["w_c"])

    mw = 512
    nm = D // mw
    goff = C_GATE * HEAD // mw
    bg = p["b_gate"][None, :]

    def merge(ta, tb, tc, ga, gb, gc, ba, bb, bc):
        sa = _sigmoid(ga.astype(F32) + ba)
        sb = _sigmoid(gb.astype(F32) + bb)
        sc = _sigmoid(gc.astype(F32) + bc)
        return sa * ta.astype(F32) + sb * tb.astype(F32) + sc * tc.astype(F32)

    (merged,) = _ew("merge", merge,
                    [(ta, "t", 0), (tb, "t", 0), (tc, "t", 0),
                     (proj, "t", goff), (proj, "t", goff + nm), (proj, "t", goff + 2 * nm),
                     (bg, "v", 0), (bg, "v", nm), (bg, "v", 2 * nm)],
                    [(D, BF16)], rows=S, width=mw, ncols=nm, ts=512)
    out = _mm_nn("proj_out", merged, p["w_out"], out_dtype=F32)
    gq = p["post_g"][None, :]
    (x_next,) = _ew("post_norm_residual", lambda x, o, g: x + _rms_stats(o)[0] * g,
                    [(x, "t", 0), (out, "t", 0), (gq, "v", 0)], [(D, F32)], rows=S, width=D, ncols=1, ts=ts)
    saved = dict(x=x, h=h, proj=proj, bias=bias, ya=ya, lse_a=lse_a, qb=qb, kb=kb, yb=yb, lse_b=lse_b,
                 geo_b=geo_b, saved_c=saved_c, yc=yc, lse_c=lse_c, ua=ua, ub=ub, uc=uc, ta=ta, tb=tb, tc=tc,
                 merged=merged, out=out)
    return x_next, saved, (_fill_own(next_full, next_shards) if next_full else None)


def _pair_sums(gs, got):
    c = lax.axis_index("c")
    out = []
    for g, t in zip(gs, got):
        hr = g.shape[1] // 2
        out.append(_add_rows("grad_pair_sum", [lax.dynamic_slice_in_dim(g, c * hr, hr, axis=1), t]))
    return out


def _chip_sums(pair, recv):
    me = 2 * lax.axis_index("x") + lax.axis_index("y")
    out = []
    for q, t in zip(pair, recv):
        own = lax.dynamic_index_in_dim(q, me, axis=0, keepdims=False)
        out.append(_add_rows("grad_chip_sum", [own, t[0], t[1], t[2]]))
    return out


def _fill_own(gathered, shards):
    me = 2 * lax.axis_index("x") + lax.axis_index("y")
    return [lax.dynamic_update_slice_in_dim(g, a[None], me, axis=0) for g, a in zip(gathered, shards)]


def _layer_bwd(gy, p, s, cos, sin, pending=None, reduce_own=False):
    S, D = gy.shape
    ts = 256
    proj = s["proj"]
    grads = {}

    def post_bwd(o, dy, g):
        xhat, r = _rms_stats(o)
        return _rms_bwd(xhat, r, dy, g), dy * xhat

    dout, dg_post = _ew("post_norm_bwd", post_bwd, [(s["out"], "t", 0), (gy, "t", 0), (p["post_g"][None, :], "v", 0)],
                        [(D, BF16)], accs=("n",), rows=S, width=D, ncols=1, ts=ts)
    grads["post_g"] = dg_post[0]
    dmerged = _mm_nn("proj_out_bwd", dout, p["w_out_t"], out_dtype=BF16)
    grads["w_out"] = _mm_tn("proj_out_wgrad", s["merged"], dout, blocks=1).reshape(N_CHIPS, D // N_CHIPS, D)

    mw = 512
    nm = D // mw
    goff = C_GATE * HEAD // mw
    bg = p["b_gate"][None, :]

    def merge_bwd(dm, ta, tb, tc, ga, gb, gc, ba, bb, bc):
        dm = dm.astype(F32)
        res_t, res_g = [], []
        for t, g, b in ((ta, ga, ba), (tb, gb, bb), (tc, gc, bc)):
            sg = _sigmoid(g.astype(F32) + b)
            res_t.append(dm * sg)
            res_g.append(dm * t.astype(F32) * sg * (1.0 - sg))
        return (*res_t, *res_g, *res_g)

    mres = _ew("merge_bwd", merge_bwd,
               [(dmerged, "t", 0), (s["ta"], "t", 0), (s["tb"], "t", 0), (s["tc"], "t", 0),
                (proj, "t", goff), (proj, "t", goff + nm), (proj, "t", goff + 2 * nm),
                (bg, "v", 0), (bg, "v", nm), (bg, "v", 2 * nm)],
               [(D, BF16)] * 6, accs=("n", "n", "n"), rows=S, width=mw, ncols=nm, ts=512,
               carry=None if pending is None else _carry_pair_exchange(pending))
    dta, dtb, dtc, dga, dgb, dgc = mres[:6]
    grads["b_gate"] = jnp.concatenate([a[0] for a in mres[6:9]])
    pair = None if pending is None else _pair_sums(pending, mres[9:])

    def silu_bwd(du, y, z):
        du, y, z = du.astype(F32), y.astype(F32), z.astype(F32)
        sg = _sigmoid(z)
        return du * (z * sg), du * y * (sg * (1.0 + z * (1.0 - sg)))

    def branch_bwd(name, dt, u, y, zoff, width, w_t):
        du = _mm_nn(name + "_proj_bwd", dt, w_t, out_dtype=BF16)
        gw = _mm_tn(name + "_wgrad", u, dt, blocks=N_CHIPS)
        unit = width // HEAD
        dy, dz = _ew(name + "_silu_bwd", silu_bwd, [(du, "t", 0), (y, "t", 0), (proj, "tf", zoff // unit)],
                     [(width, BF16), (width, BF16)], rows=S, width=width, ncols=1, ts=512)
        return dy, dz, gw

    W = DIL_HPG * HEAD
    dya, dza, grads["w_a"] = branch_bwd("branch_a", dta, s["ua"], s["ya"], C_ZA, NA_HEADS * HEAD, p["w_a_t"])
    dyb, dzb, grads["w_b"] = branch_bwd("branch_b", dtb, s["ub"], s["yb"], C_ZB, GQ_HEADS * HEAD, p["w_b_t"])
    dyc, dzc, grads["w_c"] = branch_bwd("branch_c", dtc, s["uc"], s["yc"], C_ZC, W, p["w_c_t"])

    dqa, dka, dva, dbias = _attn_bwd("na_bwd", proj, C_QA, proj, C_KA, proj, C_VA, s["ya"], dya, s["lse_a"],
                                     H=NA_HEADS, G=1, extra=s["bias"], **_na_geometry(S))
    grads["rpb"] = _na_bias_grad(dbias)

    geo_b = dict(s["geo_b"])
    geo_b["tq"], geo_b["kc"] = min(GQA_BWD_TILE[0], S), min(GQA_BWD_TILE[1], S)
    dqr, dkr, dvb, *recv = _attn_bwd("gqa_bwd", s["qb"], 0, s["kb"], 0, proj, C_VB, s["yb"], dyb, s["lse_b"],
                                     H=GQ_HEADS, G=GQ_HEADS // GKV_HEADS,
                                     carry=None if pending is None else _carry_chip_exchange(pair), **geo_b)
    half = None if pending is None else _chip_sums(pair, recv)

    def qk_bwd(name, dyr, off, nh, g):
        return _ew(name, _qk_bwd, [(dyr, "t", 0), (proj, "t", off), (g[None, :], "vf", 0), (cos, "tf", 0),
                                   (sin, "tf", 0)],
                   [(nh * HEAD, BF16)], accs=("f",), rows=S, width=HEAD, ncols=nh, ts=512)

    dqb, dgq = qk_bwd("q_norm_rope_bwd", dqr, C_QB, GQ_HEADS, p["q_g"])
    dkb, dgk = qk_bwd("k_norm_rope_bwd", dkr, C_KB, GKV_HEADS, p["k_g"])
    grads["q_g"] = dgq[0]
    grads["k_g"] = dgk[0]

    dqc, dkc, dvc = [], [], []
    for g, (win, dil) in enumerate(DIL_GROUPS):
        qg, kg, vg = s["saved_c"][g]
        geo = _dil_geometry(S, dil)
        dq, dk, dv = _attn_bwd(f"dil{g}_bwd", qg, 0, kg, 0, vg, 0, _perm(s["yc"], dil), _perm(dyc, dil),
                               _perm(s["lse_c"], dil), H=DIL_HPG, G=1, extra=_dil_slopes(g, geo["kw"]), **geo)
        dqc.append(_unperm(dq, dil))
        dkc.append(_unperm(dk[DIL_REACH:-DIL_REACH], dil).astype(BF16))
        dvc.append(_unperm(dv[DIL_REACH:-DIL_REACH], dil).astype(BF16))

    dproj = jnp.concatenate(
        [dqa, dka.astype(BF16), dva.astype(BF16), dqb, dkb, dvb.astype(BF16), *dqc, *dkc, *dvc,
         dza, dzb, dzc, dga, dgb, dgc], axis=1)
    grads["w_in"] = _mm_tn("proj_in_wgrad", s["h"], dproj, blocks=N_CHIPS, ts=1024)
    if reduce_own:
        assert pending is not None
        own = [grads[n] for n in BIG]
        own_pair = _pair_sums(own, _exchange("grad_pair_exchange", _carry_pair_exchange(own)))
        carry = _carry_complete_and_chip_exchange(half, own_pair)
    else:
        carry = None if pending is None else _carry_pair_complete(half)
    res = _mm_deep("proj_in_bwd", dproj, p["w_in_t"], out_dtype=F32, tk_cap=2560, carry=carry)
    done, own_done = None, None
    if carry is None:
        dh = res
    else:
        dh, rest = res[0], res[1:]
        done = [_join_halves(m, o) for m, o in zip(half, rest[:len(half)])]
    if reduce_own:
        own_half = _chip_sums(own_pair, rest[len(half):])
        other = _exchange("grad_pair_complete", _carry_pair_complete(own_half))
        own_done = [_join_halves(m, o) for m, o in zip(own_half, other)]

    def pre_bwd(x, dh, g, gy):
        xhat, r = _rms_stats(x)
        return gy + _rms_bwd(xhat, r, dh, g), dh * xhat

    dx, dg_pre = _ew("pre_norm_bwd", pre_bwd, [(s["x"], "t", 0), (dh, "t", 0), (p["pre_g"][None, :], "v", 0),
                                                 (gy, "t", 0)],
                     [(D, F32)], accs=("n",), rows=S, width=D, ncols=1, ts=ts)
    grads["pre_g"] = dg_pre[0]
    return dx, grads, done, own_done


BIG = ("w_in", "w_a", "w_b", "w_c", "w_out")
SMALL = ("pre_g", "b_gate", "q_g", "k_g", "rpb", "post_g")


def _pack(parts, rows_mult=8):
    flat = jnp.concatenate([a.reshape(-1) for a in parts])
    n = flat.shape[0]
    unit = rows_mult * LANE
    padded = -(-n // unit) * unit
    return jnp.pad(flat, (0, padded - n)).reshape(-1, LANE)


def kernel(x, pre_norm_g, w_in, b_gate, q_norm_g, k_norm_g, rpb, w_branch_a, w_branch_b, w_branch_c, w_out, post_norm_g, loss_target, m_pre_norm_g, m_w_in, m_b_gate, m_q_norm_g, m_k_norm_g, m_rpb, m_w_branch_a, m_w_branch_b, m_w_branch_c, m_w_out, m_post_norm_g, v_pre_norm_g, v_w_in, v_b_gate, v_q_norm_g, v_k_norm_g, v_rpb, v_w_branch_a, v_w_branch_b, v_w_branch_c, v_w_out, v_post_norm_g):
    n_layers = w_in.shape[0]
    S, D = x.shape[1], x.shape[2]
    big_w = dict(w_in=w_in, w_a=w_branch_a, w_b=w_branch_b, w_c=w_branch_c, w_out=w_out)
    big_m = dict(w_in=m_w_in, w_a=m_w_branch_a, w_b=m_w_branch_b, w_c=m_w_branch_c, w_out=m_w_out)
    big_v = dict(w_in=v_w_in, w_a=v_w_branch_a, w_b=v_w_branch_b, w_c=v_w_branch_c, w_out=v_w_out)
    small_w = dict(pre_g=pre_norm_g, b_gate=b_gate, q_g=q_norm_g, k_g=k_norm_g, rpb=rpb, post_g=post_norm_g)
    small_m = dict(pre_g=m_pre_norm_g, b_gate=m_b_gate, q_g=m_q_norm_g, k_g=m_k_norm_g, rpb=m_rpb,
                   post_g=m_post_norm_g)
    small_v = dict(pre_g=v_pre_norm_g, b_gate=v_b_gate, q_g=v_q_norm_g, k_g=v_k_norm_g, rpb=v_rpb,
                   post_g=v_post_norm_g)

    shards = [[big_w[n][l].astype(BF16) for n in BIG] for l in range(n_layers)]
    cos, sin = _rope_tables(S)

    def layer_params(l, full):
        p = dict(zip(BIG, full))
        for n in ("w_in", "w_a", "w_b", "w_c"):
            p[n + "_t"] = jnp.swapaxes(p[n], 1, 2).reshape(-1, p[n].shape[1])
        p["w_out"] = p["w_out"].reshape(D, D)
        p["w_out_t"] = p["w_out"].T
        p.update(pre_g=pre_norm_g[l], b_gate=b_gate[l], q_g=q_norm_g[l], k_g=k_norm_g[l], rpb=rpb[l],
                 post_g=post_norm_g[l])
        return p

    full = _fill_own(_allgather_weights(shards[0]), shards[0])
    act = x[0]
    saved, params = [], []
    for l in range(n_layers):
        params.append(layer_params(l, full))
        act, s, full = _layer_fwd(act, params[l], cos, sin, shards[l + 1] if l + 1 < n_layers else None)
        saved.append(s)

    dy, loss_cols = _ew("loss", lambda y, t: ((y - t) * (1.0 / D), (y - t) * (y - t)),
                        [(act, "t", 0), (loss_target[0], "t", 0)], [(D, F32)], accs=("n",),
                        rows=S, width=D, ncols=1, ts=256)

    big_g = {n: [None] * n_layers for n in BIG}
    small_g = {n: [None] * n_layers for n in SMALL}
    pending = None
    for l in reversed(range(n_layers)):
        dy, grads, done, own_done = _layer_bwd(dy, params[l], saved[l], cos, sin, pending, reduce_own=(l == 0))
        if done is not None:
            for n, r in zip(BIG, done):
                big_g[n][l + 1] = r
        pending = [grads[n] for n in BIG]
        for n in SMALL:
            small_g[n][l] = grads[n]
    for n, r in zip(BIG, own_done):
        big_g[n][0] = r

    loss_rows = -(-D // (8 * LANE)) * 8
    parts = [_pack([loss_cols[0]])] + [_pack([jnp.stack(small_g[n]) for n in SMALL])]
    tot, loss = _allreduce_small(jnp.concatenate(parts), loss_rows, 0.5 / D)
    wvec = jnp.concatenate([jnp.zeros((loss_rows, LANE), F32), _pack([small_w[n] for n in SMALL])])
    mvec = jnp.concatenate([jnp.zeros((loss_rows, LANE), F32), _pack([small_m[n] for n in SMALL])])
    vvec = jnp.concatenate([jnp.ones((loss_rows, LANE), F32), _pack([small_v[n] for n in SMALL])])
    small_out = [tot] + _adamw("adamw_small", wvec, tot, mvec, vvec)

    def unpack(vec):
        flat = vec[loss_rows:].reshape(-1)
        res, pos = {}, 0
        for n in SMALL:
            size = int(np.prod(small_w[n].shape))
            res[n] = flat[pos:pos + size].reshape(small_w[n].shape)
            pos += size
        return res

    small_res = [unpack(v) for v in small_out]

    big_res = [{}, {}, {}, {}]
    for n in BIG:
        g = jnp.stack(big_g[n])
        big_res[0][n] = g
        for k, val in enumerate(_adamw("adamw_" + n, big_w[n], g, big_m[n], big_v[n])):
            big_res[k + 1][n] = val

    order = (("pre_g", small_res), ("w_in", big_res), ("b_gate", small_res), ("q_g", small_res),
             ("k_g", small_res), ("rpb", small_res), ("w_a", big_res), ("w_b", big_res), ("w_c", big_res),
             ("w_out", big_res), ("post_g", small_res))
    outs = [loss[0, 0], dy[None]]
    for k in range(4):
        outs.extend(src[k][n] for n, src in order)
    return tuple(outs)
```

```python
import functools

import numpy as np
import jax
import jax.numpy as jnp
from jax import lax
from jax.experimental import pallas as pl
from jax.experimental.pallas import tpu as pltpu

F32 = jnp.float32
BF16 = jnp.bfloat16
MESH = pl.DeviceIdType.MESH

HEAD = 128
GRID_W = 64
EPS = 1e-6
NEG = -1e30
NA_HEADS = 8
NA_WIN_R = 8
NA_WIN_C = 16
GQ_HEADS = 8
GKV_HEADS = 2
ROPE_BASE = 10000.0
DIL_GROUPS = ((128, 1), (512, 4), (2048, 16))
DIL_REACH = 64
DIL_HPG = 4
DIL_HEADS = 12
ALIBI_MAX_EXP = 8.0
ADAM_LR = 0.001
ADAM_B1 = 0.9
ADAM_B2 = 0.999
ADAM_EPS = 1e-08
ADAM_WD = 0.01
ADAM_STEP = 10

C_QA, C_KA, C_VA = 0, 8, 16
C_QB, C_KB, C_VB = 24, 32, 34
C_QC, C_KC, C_VC = 36, 48, 60
C_ZA, C_ZB, C_ZC = 72, 80, 88
C_GATE = 92

NA_QR = 4
NA_KR = NA_QR + NA_WIN_R
NA_VARIANTS = 3

GQA_FWD_TILE = (1024, 2048)
GQA_BWD_TILE = (1024, 1024)

VMEM_LIMIT_BYTES = 48 * 1024 * 1024
LANE = 128
N_CHIPS = 4
N_DEV = 8


def _params(sem=None):
    kw = dict(vmem_limit_bytes=VMEM_LIMIT_BYTES)
    if sem is not None:
        kw["dimension_semantics"] = sem
    return pltpu.CompilerParams(**kw)


def _tile(n, cap):
    units = n // LANE
    assert units * LANE == n, n
    best = 1
    for d in range(1, units + 1):
        if units % d == 0 and d * LANE <= cap:
            best = d
    return best * LANE


def _ew(name, fn, ins, outs, accs=(), *, rows, width, ncols, ts, carry=None):
    ts = min(ts, rows)
    assert rows % ts == 0
    nr = rows // ts
    ni, no = len(ins), len(outs)

    def imap(mode, off):
        if mode == "t":
            return lambda n, i: (i, off + n)
        if mode == "tf":
            return lambda n, i: (i, off)
        if mode == "v":
            return lambda n, i: (0, off + n)
        return lambda n, i: (0, off)

    in_specs = [pl.BlockSpec((ts if m in ("t", "tf") else 1, width), imap(m, o)) for (_, m, o) in ins]
    out_specs = [pl.BlockSpec((ts, width), lambda n, i: (i, n)) for _ in outs]
    out_shape = [jax.ShapeDtypeStruct((rows, c), d) for (c, d) in outs]
    for a in accs:
        if a == "n":
            out_specs.append(pl.BlockSpec((8, width), lambda n, i: (0, n)))
            out_shape.append(jax.ShapeDtypeStruct((8, width * ncols), F32))
        else:
            out_specs.append(pl.BlockSpec((8, width), lambda n, i: (0, 0)))
            out_shape.append(jax.ShapeDtypeStruct((8, width), F32))

    def body(*refs):
        res = fn(*[r[...] for r in refs[:ni]])
        if not isinstance(res, tuple):
            res = (res,)
        for r, val in zip(refs[ni:ni + no], res[:no]):
            r[...] = val.astype(r.dtype)
        n = pl.program_id(0)
        i = pl.program_id(1)
        for kind, r, val in zip(accs, refs[ni + no:], res[no:]):
            first = (i == 0) if kind == "n" else jnp.logical_and(i == 0, n == 0)

            @pl.when(first)
            def _(r=r):
                r[...] = jnp.zeros(r.shape, r.dtype)

            r[...] += jnp.broadcast_to(jnp.sum(val.astype(F32), axis=0, keepdims=True), r.shape)

    grid = (ncols, nr)
    body, c_in, c_in_specs, c_out, c_out_specs, c_scratch = _carried(body, ni, no + len(accs), grid, carry)
    res = pl.pallas_call(
        body, name=name, grid=grid, in_specs=in_specs + c_in_specs, out_specs=out_specs + c_out_specs,
        out_shape=out_shape + c_out, scratch_shapes=c_scratch,
        compiler_params=_params(("arbitrary", "arbitrary")),
    )(*[a for (a, _, _) in ins], *c_in)
    return res


def _mm_nn(name, a, w, *, out_dtype, tm=1024, tn_cap=1024):
    M, K = a.shape
    blocked = w.ndim == 3
    NB = w.shape[-1]
    N = NB * (w.shape[0] if blocked else 1)
    tm = min(tm, M)
    tn = _tile(NB, tn_cap)
    per = NB // tn
    if blocked:
        w_spec = pl.BlockSpec((None, K, tn), lambda i, n: (n // per, 0, n % per))
    else:
        w_spec = pl.BlockSpec((K, tn), lambda i, n: (0, n))

    def body(a_ref, w_ref, o_ref):
        o_ref[...] = jnp.dot(a_ref[...], w_ref[...], preferred_element_type=F32).astype(o_ref.dtype)

    return pl.pallas_call(
        body, name=name, grid=(M // tm, N // tn),
        in_specs=[pl.BlockSpec((tm, K), lambda i, n: (i, 0)), w_spec],
        out_specs=pl.BlockSpec((tm, tn), lambda i, n: (i, n)),
        out_shape=jax.ShapeDtypeStruct((M, N), out_dtype),
        compiler_params=_params(("arbitrary", "arbitrary")),
    )(a, w)


def _mm_deep(name, a, w, *, out_dtype, tm=512, tk_cap=1024, carry=None):
    M, N = a.shape
    K = w.shape[1]
    assert w.shape[0] == N
    tm = min(tm, M)
    tk = _tile(N, tk_cap)
    nk = N // tk
    w_spec = pl.BlockSpec((tk, K), lambda i, k: (k, 0))

    def body(a_ref, w_ref, o_ref, acc_ref):
        k = pl.program_id(1)

        @pl.when(k == 0)
        def _():
            acc_ref[...] = jnp.zeros(acc_ref.shape, F32)

        acc_ref[...] += jnp.dot(a_ref[...], w_ref[...], preferred_element_type=F32)

        @pl.when(k == nk - 1)
        def _():
            o_ref[...] = acc_ref[...].astype(o_ref.dtype)

    grid = (M // tm, nk)
    body, c_in, c_in_specs, c_out, c_out_specs, c_scratch = _carried(body, 2, 1, grid, carry)
    res = pl.pallas_call(
        body, name=name, grid=grid,
        in_specs=[pl.BlockSpec((tm, tk), lambda i, k: (i, k)), w_spec] + c_in_specs,
        out_specs=[pl.BlockSpec((tm, K), lambda i, k: (i, 0))] + c_out_specs,
        out_shape=[jax.ShapeDtypeStruct((M, K), out_dtype)] + c_out,
        scratch_shapes=[pltpu.VMEM((tm, K), F32)] + c_scratch,
        compiler_params=_params(("arbitrary", "arbitrary")),
    )(a, w, *c_in)
    return res[0] if carry is None else res


def _mm_tn(name, a, b, *, blocks, ts=2048, tn_cap=1024):
    S, K = a.shape
    N = b.shape[1]
    NB = N // blocks
    ts = min(ts, S)
    tn = _tile(NB, tn_cap)
    per = NB // tn
    ns = S // ts

    def body(a_ref, b_ref, o_ref):
        s = pl.program_id(1)

        @pl.when(s == 0)
        def _():
            o_ref[...] = jnp.zeros(o_ref.shape, F32)

        o_ref[...] += lax.dot_general(a_ref[...], b_ref[...], (((0,), (0,)), ((), ())),
                                      preferred_element_type=F32)

    return pl.pallas_call(
        body, name=name, grid=(N // tn, ns),
        in_specs=[pl.BlockSpec((ts, K), lambda n, s: (s, 0)), pl.BlockSpec((ts, tn), lambda n, s: (s, n))],
        out_specs=pl.BlockSpec((None, K, tn), lambda n, s: (n // per, 0, n % per)),
        out_shape=jax.ShapeDtypeStruct((blocks, K, NB), F32),
        compiler_params=_params(("arbitrary", "arbitrary")),
    )(a, b)


def _aligned(off):
    return off if isinstance(off, int) else pl.multiple_of(off, LANE)


def _attn_bias(s, mode, extra_ref, i, tq, kc, dil, L):
    if mode == "table":
        return s + extra_ref[...]
    if mode == "dil":
        r = lax.broadcasted_iota(jnp.int32, (tq, kc), 0)
        tk = lax.broadcasted_iota(jnp.int32, (tq, kc), 1) - DIL_REACH
        dist = jnp.abs(r - tk)
        kin = (i * tq) % L + tk
        valid = (dist <= DIL_REACH) & (kin >= 0) & (kin < L)
        s = s - extra_ref[0:1, :] * (dil * dist).astype(F32)
        return jnp.where(valid, s, NEG)
    return s


def _attn_fwd(name, q, qo, k, ko, v, vo, *, H, G, S, tq, kw, kc, kstart, mode=None, extra=None,
              var=None, dil=1, L=1, out_dtype=BF16, prescaled=False, carry=None):
    nq = S // tq
    nch = kw // kc
    Sk = k.shape[0]
    scale = HEAD ** -0.5
    has_extra = mode is not None
    exp = jnp.exp2 if prescaled else jnp.exp
    log = jnp.log2 if prescaled else jnp.log

    def body(*refs):
        if has_extra:
            q_ref, k_ref, v_ref, e_ref, o_ref, l_ref = refs
        else:
            q_ref, k_ref, v_ref, o_ref, l_ref = refs
            e_ref = None
        i = pl.program_id(1)
        qb = q_ref[...]
        start = kstart(i)

        def chunk(j, carry):
            m, l, acc = carry
            off = _aligned(start + j * kc)
            kb = k_ref[pl.ds(off, kc), :]
            vb = v_ref[pl.ds(off, kc), :]
            s = lax.dot_general(qb, kb, (((1,), (1,)), ((), ())), preferred_element_type=F32)
            if not prescaled:
                s = s * scale
            s = _attn_bias(s, mode, e_ref, i, tq, kc, dil, L)
            mn = jnp.maximum(m, jnp.max(s, axis=1, keepdims=True))
            p = exp(s - mn)
            a = exp(m - mn)
            l = a * l + jnp.sum(p, axis=1, keepdims=True)
            acc = a * acc + jnp.dot(p.astype(BF16), vb, preferred_element_type=F32)
            return mn, l, acc

        init = (jnp.full((tq, 1), -3.0e38, F32), jnp.zeros((tq, 1), F32), jnp.zeros((tq, HEAD), F32))
        if nch == 1:
            m, l, acc = chunk(0, init)
        else:
            m, l, acc = lax.fori_loop(0, nch, chunk, init)
        o_ref[...] = (acc / l).astype(o_ref.dtype)
        l_ref[...] = jnp.broadcast_to(m + log(l), (tq, HEAD))

    in_specs = [
        pl.BlockSpec((tq, HEAD), lambda h, i: (i, qo + h)),
        pl.BlockSpec((Sk, HEAD), lambda h, i: (0, ko + h // G)),
        pl.BlockSpec((Sk, HEAD), lambda h, i: (0, vo + h // G)),
    ]
    args = [q, k, v]
    if mode == "table":
        in_specs.append(pl.BlockSpec((None, None, tq, kw), lambda h, i: (var(i), h, 0, 0)))
        args.append(extra)
    elif mode == "dil":
        in_specs.append(pl.BlockSpec((None, 8, kw), lambda h, i: (h, 0, 0)))
        args.append(extra)
    grid = (H, nq)
    body, c_in, c_in_specs, c_out, c_out_specs, c_scratch = _carried(body, len(args), 2, grid, carry)
    return pl.pallas_call(
        body, name=name, grid=grid, in_specs=in_specs + c_in_specs,
        out_specs=[pl.BlockSpec((tq, HEAD), lambda h, i: (i, h)),
                   pl.BlockSpec((tq, HEAD), lambda h, i: (i, h))] + c_out_specs,
        out_shape=[jax.ShapeDtypeStruct((S, H * HEAD), out_dtype),
                   jax.ShapeDtypeStruct((S, H * HEAD), F32)] + c_out,
        scratch_shapes=c_scratch,
        compiler_params=_params(("arbitrary", "arbitrary")),
    )(*args, *c_in)


def _attn_bwd(name, q, qo, k, ko, v, vo, o, do, lse, *, H, G, S, tq, kw, kc, kstart, mode=None, extra=None,
              var=None, dil=1, L=1, prescaled=False, carry=None):
    nq = S // tq
    nch = kw // kc
    Sk = k.shape[0]
    HK = H // G
    scale = HEAD ** -0.5
    dk_mult = float(np.log(2.0)) if prescaled else scale
    exp = jnp.exp2 if prescaled else jnp.exp
    has_extra = mode is not None
    table = mode == "table"
    if table:
        assert nq >= NA_VARIANTS and nch == 1

    def body(*refs):
        refs = list(refs)
        q_ref, k_ref, v_ref, o_ref, do_ref, l_ref = refs[:6]
        e_ref = refs[6] if has_extra else None
        outs = refs[7:] if has_extra else refs[6:]
        dq_ref, dk_ref, dv_ref = outs[:3]
        h = pl.program_id(0)
        i = pl.program_id(1)

        @pl.when(jnp.logical_and(h % G == 0, i == 0))
        def _():
            dk_ref[...] = jnp.zeros(dk_ref.shape, F32)
            dv_ref[...] = jnp.zeros(dv_ref.shape, F32)

        if table:
            db_ref = outs[3]

            @pl.when((i == 0) | (i == 1) | (i == nq - 1))
            def _():
                db_ref[...] = jnp.zeros(db_ref.shape, F32)

        qb = q_ref[...]
        dob = do_ref[...]
        delta = jnp.sum(o_ref[...].astype(F32) * dob.astype(F32), axis=1, keepdims=True)
        lse_b = l_ref[:, 0:1]
        start = kstart(i)

        def chunk(j, dq):
            off = _aligned(start + j * kc)
            kb = k_ref[pl.ds(off, kc), :]
            vb = v_ref[pl.ds(off, kc), :]
            s = lax.dot_general(qb, kb, (((1,), (1,)), ((), ())), preferred_element_type=F32)
            if not prescaled:
                s = s * scale
            s = _attn_bias(s, mode, e_ref, i, tq, kc, dil, L)
            p = exp(s - lse_b)
            dp = lax.dot_general(dob, vb, (((1,), (1,)), ((), ())), preferred_element_type=F32)
            ds = p * (dp - delta)
            dsb = ds.astype(BF16)
            dk_ref[pl.ds(off, kc), :] += dk_mult * lax.dot_general(
                dsb, qb, (((0,), (0,)), ((), ())), preferred_element_type=F32)
            dv_ref[pl.ds(off, kc), :] += lax.dot_general(
                p.astype(BF16), dob, (((0,), (0,)), ((), ())), preferred_element_type=F32)
            if table:
                db_ref[...] += ds
            return dq + jnp.dot(dsb, kb, preferred_element_type=F32)

        dq0 = jnp.zeros((tq, HEAD), F32)
        dq = chunk(0, dq0) if nch == 1 else lax.fori_loop(0, nch, chunk, dq0)
        dq_ref[...] = (dq * scale).astype(dq_ref.dtype)

    in_specs = [
        pl.BlockSpec((tq, HEAD), lambda h, i: (i, qo + h)),
        pl.BlockSpec((Sk, HEAD), lambda h, i: (0, ko + h // G)),
        pl.BlockSpec((Sk, HEAD), lambda h, i: (0, vo + h // G)),
        pl.BlockSpec((tq, HEAD), lambda h, i: (i, h)),
        pl.BlockSpec((tq, HEAD), lambda h, i: (i, h)),
        pl.BlockSpec((tq, HEAD), lambda h, i: (i, h)),
    ]
    args = [q, k, v, o, do, lse]
    out_specs = [
        pl.BlockSpec((tq, HEAD), lambda h, i: (i, h)),
        pl.BlockSpec((Sk, HEAD), lambda h, i: (0, h // G)),
        pl.BlockSpec((Sk, HEAD), lambda h, i: (0, h // G)),
    ]
    out_shape = [
        jax.ShapeDtypeStruct((S, H * HEAD), BF16),
        jax.ShapeDtypeStruct((Sk, HK * HEAD), F32),
        jax.ShapeDtypeStruct((Sk, HK * HEAD), F32),
    ]
    if table:
        in_specs.append(pl.BlockSpec((None, None, tq, kw), lambda h, i: (var(i), h, 0, 0)))
        args.append(extra)
        out_specs.append(pl.BlockSpec((None, None, tq, kw), lambda h, i: (var(i), h, 0, 0)))
        out_shape.append(jax.ShapeDtypeStruct(extra.shape, F32))
    elif mode == "dil":
        in_specs.append(pl.BlockSpec((None, 8, kw), lambda h, i: (h, 0, 0)))
        args.append(extra)
    grid = (H, nq)
    body, c_in, c_in_specs, c_out, c_out_specs, c_scratch = _carried(body, len(args), len(out_shape), grid, carry)
    return pl.pallas_call(
        body, name=name, grid=grid, in_specs=in_specs + c_in_specs, out_specs=out_specs + c_out_specs,
        out_shape=out_shape + c_out, scratch_shapes=c_scratch,
        compiler_params=_params(("arbitrary", "arbitrary")),
    )(*args, *c_in)


def _na_constants():
    pairs = NA_VARIANTS * NA_QR * NA_KR
    sel = np.zeros((pairs, 16), np.float32)
    row_ok = np.zeros((pairs, 1), np.float32)
    for var in range(NA_VARIANTS):
        for qr in range(NA_QR):
            for kr in range(NA_KR):
                p = (var * NA_QR + qr) * NA_KR + kr
                first_key = (0, qr, NA_QR)[var]
                dr = kr - qr + (NA_WIN_R - 1) - (0, NA_QR, 2 * NA_QR)[var]
                if first_key <= kr < first_key + NA_WIN_R:
                    assert 0 <= dr < 2 * NA_WIN_R - 1
                    sel[p, dr] = 1.0
                    row_ok[p, 0] = 1.0
    j = np.arange(GRID_W)[:, None]
    c = np.arange(GRID_W)[None, :]
    cs = np.clip(j - NA_WIN_C // 2, 0, GRID_W - NA_WIN_C)
    col_ok = ((c >= cs) & (c < cs + NA_WIN_C)).astype(np.float32).reshape(1, GRID_W * GRID_W)
    dc = np.clip(c - j + NA_WIN_C - 1, 0, 2 * NA_WIN_C - 2).reshape(-1)
    toe = np.zeros((32, GRID_W * GRID_W), np.float32)
    toe[dc, np.arange(GRID_W * GRID_W)] = 1.0
    return sel, row_ok, col_ok, toe


def _split3(x):
    hi = x.astype(BF16)
    r1 = x - hi.astype(F32)
    mid = r1.astype(BF16)
    lo = (r1 - mid.astype(F32)).astype(BF16)
    return hi, mid, lo


def _dot_sel_left(onehot_bf16, x):
    return sum(jnp.dot(onehot_bf16, t, preferred_element_type=F32) for t in _split3(x))


def _dot_sel_right(x, onehot_bf16):
    return sum(jnp.dot(t, onehot_bf16, preferred_element_type=F32) for t in _split3(x))


def _na_bias_build(rpb_l):
    sel, row_ok, col_ok, toe = _na_constants()
    pairs = sel.shape[0]
    ww = GRID_W * GRID_W
    t = jnp.pad(rpb_l, ((0, 0), (0, 1), (0, 1)))

    def body(t_ref, sel_ref, toe_ref, rok_ref, cok_ref, o_ref):
        rows = _dot_sel_right(t_ref[...], toe_ref[...])
        val = _dot_sel_left(sel_ref[...], rows)
        ok = rok_ref[...] * cok_ref[...]
        o_ref[...] = jnp.where(ok > 0.5, val, NEG)

    full = lambda shape: pl.BlockSpec(shape, lambda h: (0,) * len(shape))
    out = pl.pallas_call(
        body, name="na_bias_build", grid=(NA_HEADS,),
        in_specs=[pl.BlockSpec((None, 16, 32), lambda h: (h, 0, 0)), full((pairs, 16)), full((32, ww)),
                  full((pairs, 1)), full((1, ww))],
        out_specs=pl.BlockSpec((None, pairs, ww), lambda h: (h, 0, 0)),
        out_shape=jax.ShapeDtypeStruct((NA_HEADS, pairs, ww), F32),
        compiler_params=_params(("arbitrary",)),
    )(t, jnp.asarray(sel, BF16), jnp.asarray(toe, BF16), jnp.asarray(row_ok), jnp.asarray(col_ok))
    out = out.reshape(NA_HEADS, NA_VARIANTS, NA_QR, NA_KR, GRID_W, GRID_W)
    out = out.transpose(1, 0, 2, 4, 3, 5)
    return out.reshape(NA_VARIANTS, NA_HEADS, NA_QR * GRID_W, NA_KR * GRID_W)


def _na_bias_grad(dbias):
    sel, _, _, toe = _na_constants()
    pairs = sel.shape[0]
    ww = GRID_W * GRID_W
    d = dbias.reshape(NA_VARIANTS, NA_HEADS, NA_QR, GRID_W, NA_KR, GRID_W).transpose(1, 0, 2, 4, 3, 5)
    d = d.reshape(NA_HEADS, pairs, ww)

    def body(d_ref, selt_ref, toet_ref, o_ref):
        rows = _dot_sel_left(selt_ref[...], d_ref[...])
        o_ref[...] = _dot_sel_right(rows, toet_ref[...])

    full = lambda shape: pl.BlockSpec(shape, lambda h: (0,) * len(shape))
    out = pl.pallas_call(
        body, name="na_bias_grad", grid=(NA_HEADS,),
        in_specs=[pl.BlockSpec((None, pairs, ww), lambda h: (h, 0, 0)), full((16, pairs)), full((ww, 32))],
        out_specs=pl.BlockSpec((None, 16, 32), lambda h: (h, 0, 0)),
        out_shape=jax.ShapeDtypeStruct((NA_HEADS, 16, 32), F32),
        compiler_params=_params(("arbitrary",)),
    )(d, jnp.asarray(sel.T, BF16), jnp.asarray(toe.T, BF16))
    return out[:, :2 * NA_WIN_R - 1, :2 * NA_WIN_C - 1]


def _sigmoid(x):
    return 1.0 / (1.0 + jnp.exp(-x))


def _swap32(y):
    lane = lax.broadcasted_iota(jnp.int32, y.shape, 1)
    up = pltpu.roll(y, HEAD - 32, 1)
    down = pltpu.roll(y, 32, 1)
    return jnp.where((lane // 32) % 2 == 0, up, down)


def _rope_tables(S):
    t = jnp.arange(S, dtype=jnp.int32)
    freqs = ROPE_BASE ** (-jnp.arange(32, dtype=F32) / 32)
    ar = (t // GRID_W).astype(F32)[:, None] * freqs[None, :]
    ac = (t % GRID_W).astype(F32)[:, None] * freqs[None, :]
    cos = jnp.concatenate([jnp.cos(ar), jnp.cos(ar), jnp.cos(ac), jnp.cos(ac)], axis=1)
    sin = jnp.concatenate([-jnp.sin(ar), jnp.sin(ar), -jnp.sin(ac), jnp.sin(ac)], axis=1)
    return cos, sin


def _rms_stats(x):
    x = x.astype(F32)
    r = lax.rsqrt(jnp.mean(x * x, axis=-1, keepdims=True) + EPS)
    return x * r, r


def _rms_bwd(xhat, r, dy, g):
    dxh = dy * g
    return r * (dxh - xhat * jnp.mean(dxh * xhat, axis=-1, keepdims=True))


def _qk_fwd(x, g, cos, sin):
    xhat, _ = _rms_stats(x)
    y = xhat * g
    return y * cos + _swap32(y) * sin


def _qk_bwd(dyr, x, g, cos, sin):
    dyr = dyr.astype(F32)
    dy = dyr * cos + _swap32(dyr * sin)
    xhat, r = _rms_stats(x)
    return _rms_bwd(xhat, r, dy, g), dy * xhat


def _perm(a, dil):
    if dil == 1:
        return a
    S = a.shape[0]
    return a.reshape(S // dil, dil, *a.shape[1:]).swapaxes(0, 1).reshape(a.shape)


def _unperm(a, dil):
    if dil == 1:
        return a
    S = a.shape[0]
    return a.reshape(dil, S // dil, *a.shape[1:]).swapaxes(0, 1).reshape(a.shape)


def _place():
    x, y, c = lax.axis_index("x"), lax.axis_index("y"), lax.axis_index("c")
    chips = [(1 - x, y), (x, 1 - y), (1 - x, 1 - y)]
    return x, y, c, chips


def _remote(src, dst, send, recv, dev):
    return pltpu.make_async_remote_copy(src_ref=src, dst_ref=dst, send_sem=send, recv_sem=recv,
                                        device_id=dev, device_id_type=MESH)


ANY = pl.BlockSpec(memory_space=pl.ANY)


class _Xfer:
    def __init__(self, send, recv):
        self.send, self.recv = send, recv

    def start(self):
        self.send().start()

    def wait(self):
        self.recv().wait_recv()
        self.send().wait_send()


def _sem_shapes(nsem):
    return [pltpu.SemaphoreType.DMA((nsem,))] * 2


def _carried(body, n_in, n_out, grid, carry):
    if carry is None:
        return body, [], [], [], [], []
    ins, out_shapes, plan, nsem = carry
    ci, co = len(ins), len(out_shapes)

    def new_body(*refs):
        a, cin = refs[:n_in], refs[n_in:n_in + ci]
        b = refs[n_in + ci:n_in + ci + n_out]
        cout = refs[n_in + ci + n_out:n_in + ci + n_out + co]
        rest = refs[n_in + ci + n_out + co:]
        scratch, sems = rest[:-2], rest[-2:]
        ids = [pl.program_id(d) for d in range(len(grid))]
        first = functools.reduce(jnp.logical_and, [i == 0 for i in ids])
        last = functools.reduce(jnp.logical_and, [i == g - 1 for i, g in zip(ids, grid)])

        @pl.when(first)
        def _():
            for t in plan(cin, cout, *sems):
                t.start()

        body(*a, *b, *scratch)

        @pl.when(last)
        def _():
            for t in plan(cin, cout, *sems):
                t.wait()

    return new_body, list(ins), [ANY] * ci, list(out_shapes), [ANY] * co, _sem_shapes(nsem)


def _exchange(name, carry):
    ins, out_shapes, plan, nsem = carry
    n = len(ins)

    def body(*refs):
        items = plan(refs[:n], refs[n:n + len(out_shapes)], *refs[-2:])
        for t in items:
            t.start()
        for t in items:
            t.wait()

    return pl.pallas_call(body, name=name, in_specs=[ANY] * n, out_specs=[ANY] * len(out_shapes),
                          out_shape=list(out_shapes), scratch_shapes=_sem_shapes(nsem))(*ins)


def _plan_gather(ins, outs, send, recv):
    x, y, c, chips = _place()
    me = 2 * x + y
    items = []
    for w, (src, dst) in enumerate(zip(ins, outs)):
        for r, (px, py) in enumerate(chips):
            k = 3 * w + r
            to = functools.partial(_remote, src, dst.at[me], send.at[k], recv.at[k], (px, py, c))
            frm = functools.partial(_remote, src, dst.at[2 * px + py], send.at[k], recv.at[k], (px, py, c))
            items.append(_Xfer(send=to, recv=frm))
    return items


def _carry_gather(shards):
    shapes = [jax.ShapeDtypeStruct((N_CHIPS,) + a.shape, a.dtype) for a in shards]
    return (shards, shapes, _plan_gather, 3 * len(shards))


def _plan_pair_exchange(ins, outs, send, recv):
    x, y, c, _ = _place()
    items = []
    for w, (src, dst) in enumerate(zip(ins, outs)):
        hr = src.shape[1] // 2
        theirs = pl.ds(pl.multiple_of((1 - c) * hr, 8), hr)
        cp = functools.partial(_remote, src.at[:, theirs, :], dst, send.at[w], recv.at[w], (x, y, 1 - c))
        items.append(_Xfer(send=cp, recv=cp))
    return items


def _carry_pair_exchange(gs):
    shapes = [jax.ShapeDtypeStruct((a.shape[0], a.shape[1] // 2, a.shape[2]), a.dtype) for a in gs]
    return (gs, shapes, _plan_pair_exchange, len(gs))


def _plan_chip_exchange(ins, outs, send, recv, base=0):
    x, y, c, chips = _place()
    items = []
    for w, (src, dst) in enumerate(zip(ins, outs)):
        for r, (px, py) in enumerate(chips):
            k = base + 3 * w + r
            cp = functools.partial(_remote, src.at[2 * px + py], dst.at[r], send.at[k], recv.at[k], (px, py, c))
            items.append(_Xfer(send=cp, recv=cp))
    return items


def _carry_chip_exchange(qs):
    shapes = [jax.ShapeDtypeStruct((3,) + a.shape[1:], a.dtype) for a in qs]
    return (qs, shapes, _plan_chip_exchange, 3 * len(qs))


def _plan_pair_complete(ins, outs, send, recv):
    x, y, c, _ = _place()
    items = []
    for w, (src, dst) in enumerate(zip(ins, outs)):
        cp = functools.partial(_remote, src, dst, send.at[w], recv.at[w], (x, y, 1 - c))
        items.append(_Xfer(send=cp, recv=cp))
    return items


def _carry_pair_complete(rs):
    shapes = [jax.ShapeDtypeStruct(a.shape, a.dtype) for a in rs]
    return (rs, shapes, _plan_pair_complete, len(rs))


def _carry_complete_and_chip_exchange(rs, qs):
    ins_a, shapes_a, plan_a, n_a = _carry_pair_complete(rs)
    ins_b, shapes_b, plan_b, n_b = _carry_chip_exchange(qs)

    def plan(ins, outs, send, recv):
        return (plan_a(ins[:len(ins_a)], outs[:len(shapes_a)], send, recv)
                + plan_b(ins[len(ins_a):], outs[len(shapes_a):], send, recv, base=n_a))

    return (list(ins_a) + list(ins_b), shapes_a + shapes_b, plan, n_a + n_b)


def _join_halves(mine, other):
    first = lax.axis_index("c") == 0
    return jnp.concatenate([jnp.where(first, mine, other), jnp.where(first, other, mine)])


def _allgather_weights(ws):
    n = len(ws)

    def body(*refs):
        ins, outs = refs[:n], refs[n:2 * n]
        send, recv = refs[2 * n:]
        x, y, c, chips = _place()
        me = 2 * x + y
        sib = (x, y, 1 - c)
        sends = []
        for w in range(n):
            hr = ins[w].shape[0] // 2
            mine = pl.ds(pl.multiple_of(c * hr, 16), hr)
            for r, (px, py) in enumerate(chips):
                cp = _remote(ins[w].at[mine, :], outs[w].at[me, mine, :], send.at[w, r], recv.at[w, r], (px, py, c))
                cp.start()
                sends.append(cp)
        for w in range(n):
            hr = ins[w].shape[0] // 2
            mine = pl.ds(pl.multiple_of(c * hr, 16), hr)
            for r, (px, py) in enumerate(chips):
                blk = outs[w].at[2 * px + py, mine, :]
                _remote(blk, blk, send.at[w, r], recv.at[w, r], (px, py, c)).wait_recv()
                cp = _remote(blk, blk, send.at[w, 3 + r], recv.at[w, 3 + r], sib)
                cp.start()
                sends.append(cp)
        for w in range(n):
            hr = ins[w].shape[0] // 2
            other = pl.ds(pl.multiple_of((1 - c) * hr, 16), hr)
            for r, (px, py) in enumerate(chips):
                blk = outs[w].at[2 * px + py, other, :]
                _remote(blk, blk, send.at[w, 3 + r], recv.at[w, 3 + r], sib).wait_recv()
        for cp in sends:
            cp.wait_send()

    out_shape = [jax.ShapeDtypeStruct((N_CHIPS,) + a.shape, a.dtype) for a in ws]
    return pl.pallas_call(
        body, name="allgather_weights", in_specs=[ANY] * n, out_specs=[ANY] * n, out_shape=out_shape,
        scratch_shapes=[pltpu.SemaphoreType.DMA((n, 6)), pltpu.SemaphoreType.DMA((n, 6))],
    )(*ws)


def _allreduce_small(vec, loss_rows, loss_scale):
    rows = vec.shape[0]

    def body(v_ref, tot_ref, loss_ref, gat_ref, send, recv):
        x, y, c, _ = _place()
        me = 4 * x + 2 * y + c
        gat_ref[me] = v_ref[...]
        peers = []
        for r in range(1, N_DEV):
            px = 1 - x if r & 4 else x
            py = 1 - y if r & 2 else y
            pc = 1 - c if r & 1 else c
            peers.append((px, py, pc))
            _remote(v_ref, gat_ref.at[me], send.at[r - 1], recv.at[r - 1], (px, py, pc)).start()
        for r, (px, py, pc) in enumerate(peers):
            _remote(v_ref, gat_ref.at[4 * px + 2 * py + pc], send.at[r], recv.at[r], (px, py, pc)).wait_recv()
        for r, (px, py, pc) in enumerate(peers):
            _remote(v_ref, gat_ref.at[me], send.at[r], recv.at[r], (px, py, pc)).wait_send()
        tot = gat_ref[0]
        for d in range(1, N_DEV):
            tot = tot + gat_ref[d]
        tot_ref[...] = tot
        loss_ref[...] = jnp.full(loss_ref.shape, loss_scale, F32) * jnp.sum(tot[:loss_rows])

    vm = pl.BlockSpec(memory_space=pltpu.VMEM)
    tot, loss = pl.pallas_call(
        body, name="allreduce_small", in_specs=[vm], out_specs=[vm, vm],
        out_shape=[jax.ShapeDtypeStruct((rows, LANE), F32), jax.ShapeDtypeStruct((8, LANE), F32)],
        scratch_shapes=[pltpu.VMEM((N_DEV, rows, LANE), F32), pltpu.SemaphoreType.DMA((N_DEV - 1,)),
                        pltpu.SemaphoreType.DMA((N_DEV - 1,))],
        compiler_params=_params(),
    )(vec)
    return tot, loss


def _add_rows(name, terms):
    shape = terms[0].shape
    C = shape[-1]
    flat = [t.reshape(-1, C) for t in terms]
    rows = flat[0].shape[0]
    width = _tile(C, 1024)

    def fn(*vals):
        tot = vals[0]
        for val in vals[1:]:
            tot = tot + val
        return tot

    (out,) = _ew(name, fn, [(t, "t", 0) for t in flat], [(C, F32)], rows=rows, width=width, ncols=C // width,
                 ts=512)
    return out.reshape(shape)


def _adamw(name, w, g, m, v):
    shape = w.shape
    C = shape[-1]
    flat = [t.reshape(-1, C) for t in (w, g, m, v)]
    rows = flat[0].shape[0]
    width = _tile(C, 1024)

    def fn(w, g, m, v):
        m2 = ADAM_B1 * m + (1.0 - ADAM_B1) * g
        v2 = ADAM_B2 * v + (1.0 - ADAM_B2) * (g * g)
        m_hat = m2 / (1.0 - ADAM_B1 ** ADAM_STEP)
        v_hat = v2 / (1.0 - ADAM_B2 ** ADAM_STEP)
        delta = -ADAM_LR * (m_hat / (jnp.sqrt(v_hat) + ADAM_EPS) + ADAM_WD * w)
        return delta, m2, v2

    outs = _ew(name, fn, [(t, "t", 0) for t in flat], [(C, F32)] * 3, rows=rows, width=width,
               ncols=C // width, ts=min(512, rows))
    return [o.reshape(shape) for o in outs]


def _na_geometry(S):
    rows = S // GRID_W
    nq = rows // NA_QR
    tq = NA_QR * GRID_W
    kw = NA_KR * GRID_W

    def kstart(i):
        return jnp.clip(NA_QR * i - NA_WIN_R // 2, 0, rows - NA_KR) * GRID_W

    def var(i):
        return jnp.where(i == 0, 0, jnp.where(i == nq - 1, 2, 1))

    return dict(S=S, tq=tq, kw=kw, kc=kw, kstart=kstart, var=var, mode="table")


def _dil_geometry(S, dil):
    L = S // dil
    tq = min(512, L)
    kw = tq + 2 * DIL_REACH
    return dict(S=S, tq=tq, kw=kw, kc=kw, kstart=lambda i: i * tq, mode="dil", dil=dil, L=L)


def _dil_slopes(g, kw):
    idx = jnp.arange(1, DIL_HEADS + 1, dtype=F32)[g * DIL_HPG:(g + 1) * DIL_HPG]
    slopes = 2.0 ** (-ALIBI_MAX_EXP * idx / DIL_HEADS)
    return jnp.broadcast_to(slopes[:, None, None], (DIL_HPG, 8, kw))


def _layer_fwd(x, p, cos, sin, next_shards=None):
    S, D = x.shape
    ts = 256
    gp = p["pre_g"][None, :]
    (h,) = _ew("pre_norm", lambda x, g: _rms_stats(x)[0] * g, [(x, "t", 0), (gp, "v", 0)], [(D, BF16)],
               rows=S, width=D, ncols=1, ts=ts)
    proj = _mm_nn("proj_in", h, p["w_in"], out_dtype=BF16)

    bias = _na_bias_build(p["rpb"])
    ya, lse_a = _attn_fwd("na_fwd", proj, C_QA, proj, C_KA, proj, C_VA, H=NA_HEADS, G=1, extra=bias,
                          **_na_geometry(S))

    def qk(name, off, nh, g, mult):
        fn = lambda x, g, cos, sin: _qk_fwd(x, g, cos, sin) * mult
        (out,) = _ew(name, fn, [(proj, "t", off), (g[None, :], "vf", 0), (cos, "tf", 0), (sin, "tf", 0)],
                     [(nh * HEAD, BF16)], rows=S, width=HEAD, ncols=nh, ts=512)
        return out

    qb = qk("q_norm_rope", C_QB, GQ_HEADS, p["q_g"], HEAD ** -0.5 * float(np.log2(np.e)))
    kb = qk("k_norm_rope", C_KB, GKV_HEADS, p["k_g"], 1.0)
    geo_b = dict(S=S, tq=min(GQA_FWD_TILE[0], S), kw=S, kc=min(GQA_FWD_TILE[1], S), kstart=lambda i: 0,
                 prescaled=True)
    carry = None if next_shards is None else _carry_gather(next_shards)
    yb, lse_b, *next_full = _attn_fwd("gqa_fwd", qb, 0, kb, 0, proj, C_VB, H=GQ_HEADS, G=GQ_HEADS // GKV_HEADS,
                                      carry=carry, **geo_b)

    W = DIL_HPG * HEAD
    og, lg, saved_c = [], [], []
    for g, (win, dil) in enumerate(DIL_GROUPS):
        assert (win // 2) // dil == DIL_REACH
        sl = lambda off: proj[:, (off + g * DIL_HPG) * HEAD:(off + (g + 1) * DIL_HPG) * HEAD]
        qg = _perm(sl(C_QC), dil)
        kg = jnp.pad(_perm(sl(C_KC), dil), ((DIL_REACH, DIL_REACH), (0, 0)))
        vg = jnp.pad(_perm(sl(C_VC), dil), ((DIL_REACH, DIL_REACH), (0, 0)))
        geo = _dil_geometry(S, dil)
        o, l = _attn_fwd(f"dil{g}_fwd", qg, 0, kg, 0, vg, 0, H=DIL_HPG, G=1, extra=_dil_slopes(g, geo["kw"]),
                         out_dtype=F32, **geo)
        og.append(_unperm(o, dil))
        lg.append(_unperm(l, dil))
        saved_c.append((qg, kg, vg))

    def combine(o1, o2, o3, l1, l2, l3):
        m = jnp.maximum(jnp.maximum(l1, l2), l3)
        e1, e2, e3 = jnp.exp(l1 - m), jnp.exp(l2 - m), jnp.exp(l3 - m)
        tot = e1 + e2 + e3
        return (e1 * o1 + e2 * o2 + e3 * o3) / tot, m + jnp.log(tot)

    yc, lse_c = _ew("dil_combine", combine, [(a, "t", 0) for a in og + lg], [(W, BF16), (W, F32)],
                    rows=S, width=W, ncols=1, ts=512)

    def gate(y, z):
        z = z.astype(F32)
        return y.astype(F32) * (z * _sigmoid(z))

    def branch(name, y, zoff, width, w):
        unit = width // HEAD
        (u,) = _ew(name + "_silu", gate, [(y, "t", 0), (proj, "tf", zoff // unit)], [(width, BF16)],
                   rows=S, width=width, ncols=1, ts=512)
        return u, _mm_nn(name + "_proj", u, w, out_dtype=BF16)

    ua, ta = branch("branch_a", ya, C_ZA, NA_HEADS * HEAD, p["w_a"])
    ub, tb = branch("branch_b", yb, C_ZB, GQ_HEADS * HEAD, p["w_b"])
    uc, tc = branch("branch_c", yc, C_ZC, W, p["w_c"])

    mw = 512
    nm = D // mw
    goff = C_GATE * HEAD // mw
    bg = p["b_gate"][None, :]

    def merge(ta, tb, tc, ga, gb, gc, ba, bb, bc):
        sa = _sigmoid(ga.astype(F32) + ba)
        sb = _sigmoid(gb.astype(F32) + bb)
        sc = _sigmoid(gc.astype(F32) + bc)
        return sa * ta.astype(F32) + sb * tb.astype(F32) + sc * tc.astype(F32)

    (merged,) = _ew("merge", merge,
                    [(ta, "t", 0), (tb, "t", 0), (tc, "t", 0),
                     (proj, "t", goff), (proj, "t", goff + nm), (proj, "t", goff + 2 * nm),
                     (bg, "v", 0), (bg, "v", nm), (bg, "v", 2 * nm)],
                    [(D, BF16)], rows=S, width=mw, ncols=nm, ts=512)
    out = _mm_nn("proj_out", merged, p["w_out"], out_dtype=F32)
    gq = p["post_g"][None, :]
    (x_next,) = _ew("post_norm_residual", lambda x, o, g: x + _rms_stats(o)[0] * g,
                    [(x, "t", 0), (out, "t", 0), (gq, "v", 0)], [(D, F32)], rows=S, width=D, ncols=1, ts=ts)
    saved = dict(x=x, h=h, proj=proj, bias=bias, ya=ya, lse_a=lse_a, qb=qb, kb=kb, yb=yb, lse_b=lse_b,
                 geo_b=geo_b, saved_c=saved_c, yc=yc, lse_c=lse_c, ua=ua, ub=ub, uc=uc, ta=ta, tb=tb, tc=tc,
                 merged=merged, out=out)
    return x_next, saved, (_fill_own(next_full, next_shards) if next_full else None)


def _pair_sums(gs, got):
    c = lax.axis_index("c")
    out = []
    for g, t in zip(gs, got):
        hr = g.shape[1] // 2
        out.append(_add_rows("grad_pair_sum", [lax.dynamic_slice_in_dim(g, c * hr, hr, axis=1), t]))
    return out


def _chip_sums(pair, recv):
    me = 2 * lax.axis_index("x") + lax.axis_index("y")
    out = []
    for q, t in zip(pair, recv):
        own = lax.dynamic_index_in_dim(q, me, axis=0, keepdims=False)
        out.append(_add_rows("grad_chip_sum", [own, t[0], t[1], t[2]]))
    return out


def _fill_own(gathered, shards):
    me = 2 * lax.axis_index("x") + lax.axis_index("y")
    return [lax.dynamic_update_slice_in_dim(g, a[None], me, axis=0) for g, a in zip(gathered, shards)]


def _layer_bwd(gy, p, s, cos, sin, pending=None, reduce_own=False):
    S, D = gy.shape
    ts = 256
    proj = s["proj"]
    grads = {}

    def post_bwd(o, dy, g):
        xhat, r = _rms_stats(o)
        return _rms_bwd(xhat, r, dy, g), dy * xhat

    dout, dg_post = _ew("post_norm_bwd", post_bwd, [(s["out"], "t", 0), (gy, "t", 0), (p["post_g"][None, :], "v", 0)],
                        [(D, BF16)], accs=("n",), rows=S, width=D, ncols=1, ts=ts)
    grads["post_g"] = dg_post[0]
    dmerged = _mm_nn("proj_out_bwd", dout, p["w_out_t"], out_dtype=BF16)
    grads["w_out"] = _mm_tn("proj_out_wgrad", s["merged"], dout, blocks=1, ts=1024).reshape(N_CHIPS, D // N_CHIPS, D)

    mw = 512
    nm = D // mw
    goff = C_GATE * HEAD // mw
    bg = p["b_gate"][None, :]

    def merge_bwd(dm, ta, tb, tc, ga, gb, gc, ba, bb, bc):
        dm = dm.astype(F32)
        res_t, res_g = [], []
        for t, g, b in ((ta, ga, ba), (tb, gb, bb), (tc, gc, bc)):
            sg = _sigmoid(g.astype(F32) + b)
            res_t.append(dm * sg)
            res_g.append(dm * t.astype(F32) * sg * (1.0 - sg))
        return (*res_t, *res_g, *res_g)

    mres = _ew("merge_bwd", merge_bwd,
               [(dmerged, "t", 0), (s["ta"], "t", 0), (s["tb"], "t", 0), (s["tc"], "t", 0),
                (proj, "t", goff), (proj, "t", goff + nm), (proj, "t", goff + 2 * nm),
                (bg, "v", 0), (bg, "v", nm), (bg, "v", 2 * nm)],
               [(D, BF16)] * 6, accs=("n", "n", "n"), rows=S, width=mw, ncols=nm, ts=512,
               carry=None if pending is None else _carry_pair_exchange(pending))
    dta, dtb, dtc, dga, dgb, dgc = mres[:6]
    grads["b_gate"] = jnp.concatenate([a[0] for a in mres[6:9]])
    pair = None if pending is None else _pair_sums(pending, mres[9:])

    def silu_bwd(du, y, z):
        du, y, z = du.astype(F32), y.astype(F32), z.astype(F32)
        sg = _sigmoid(z)
        return du * (z * sg), du * y * (sg * (1.0 + z * (1.0 - sg)))

    def branch_bwd(name, dt, u, y, zoff, width, w_t):
        du = _mm_nn(name + "_proj_bwd", dt, w_t, out_dtype=BF16)
        gw = _mm_tn(name + "_wgrad", u, dt, blocks=N_CHIPS)
        unit = width // HEAD
        dy, dz = _ew(name + "_silu_bwd", silu_bwd, [(du, "t", 0), (y, "t", 0), (proj, "tf", zoff // unit)],
                     [(width, BF16), (width, BF16)], rows=S, width=width, ncols=1, ts=512)
        return dy, dz, gw

    W = DIL_HPG * HEAD
    dya, dza, grads["w_a"] = branch_bwd("branch_a", dta, s["ua"], s["ya"], C_ZA, NA_HEADS * HEAD, p["w_a_t"])
    dyb, dzb, grads["w_b"] = branch_bwd("branch_b", dtb, s["ub"], s["yb"], C_ZB, GQ_HEADS * HEAD, p["w_b_t"])
    dyc, dzc, grads["w_c"] = branch_bwd("branch_c", dtc, s["uc"], s["yc"], C_ZC, W, p["w_c_t"])

    dqa, dka, dva, dbias = _attn_bwd("na_bwd", proj, C_QA, proj, C_KA, proj, C_VA, s["ya"], dya, s["lse_a"],
                                     H=NA_HEADS, G=1, extra=s["bias"], **_na_geometry(S))
    grads["rpb"] = _na_bias_grad(dbias)

    geo_b = dict(s["geo_b"])
    geo_b["tq"], geo_b["kc"] = min(GQA_BWD_TILE[0], S), min(GQA_BWD_TILE[1], S)
    dqr, dkr, dvb, *recv = _attn_bwd("gqa_bwd", s["qb"], 0, s["kb"], 0, proj, C_VB, s["yb"], dyb, s["lse_b"],
                                     H=GQ_HEADS, G=GQ_HEADS // GKV_HEADS,
                                     carry=None if pending is None else _carry_chip_exchange(pair), **geo_b)
    half = None if pending is None else _chip_sums(pair, recv)

    def qk_bwd(name, dyr, off, nh, g):
        return _ew(name, _qk_bwd, [(dyr, "t", 0), (proj, "t", off), (g[None, :], "vf", 0), (cos, "tf", 0),
                                   (sin, "tf", 0)],
                   [(nh * HEAD, BF16)], accs=("f",), rows=S, width=HEAD, ncols=nh, ts=512)

    dqb, dgq = qk_bwd("q_norm_rope_bwd", dqr, C_QB, GQ_HEADS, p["q_g"])
    dkb, dgk = qk_bwd("k_norm_rope_bwd", dkr, C_KB, GKV_HEADS, p["k_g"])
    grads["q_g"] = dgq[0]
    grads["k_g"] = dgk[0]

    dqc, dkc, dvc = [], [], []
    for g, (win, dil) in enumerate(DIL_GROUPS):
        qg, kg, vg = s["saved_c"][g]
        geo = _dil_geometry(S, dil)
        dq, dk, dv = _attn_bwd(f"dil{g}_bwd", qg, 0, kg, 0, vg, 0, _perm(s["yc"], dil), _perm(dyc, dil),
                               _perm(s["lse_c"], dil), H=DIL_HPG, G=1, extra=_dil_slopes(g, geo["kw"]), **geo)
        dqc.append(_unperm(dq, dil))
        dkc.append(_unperm(dk[DIL_REACH:-DIL_REACH], dil).astype(BF16))
        dvc.append(_unperm(dv[DIL_REACH:-DIL_REACH], dil).astype(BF16))

    dproj = jnp.concatenate(
        [dqa, dka.astype(BF16), dva.astype(BF16), dqb, dkb, dvb.astype(BF16), *dqc, *dkc, *dvc,
         dza, dzb, dzc, dga, dgb, dgc], axis=1)
    grads["w_in"] = _mm_tn("proj_in_wgrad", s["h"], dproj, blocks=N_CHIPS, ts=1024)
    if reduce_own:
        assert pending is not None
        own = [grads[n] for n in BIG]
        own_pair = _pair_sums(own, _exchange("grad_pair_exchange", _carry_pair_exchange(own)))
        carry = _carry_complete_and_chip_exchange(half, own_pair)
    else:
        carry = None if pending is None else _carry_pair_complete(half)
    res = _mm_deep("proj_in_bwd", dproj, p["w_in_t"], out_dtype=F32, tk_cap=2560, carry=carry)
    done, own_done = None, None
    if carry is None:
        dh = res
    else:
        dh, rest = res[0], res[1:]
        done = [_join_halves(m, o) for m, o in zip(half, rest[:len(half)])]
    if reduce_own:
        own_half = _chip_sums(own_pair, rest[len(half):])
        other = _exchange("grad_pair_complete", _carry_pair_complete(own_half))
        own_done = [_join_halves(m, o) for m, o in zip(own_half, other)]

    def pre_bwd(x, dh, g, gy):
        xhat, r = _rms_stats(x)
        return gy + _rms_bwd(xhat, r, dh, g), dh * xhat

    dx, dg_pre = _ew("pre_norm_bwd", pre_bwd, [(s["x"], "t", 0), (dh, "t", 0), (p["pre_g"][None, :], "v", 0),
                                                 (gy, "t", 0)],
                     [(D, F32)], accs=("n",), rows=S, width=D, ncols=1, ts=ts)
    grads["pre_g"] = dg_pre[0]
    return dx, grads, done, own_done


BIG = ("w_in", "w_a", "w_b", "w_c", "w_out")
SMALL = ("pre_g", "b_gate", "q_g", "k_g", "rpb", "post_g")


def _pack(parts, rows_mult=8):
    flat = jnp.concatenate([a.reshape(-1) for a in parts])
    n = flat.shape[0]
    unit = rows_mult * LANE
    padded = -(-n // unit) * unit
    return jnp.pad(flat, (0, padded - n)).reshape(-1, LANE)


def kernel(x, pre_norm_g, w_in, b_gate, q_norm_g, k_norm_g, rpb, w_branch_a, w_branch_b, w_branch_c, w_out, post_norm_g, loss_target, m_pre_norm_g, m_w_in, m_b_gate, m_q_norm_g, m_k_norm_g, m_rpb, m_w_branch_a, m_w_branch_b, m_w_branch_c, m_w_out, m_post_norm_g, v_pre_norm_g, v_w_in, v_b_gate, v_q_norm_g, v_k_norm_g, v_rpb, v_w_branch_a, v_w_branch_b, v_w_branch_c, v_w_out, v_post_norm_g):
    n_layers = w_in.shape[0]
    S, D = x.shape[1], x.shape[2]
    big_w = dict(w_in=w_in, w_a=w_branch_a, w_b=w_branch_b, w_c=w_branch_c, w_out=w_out)
    big_m = dict(w_in=m_w_in, w_a=m_w_branch_a, w_b=m_w_branch_b, w_c=m_w_branch_c, w_out=m_w_out)
    big_v = dict(w_in=v_w_in, w_a=v_w_branch_a, w_b=v_w_branch_b, w_c=v_w_branch_c, w_out=v_w_out)
    small_w = dict(pre_g=pre_norm_g, b_gate=b_gate, q_g=q_norm_g, k_g=k_norm_g, rpb=rpb, post_g=post_norm_g)
    small_m = dict(pre_g=m_pre_norm_g, b_gate=m_b_gate, q_g=m_q_norm_g, k_g=m_k_norm_g, rpb=m_rpb,
                   post_g=m_post_norm_g)
    small_v = dict(pre_g=v_pre_norm_g, b_gate=v_b_gate, q_g=v_q_norm_g, k_g=v_k_norm_g, rpb=v_rpb,
                   post_g=v_post_norm_g)

    shards = [[big_w[n][l].astype(BF16) for n in BIG] for l in range(n_layers)]
    cos, sin = _rope_tables(S)

    def layer_params(l, full):
        p = dict(zip(BIG, full))
        for n in ("w_in", "w_a", "w_b", "w_c"):
            p[n + "_t"] = jnp.swapaxes(p[n], 1, 2).reshape(-1, p[n].shape[1])
        p["w_out"] = p["w_out"].reshape(D, D)
        p["w_out_t"] = p["w_out"].T
        p.update(pre_g=pre_norm_g[l], b_gate=b_gate[l], q_g=q_norm_g[l], k_g=k_norm_g[l], rpb=rpb[l],
                 post_g=post_norm_g[l])
        return p

    full = _fill_own(_allgather_weights(shards[0]), shards[0])
    act = x[0]
    saved, params = [], []
    for l in range(n_layers):
        params.append(layer_params(l, full))
        act, s, full = _layer_fwd(act, params[l], cos, sin, shards[l + 1] if l + 1 < n_layers else None)
        saved.append(s)

    dy, loss_cols = _ew("loss", lambda y, t: ((y - t) * (1.0 / D), (y - t) * (y - t)),
                        [(act, "t", 0), (loss_target[0], "t", 0)], [(D, F32)], accs=("n",),
                        rows=S, width=D, ncols=1, ts=256)

    big_g = {n: [None] * n_layers for n in BIG}
    small_g = {n: [None] * n_layers for n in SMALL}
    pending = None
    for l in reversed(range(n_layers)):
        dy, grads, done, own_done = _layer_bwd(dy, params[l], saved[l], cos, sin, pending, reduce_own=(l == 0))
        if done is not None:
            for n, r in zip(BIG, done):
                big_g[n][l + 1] = r
        pending = [grads[n] for n in BIG]
        for n in SMALL:
            small_g[n][l] = grads[n]
    for n, r in zip(BIG, own_done):
        big_g[n][0] = r

    loss_rows = -(-D // (8 * LANE)) * 8
    parts = [_pack([loss_cols[0]])] + [_pack([jnp.stack(small_g[n]) for n in SMALL])]
    tot, loss = _allreduce_small(jnp.concatenate(parts), loss_rows, 0.5 / D)
    wvec = jnp.concatenate([jnp.zeros((loss_rows, LANE), F32), _pack([small_w[n] for n in SMALL])])
    mvec = jnp.concatenate([jnp.zeros((loss_rows, LANE), F32), _pack([small_m[n] for n in SMALL])])
    vvec = jnp.concatenate([jnp.ones((loss_rows, LANE), F32), _pack([small_v[n] for n in SMALL])])
    small_out = [tot] + _adamw("adamw_small", wvec, tot, mvec, vvec)

    def unpack(vec):
        flat = vec[loss_rows:].reshape(-1)
        res, pos = {}, 0
        for n in SMALL:
            size = int(np.prod(small_w[n].shape))
            res[n] = flat[pos:pos + size].reshape(small_w[n].shape)
            pos += size
        return res

    small_res = [unpack(v) for v in small_out]

    big_res = [{}, {}, {}, {}]
    for n in BIG:
        g = jnp.stack(big_g[n])
        big_res[0][n] = g
        for k, val in enumerate(_adamw("adamw_" + n, big_w[n], g, big_m[n], big_v[n])):
            big_res[k + 1][n] = val

    order = (("pre_g", small_res), ("w_in", big_res), ("b_gate", small_res), ("q_g", small_res),
             ("k_g", small_res), ("rpb", small_res), ("w_a", big_res), ("w_b", big_res), ("w_c", big_res),
             ("w_out", big_res), ("post_g", small_res))
    outs = [loss[0, 0], dy[None]]
    for k in range(4):
        outs.extend(src[k][n] for n, src in order)
    return tuple(outs)
```
